```python
import math
import jax, jax.numpy as jnp
from jax import lax
import numpy as np

D_MODEL = 1024
BATCH = 8
SEQ = 16384
DEPTH = 4

N_MIXERS = 4
N_MEM = 256
HEAD_DIM = 64
MEM_HEADS = 4
MEM_WIDTH = MEM_HEADS * HEAD_DIM
MLA_HEADS = 12
MLA_Q_LORA = 384
MLA_KV_LORA = 256
MLA_NOPE = 64
MLA_ROPE = 32
MLA_V = 64
ROPE_THETA = 10000.0
Q_BLOCK = 128
A_IN = MLA_Q_LORA + MLA_KV_LORA + MLA_ROPE + MEM_WIDTH
A_OUT = MLA_HEADS * MLA_V + MEM_WIDTH
DIL_GROUPS = ((128, 1), (512, 4), (2048, 16))
DIL_HEADS = 8
DIL_BLOCK = 64
DIL_QKV = len(DIL_GROUPS) * 3 * DIL_HEADS * HEAD_DIM
B_IN = DIL_QKV + MEM_WIDTH
B_OUT = DIL_HEADS * HEAD_DIM + MEM_WIDTH
ALIBI_MAX = 8.0
CONV_CH = 768
CONV_WIDTH = 31
C_IN = 2 * CONV_CH + MEM_WIDTH
C_OUT = CONV_CH + MEM_WIDTH
SC_CH = 768
SC_WIDTH = 3
D_IN = 3 * SC_CH + MEM_WIDTH
D_OUT = SC_CH + MEM_WIDTH
D_FF = 2816
EPS = 1e-6
NEG = -1e30

kernel_name = "hybrid_interleaved_mla_dilated_conformer_shortconv_encoder"


def _n_uses(m):
    return len(range(m, DEPTH, N_MIXERS))


def rms_norm(x, g):
    xf = x.astype(jnp.float32)
    y = xf * lax.rsqrt(jnp.mean(xf * xf, axis=-1, keepdims=True) + EPS)
    return (y * g.astype(jnp.float32)).astype(x.dtype)


def layer_norm(x, g, b):
    xf = x.astype(jnp.float32)
    mu = jnp.mean(xf, axis=-1, keepdims=True)
    var = jnp.mean(jnp.square(xf - mu), axis=-1, keepdims=True)
    y = (xf - mu) * lax.rsqrt(var + EPS)
    return (y * g.astype(jnp.float32) + b.astype(jnp.float32)).astype(x.dtype)


def swiglu(h, wg, wu, wd):
    return jnp.einsum('bsf,fd->bsd', jax.nn.silu(h @ wg) * (h @ wu), wd)


def depthwise_conv(u, w):
    width, ch = w.shape
    return lax.conv_general_dilated(
        u, w[:, None, :].astype(u.dtype), window_strides=(1,),
        padding=[(width // 2, width // 2)],
        dimension_numbers=('NWC', 'WIO', 'NWC'), feature_group_count=ch)


def alibi_slopes(n):
    return 2.0 ** (-ALIBI_MAX * (jnp.arange(n, dtype=jnp.float32) + 1.0) / n)


def apply_rope(t, cos, sin):
    if t.ndim == 4:
        cos, sin = cos[:, :, None, :], sin[:, :, None, :]
    half = t.shape[-1] // 2
    tf = t.astype(jnp.float32)
    t1, t2 = tf[..., :half], tf[..., half:]
    return jnp.concatenate([t1 * cos - t2 * sin, t1 * sin + t2 * cos], axis=-1).astype(t.dtype)


def memory_attention(q_mem, mem_n, w_kv):
    bsz, seq, _ = q_mem.shape
    kv = (mem_n @ w_kv).reshape(bsz, mem_n.shape[1], 2, MEM_HEADS, HEAD_DIM)
    k, v = kv[:, :, 0], kv[:, :, 1]
    q = q_mem.reshape(bsz, seq, MEM_HEADS, HEAD_DIM)
    s = jnp.einsum('bshd,bnhd->bhsn', q, k).astype(jnp.float32) * HEAD_DIM ** -0.5
    p = jax.nn.softmax(s, axis=-1).astype(v.dtype)
    return jnp.einsum('bhsn,bnhd->bshd', p, v).reshape(bsz, seq, MEM_WIDTH)


def mla_mixer(h, positions, w_in, q_g, kv_g, w_uq, w_ukv):
    bsz, seq, _ = h.shape
    z = h @ w_in
    o1 = MLA_Q_LORA
    o2 = o1 + MLA_KV_LORA
    o3 = o2 + MLA_ROPE
    c_q, c_kv, k_r, q_mem = z[..., :o1], z[..., o1:o2], z[..., o2:o3], z[..., o3:]
    half = MLA_ROPE // 2
    inv = ROPE_THETA ** (-jnp.arange(half, dtype=jnp.float32) / half)
    ang = positions.astype(jnp.float32)[..., None] * inv
    cos, sin = jnp.cos(ang), jnp.sin(ang)
    q = (rms_norm(c_q, q_g) @ w_uq).reshape(bsz, seq, MLA_HEADS, MLA_NOPE + MLA_ROPE)
    q_n, q_r = q[..., :MLA_NOPE], apply_rope(q[..., MLA_NOPE:], cos, sin)
    kv = (rms_norm(c_kv, kv_g) @ w_ukv).reshape(bsz, seq, MLA_HEADS, MLA_NOPE + MLA_V)
    k_n, v = kv[..., :MLA_NOPE], kv[..., MLA_NOPE:]
    k_r = apply_rope(k_r, cos, sin)
    scale = (MLA_NOPE + MLA_ROPE) ** -0.5
    nb = seq // Q_BLOCK
    qn_b = q_n.reshape(bsz, nb, Q_BLOCK, MLA_HEADS, MLA_NOPE).transpose(1, 0, 2, 3, 4)
    qr_b = q_r.reshape(bsz, nb, Q_BLOCK, MLA_HEADS, MLA_ROPE).transpose(1, 0, 2, 3, 4)

    def block(args):
        qn, qr = args
        s = (jnp.einsum('bqhd,bkhd->bhqk', qn, k_n)
             + jnp.einsum('bqhr,bkr->bhqk', qr, k_r)).astype(jnp.float32) * scale
        p = jax.nn.softmax(s, axis=-1).astype(v.dtype)
        return jnp.einsum('bhqk,bkhd->bqhd', p, v)

    o = lax.map(block, (qn_b, qr_b))
    o = o.transpose(1, 0, 2, 3, 4).reshape(bsz, seq, MLA_HEADS * MLA_V)
    return o, q_mem


def dilated_group(q, k, v, dilation, half, slopes):
    bsz, seq, nh, dh = q.shape
    sub_len = seq // dilation
    nb = -(-sub_len // DIL_BLOCK)
    lp = nb * DIL_BLOCK

    def sub(t):
        return t.reshape(bsz, sub_len, dilation, nh, dh).transpose(0, 2, 1, 3, 4)

    qs = jnp.pad(sub(q), ((0, 0), (0, 0), (0, lp - sub_len), (0, 0), (0, 0)))
    qs = qs.reshape(bsz, dilation, nb, DIL_BLOCK, nh, dh)

    def kv_blocks(t):
        t = jnp.pad(sub(t), ((0, 0), (0, 0), (DIL_BLOCK, lp - sub_len + DIL_BLOCK), (0, 0), (0, 0)))
        t = t.reshape(bsz, dilation, nb + 2, DIL_BLOCK, nh, dh)
        return jnp.concatenate([t[:, :, :-2], t[:, :, 1:-1], t[:, :, 2:]], axis=3)

    kb, vb = kv_blocks(k), kv_blocks(v)
    qi = jnp.arange(DIL_BLOCK)
    kj = jnp.arange(3 * DIL_BLOCK)
    rel = kj[None, :] - DIL_BLOCK - qi[:, None]
    key_idx = jnp.arange(nb)[:, None] * DIL_BLOCK - DIL_BLOCK + kj[None, :]
    mask = (jnp.abs(rel) <= half)[None] & ((key_idx >= 0) & (key_idx < sub_len))[:, None, :]
    dist = (jnp.abs(rel) * dilation).astype(jnp.float32)
    s = jnp.einsum('brnqhd,brnkhd->brnhqk', qs, kb).astype(jnp.float32) * dh ** -0.5
    s = s - slopes[:, None, None] * dist
    s = jnp.where(mask[:, None], s, NEG)
    lse = jax.nn.logsumexp(s, axis=-1)
    p = jnp.exp(s - lse[..., None]).astype(v.dtype)
    o = jnp.einsum('brnhqk,brnkhd->brnqhd', p, vb)
    o = o.reshape(bsz, dilation, lp, nh, dh)[:, :, :sub_len]
    o = o.transpose(0, 2, 1, 3, 4).reshape(bsz, seq, nh, dh)
    lse = lse.transpose(0, 1, 2, 4, 3).reshape(bsz, dilation, lp, nh)[:, :, :sub_len]
    lse = lse.transpose(0, 2, 1, 3).reshape(bsz, seq, nh)
    return o, lse


def dilated_mixer(h, w_in):
    bsz, seq, _ = h.shape
    z = h @ w_in
    n_g = len(DIL_GROUPS)
    zd = z[..., :DIL_QKV].reshape(bsz, seq, n_g, 3, DIL_HEADS, HEAD_DIM)
    q_mem = z[..., DIL_QKV:]
    slopes = alibi_slopes(n_g * DIL_HEADS).reshape(n_g, DIL_HEADS)
    outs, lses = [], []
    for g, (window, dilation) in enumerate(DIL_GROUPS):
        o, l = dilated_group(zd[:, :, g, 0], zd[:, :, g, 1], zd[:, :, g, 2],
                             dilation, window // (2 * dilation), slopes[g])
        outs.append(o)
        lses.append(l)
    wts = jax.nn.softmax(jnp.stack(lses, axis=0), axis=0)
    o = jnp.einsum('gbsh,gbshd->bshd', wts.astype(outs[0].dtype), jnp.stack(outs, axis=0))
    return o.reshape(bsz, seq, DIL_HEADS * HEAD_DIM), q_mem


def conformer_conv_mixer(h, w_in, conv_w, conv_b, ln_g, ln_b):
    z = h @ w_in
    a, gate, q_mem = z[..., :CONV_CH], z[..., CONV_CH:2 * CONV_CH], z[..., 2 * CONV_CH:]
    u = a * jax.nn.sigmoid(gate)
    u = depthwise_conv(u, conv_w) + conv_b
    u = jax.nn.silu(layer_norm(u, ln_g, ln_b))
    return u, q_mem


def short_conv_mixer(h, w_in, conv_w):
    z = h @ w_in
    bg, cg = z[..., :SC_CH], z[..., SC_CH:2 * SC_CH]
    hx, q_mem = z[..., 2 * SC_CH:3 * SC_CH], z[..., 3 * SC_CH:]
    return bg * depthwise_conv(cg * hx, conv_w), q_mem


def _fwd_setup_inputs(seed: int = 0) -> dict:
    key = jax.random.key(seed)
    ks = jax.random.split(key, 32)
    f32 = jnp.float32

    def w(k, shape, fan_in):
        return jax.random.normal(k, shape, f32) * fan_in ** -0.5

    def gain(k, shape):
        return 1.0 + 0.05 * jax.random.normal(k, shape, f32)

    na, nb_, nc, nd = _n_uses(0), _n_uses(1), _n_uses(2), _n_uses(3)
    offset = jax.random.randint(ks[2], (BATCH, 1), 0, 4096, dtype=jnp.int32)
    positions = offset + jnp.arange(SEQ, dtype=jnp.int32)[None, :]
    return {
        "x": jax.random.normal(ks[0], (BATCH, SEQ, D_MODEL), f32),
        "mem": jax.random.normal(ks[1], (BATCH, N_MEM, D_MODEL), f32),
        "positions": positions,
        "norm_g": gain(ks[3], (DEPTH, 7, D_MODEL)),
        "ffn_w_gate": w(ks[4], (DEPTH, 2, D_MODEL, D_FF), D_MODEL),
        "ffn_w_up": w(ks[5], (DEPTH, 2, D_MODEL, D_FF), D_MODEL),
        "ffn_w_down": w(ks[6], (DEPTH, 2, D_FF, D_MODEL), D_FF),
        "mem_w_kv": w(ks[7], (DEPTH, D_MODEL, 2 * MEM_WIDTH), D_MODEL),
        "a_w_in": w(ks[8], (na, D_MODEL, A_IN), D_MODEL),
        "a_q_norm": gain(ks[9], (na, MLA_Q_LORA)),
        "a_kv_norm": gain(ks[10], (na, MLA_KV_LORA)),
        "a_w_uq": w(ks[11], (na, MLA_Q_LORA, MLA_HEADS * (MLA_NOPE + MLA_ROPE)), MLA_Q_LORA),
        "a_w_ukv": w(ks[12], (na, MLA_KV_LORA, MLA_HEADS * (MLA_NOPE + MLA_V)), MLA_KV_LORA),
        "a_w_out": w(ks[13], (na, A_OUT, D_MODEL), A_OUT),
        "b_w_in": w(ks[14], (nb_, D_MODEL, B_IN), D_MODEL),
        "b_w_out": w(ks[15], (nb_, B_OUT, D_MODEL), B_OUT),
        "c_w_in": w(ks[16], (nc, D_MODEL, C_IN), D_MODEL),
        "c_conv_w": w(ks[17], (nc, CONV_WIDTH, CONV_CH), CONV_WIDTH),
        "c_conv_b": 0.02 * jax.random.normal(ks[18], (nc, CONV_CH), f32),
        "c_ln_g": gain(ks[19], (nc, CONV_CH)),
        "c_ln_b": 0.02 * jax.random.normal(ks[20], (nc, CONV_CH), f32),
        "c_w_out": w(ks[21], (nc, C_OUT, D_MODEL), C_OUT),
        "d_w_in": w(ks[22], (nd, D_MODEL, D_IN), D_MODEL),
        "d_conv_w": w(ks[23], (nd, SC_WIDTH, SC_CH), SC_WIDTH),
        "d_w_out": w(ks[24], (nd, D_OUT, D_MODEL), D_OUT),
    }


def _fwd_reference(x, mem, positions, norm_g, ffn_w_gate, ffn_w_up, ffn_w_down, mem_w_kv,
              a_w_in, a_q_norm, a_kv_norm, a_w_uq, a_w_ukv, a_w_out,
              b_w_in, b_w_out,
              c_w_in, c_conv_w, c_conv_b, c_ln_g, c_ln_b, c_w_out,
              d_w_in, d_conv_w, d_w_out):
    for i in range(DEPTH):
        g = norm_g[i]
        h = rms_norm(x, g[0])
        x = x + 0.5 * rms_norm(swiglu(h, ffn_w_gate[i, 0], ffn_w_up[i, 0], ffn_w_down[i, 0]), g[1])
        h = rms_norm(x, g[2])
        m, j = i % N_MIXERS, i // N_MIXERS
        if m == 0:
            o, q_mem = mla_mixer(h, positions, a_w_in[j], a_q_norm[j], a_kv_norm[j], a_w_uq[j], a_w_ukv[j])
            w_out = a_w_out[j]
        elif m == 1:
            o, q_mem = dilated_mixer(h, b_w_in[j])
            w_out = b_w_out[j]
        elif m == 2:
            o, q_mem = conformer_conv_mixer(h, c_w_in[j], c_conv_w[j], c_conv_b[j], c_ln_g[j], c_ln_b[j])
            w_out = c_w_out[j]
        else:
            o, q_mem = short_conv_mixer(h, d_w_in[j], d_conv_w[j])
            w_out = d_w_out[j]
        mo = memory_attention(q_mem, rms_norm(mem, g[6]), mem_w_kv[i])
        y = jnp.concatenate([o, mo], axis=-1) @ w_out
        x = x + rms_norm(y, g[3])
        h = rms_norm(x, g[4])
        x = x + 0.5 * rms_norm(swiglu(h, ffn_w_gate[i, 1], ffn_w_up[i, 1], ffn_w_down[i, 1]), g[5])
    return x


import jax as _jax
import jax.numpy as _jnp

TWIN_FORMAT = 'train_step'
FWD_PARAMS = ['x', 'mem', 'positions', 'norm_g', 'ffn_w_gate', 'ffn_w_up', 'ffn_w_down', 'mem_w_kv', 'a_w_in', 'a_q_norm', 'a_kv_norm', 'a_w_uq', 'a_w_ukv', 'a_w_out', 'b_w_in', 'b_w_out', 'c_w_in', 'c_conv_w', 'c_conv_b', 'c_ln_g', 'c_ln_b', 'c_w_out', 'd_w_in', 'd_conv_w', 'd_w_out']
TWIN_WEIGHTS = ['norm_g', 'ffn_w_gate', 'ffn_w_up', 'ffn_w_down', 'mem_w_kv', 'a_w_in', 'a_q_norm', 'a_kv_norm', 'a_w_uq', 'a_w_ukv', 'a_w_out', 'b_w_in', 'b_w_out', 'c_w_in', 'c_conv_w', 'c_conv_b', 'c_ln_g', 'c_ln_b', 'c_w_out', 'd_w_in', 'd_conv_w', 'd_w_out']
TWIN_DIFF_INPUT = 'x'
TWIN_INPUTS = ['x', 'mem', 'positions', 'norm_g', 'ffn_w_gate', 'ffn_w_up', 'ffn_w_down', 'mem_w_kv', 'a_w_in', 'a_q_norm', 'a_kv_norm', 'a_w_uq', 'a_w_ukv', 'a_w_out', 'b_w_in', 'b_w_out', 'c_w_in', 'c_conv_w', 'c_conv_b', 'c_ln_g', 'c_ln_b', 'c_w_out', 'd_w_in', 'd_conv_w', 'd_w_out', 'loss_target', 'm_norm_g', 'm_ffn_w_gate', 'm_ffn_w_up', 'm_ffn_w_down', 'm_mem_w_kv', 'm_a_w_in', 'm_a_q_norm', 'm_a_kv_norm', 'm_a_w_uq', 'm_a_w_ukv', 'm_a_w_out', 'm_b_w_in', 'm_b_w_out', 'm_c_w_in', 'm_c_conv_w', 'm_c_conv_b', 'm_c_ln_g', 'm_c_ln_b', 'm_c_w_out', 'm_d_w_in', 'm_d_conv_w', 'm_d_w_out', 'v_norm_g', 'v_ffn_w_gate', 'v_ffn_w_up', 'v_ffn_w_down', 'v_mem_w_kv', 'v_a_w_in', 'v_a_q_norm', 'v_a_kv_norm', 'v_a_w_uq', 'v_a_w_ukv', 'v_a_w_out', 'v_b_w_in', 'v_b_w_out', 'v_c_w_in', 'v_c_conv_w', 'v_c_conv_b', 'v_c_ln_g', 'v_c_ln_b', 'v_c_w_out', 'v_d_w_in', 'v_d_conv_w', 'v_d_w_out']
TWIN_OUTPUTS = ['loss', 'grad_x', 'grad_norm_g', 'grad_ffn_w_gate', 'grad_ffn_w_up', 'grad_ffn_w_down', 'grad_mem_w_kv', 'grad_a_w_in', 'grad_a_q_norm', 'grad_a_kv_norm', 'grad_a_w_uq', 'grad_a_w_ukv', 'grad_a_w_out', 'grad_b_w_in', 'grad_b_w_out', 'grad_c_w_in', 'grad_c_conv_w', 'grad_c_conv_b', 'grad_c_ln_g', 'grad_c_ln_b', 'grad_c_w_out', 'grad_d_w_in', 'grad_d_conv_w', 'grad_d_w_out', 'delta_norm_g', 'delta_ffn_w_gate', 'delta_ffn_w_up', 'delta_ffn_w_down', 'delta_mem_w_kv', 'delta_a_w_in', 'delta_a_q_norm', 'delta_a_kv_norm', 'delta_a_w_uq', 'delta_a_w_ukv', 'delta_a_w_out', 'delta_b_w_in', 'delta_b_w_out', 'delta_c_w_in', 'delta_c_conv_w', 'delta_c_conv_b', 'delta_c_ln_g', 'delta_c_ln_b', 'delta_c_w_out', 'delta_d_w_in', 'delta_d_conv_w', 'delta_d_w_out', 'new_m_norm_g', 'new_m_ffn_w_gate', 'new_m_ffn_w_up', 'new_m_ffn_w_down', 'new_m_mem_w_kv', 'new_m_a_w_in', 'new_m_a_q_norm', 'new_m_a_kv_norm', 'new_m_a_w_uq', 'new_m_a_w_ukv', 'new_m_a_w_out', 'new_m_b_w_in', 'new_m_b_w_out', 'new_m_c_w_in', 'new_m_c_conv_w', 'new_m_c_conv_b', 'new_m_c_ln_g', 'new_m_c_ln_b', 'new_m_c_w_out', 'new_m_d_w_in', 'new_m_d_conv_w', 'new_m_d_w_out', 'new_v_norm_g', 'new_v_ffn_w_gate', 'new_v_ffn_w_up', 'new_v_ffn_w_down', 'new_v_mem_w_kv', 'new_v_a_w_in', 'new_v_a_q_norm', 'new_v_a_kv_norm', 'new_v_a_w_uq', 'new_v_a_w_ukv', 'new_v_a_w_out', 'new_v_b_w_in', 'new_v_b_w_out', 'new_v_c_w_in', 'new_v_c_conv_w', 'new_v_c_conv_b', 'new_v_c_ln_g', 'new_v_c_ln_b', 'new_v_c_w_out', 'new_v_d_w_in', 'new_v_d_conv_w', 'new_v_d_w_out']
TWIN_LEAF_KINDS = {'loss': 'loss', 'grad_x': 'grad_x', 'grad_norm_g': 'grad_w', 'grad_ffn_w_gate': 'grad_w', 'grad_ffn_w_up': 'grad_w', 'grad_ffn_w_down': 'grad_w', 'grad_mem_w_kv': 'grad_w', 'grad_a_w_in': 'grad_w', 'grad_a_q_norm': 'grad_w', 'grad_a_kv_norm': 'grad_w', 'grad_a_w_uq': 'grad_w', 'grad_a_w_ukv': 'grad_w', 'grad_a_w_out': 'grad_w', 'grad_b_w_in': 'grad_w', 'grad_b_w_out': 'grad_w', 'grad_c_w_in': 'grad_w', 'grad_c_conv_w': 'grad_w', 'grad_c_conv_b': 'grad_w', 'grad_c_ln_g': 'grad_w', 'grad_c_ln_b': 'grad_w', 'grad_c_w_out': 'grad_w', 'grad_d_w_in': 'grad_w', 'grad_d_conv_w': 'grad_w', 'grad_d_w_out': 'grad_w', 'delta_norm_g': 'delta_w', 'delta_ffn_w_gate': 'delta_w', 'delta_ffn_w_up': 'delta_w', 'delta_ffn_w_down': 'delta_w', 'delta_mem_w_kv': 'delta_w', 'delta_a_w_in': 'delta_w', 'delta_a_q_norm': 'delta_w', 'delta_a_kv_norm': 'delta_w', 'delta_a_w_uq': 'delta_w', 'delta_a_w_ukv': 'delta_w', 'delta_a_w_out': 'delta_w', 'delta_b_w_in': 'delta_w', 'delta_b_w_out': 'delta_w', 'delta_c_w_in': 'delta_w', 'delta_c_conv_w': 'delta_w', 'delta_c_conv_b': 'delta_w', 'delta_c_ln_g': 'delta_w', 'delta_c_ln_b': 'delta_w', 'delta_c_w_out': 'delta_w', 'delta_d_w_in': 'delta_w', 'delta_d_conv_w': 'delta_w', 'delta_d_w_out': 'delta_w', 'new_m_norm_g': 'new_m', 'new_m_ffn_w_gate': 'new_m', 'new_m_ffn_w_up': 'new_m', 'new_m_ffn_w_down': 'new_m', 'new_m_mem_w_kv': 'new_m', 'new_m_a_w_in': 'new_m', 'new_m_a_q_norm': 'new_m', 'new_m_a_kv_norm': 'new_m', 'new_m_a_w_uq': 'new_m', 'new_m_a_w_ukv': 'new_m', 'new_m_a_w_out': 'new_m', 'new_m_b_w_in': 'new_m', 'new_m_b_w_out': 'new_m', 'new_m_c_w_in': 'new_m', 'new_m_c_conv_w': 'new_m', 'new_m_c_conv_b': 'new_m', 'new_m_c_ln_g': 'new_m', 'new_m_c_ln_b': 'new_m', 'new_m_c_w_out': 'new_m', 'new_m_d_w_in': 'new_m', 'new_m_d_conv_w': 'new_m', 'new_m_d_w_out': 'new_m', 'new_v_norm_g': 'new_v', 'new_v_ffn_w_gate': 'new_v', 'new_v_ffn_w_up': 'new_v', 'new_v_ffn_w_down': 'new_v', 'new_v_mem_w_kv': 'new_v', 'new_v_a_w_in': 'new_v', 'new_v_a_q_norm': 'new_v', 'new_v_a_kv_norm': 'new_v', 'new_v_a_w_uq': 'new_v', 'new_v_a_w_ukv': 'new_v', 'new_v_a_w_out': 'new_v', 'new_v_b_w_in': 'new_v', 'new_v_b_w_out': 'new_v', 'new_v_c_w_in': 'new_v', 'new_v_c_conv_w': 'new_v', 'new_v_c_conv_b': 'new_v', 'new_v_c_ln_g': 'new_v', 'new_v_c_ln_b': 'new_v', 'new_v_c_w_out': 'new_v', 'new_v_d_w_in': 'new_v', 'new_v_d_conv_w': 'new_v', 'new_v_d_w_out': 'new_v'}


def _forward(args):
    return _fwd_reference(*[args[k] for k in FWD_PARAMS])


def _output_shape():
    def fwd():
        inp = _fwd_setup_inputs(0)
        return _fwd_reference(*[inp[k] for k in FWD_PARAMS])
    out = _jax.eval_shape(fwd)
    return out.shape, out.dtype

N_MICROBATCH = 1
ADAM_LR = 0.001
ADAM_B1 = 0.9
ADAM_B2 = 0.999
ADAM_EPS = 1e-08
ADAM_WD = 0.01
ADAM_STEP = 10
PER_EXAMPLE_BATCH_AXIS = {'x': 0, 'mem': 0, 'positions': 0, 'loss_target': 0}
SHARED_INPUTS = []
_WEIGHT_DTYPES = {'norm_g': _jnp.float32, 'ffn_w_gate': _jnp.float32, 'ffn_w_up': _jnp.float32, 'ffn_w_down': _jnp.float32, 'mem_w_kv': _jnp.float32, 'a_w_in': _jnp.float32, 'a_q_norm': _jnp.float32, 'a_kv_norm': _jnp.float32, 'a_w_uq': _jnp.float32, 'a_w_ukv': _jnp.float32, 'a_w_out': _jnp.float32, 'b_w_in': _jnp.float32, 'b_w_out': _jnp.float32, 'c_w_in': _jnp.float32, 'c_conv_w': _jnp.float32, 'c_conv_b': _jnp.float32, 'c_ln_g': _jnp.float32, 'c_ln_b': _jnp.float32, 'c_w_out': _jnp.float32, 'd_w_in': _jnp.float32, 'd_conv_w': _jnp.float32, 'd_w_out': _jnp.float32}
MOMENT_SCALE = {'norm_g': 5.247105e+01, 'ffn_w_gate': 2.402333e+00, 'ffn_w_up': 2.955713e+00, 'ffn_w_down': 4.888508e+00, 'mem_w_kv': 1.846714e+01, 'a_w_in': 1.269334e+01, 'a_q_norm': 3.906493e+00, 'a_kv_norm': 3.810355e+01, 'a_w_uq': 2.234080e+00, 'a_w_ukv': 9.784123e+00, 'a_w_out': 2.684362e+01, 'b_w_in': 9.401470e+00, 'b_w_out': 2.324185e+01, 'c_w_in': 7.516352e+00, 'c_conv_w': 1.324097e+01, 'c_conv_b': 8.503196e+01, 'c_ln_g': 3.301554e+01, 'c_ln_b': 4.610222e+01, 'c_w_out': 1.994584e+01, 'd_w_in': 2.029061e+00, 'd_conv_w': 2.158951e+00, 'd_w_out': 2.191916e+00}


def _to_microbatches(a, axis):
    t = _jnp.moveaxis(a, axis, 0)
    t = t.reshape((N_MICROBATCH, t.shape[0] // N_MICROBATCH) + t.shape[1:])
    return _jnp.moveaxis(t, 1, axis + 1)


def setup_inputs(seed: int = 0) -> dict:
    inp = _fwd_setup_inputs(seed)
    key = _jax.random.fold_in(_jax.random.key(seed), 7919)
    shape, _ = _output_shape()
    out = dict(inp)
    out["loss_target"] = _jax.random.normal(_jax.random.fold_in(key, 0), shape, _jnp.float32)
    for i, name in enumerate(TWIN_WEIGHTS):
        w = inp[name].astype(_jnp.float32)
        if MOMENT_SCALE is None:
            s = _jnp.sqrt(_jnp.mean(_jnp.square(w)) + 1e-30)
        else:
            s = MOMENT_SCALE[name]
        km, kv = _jax.random.split(_jax.random.fold_in(key, i + 1))
        out[name] = w
        out["m_" + name] = s * _jax.random.normal(km, w.shape, _jnp.float32)
        out["v_" + name] = (s * s) * _jax.random.uniform(kv, w.shape, _jnp.float32, 0.5, 1.5)
    if N_MICROBATCH > 1:
        for name, axis in PER_EXAMPLE_BATCH_AXIS.items():
            out[name] = _to_microbatches(out[name], axis)
    return {'x': out['x'], 'mem': out['mem'], 'positions': out['positions'], 'norm_g': out['norm_g'], 'ffn_w_gate': out['ffn_w_gate'], 'ffn_w_up': out['ffn_w_up'], 'ffn_w_down': out['ffn_w_down'], 'mem_w_kv': out['mem_w_kv'], 'a_w_in': out['a_w_in'], 'a_q_norm': out['a_q_norm'], 'a_kv_norm': out['a_kv_norm'], 'a_w_uq': out['a_w_uq'], 'a_w_ukv': out['a_w_ukv'], 'a_w_out': out['a_w_out'], 'b_w_in': out['b_w_in'], 'b_w_out': out['b_w_out'], 'c_w_in': out['c_w_in'], 'c_conv_w': out['c_conv_w'], 'c_conv_b': out['c_conv_b'], 'c_ln_g': out['c_ln_g'], 'c_ln_b': out['c_ln_b'], 'c_w_out': out['c_w_out'], 'd_w_in': out['d_w_in'], 'd_conv_w': out['d_conv_w'], 'd_w_out': out['d_w_out'], 'loss_target': out['loss_target'], 'm_norm_g': out['m_norm_g'], 'm_ffn_w_gate': out['m_ffn_w_gate'], 'm_ffn_w_up': out['m_ffn_w_up'], 'm_ffn_w_down': out['m_ffn_w_down'], 'm_mem_w_kv': out['m_mem_w_kv'], 'm_a_w_in': out['m_a_w_in'], 'm_a_q_norm': out['m_a_q_norm'], 'm_a_kv_norm': out['m_a_kv_norm'], 'm_a_w_uq': out['m_a_w_uq'], 'm_a_w_ukv': out['m_a_w_ukv'], 'm_a_w_out': out['m_a_w_out'], 'm_b_w_in': out['m_b_w_in'], 'm_b_w_out': out['m_b_w_out'], 'm_c_w_in': out['m_c_w_in'], 'm_c_conv_w': out['m_c_conv_w'], 'm_c_conv_b': out['m_c_conv_b'], 'm_c_ln_g': out['m_c_ln_g'], 'm_c_ln_b': out['m_c_ln_b'], 'm_c_w_out': out['m_c_w_out'], 'm_d_w_in': out['m_d_w_in'], 'm_d_conv_w': out['m_d_conv_w'], 'm_d_w_out': out['m_d_w_out'], 'v_norm_g': out['v_norm_g'], 'v_ffn_w_gate': out['v_ffn_w_gate'], 'v_ffn_w_up': out['v_ffn_w_up'], 'v_ffn_w_down': out['v_ffn_w_down'], 'v_mem_w_kv': out['v_mem_w_kv'], 'v_a_w_in': out['v_a_w_in'], 'v_a_q_norm': out['v_a_q_norm'], 'v_a_kv_norm': out['v_a_kv_norm'], 'v_a_w_uq': out['v_a_w_uq'], 'v_a_w_ukv': out['v_a_w_ukv'], 'v_a_w_out': out['v_a_w_out'], 'v_b_w_in': out['v_b_w_in'], 'v_b_w_out': out['v_b_w_out'], 'v_c_w_in': out['v_c_w_in'], 'v_c_conv_w': out['v_c_conv_w'], 'v_c_conv_b': out['v_c_conv_b'], 'v_c_ln_g': out['v_c_ln_g'], 'v_c_ln_b': out['v_c_ln_b'], 'v_c_w_out': out['v_c_w_out'], 'v_d_w_in': out['v_d_w_in'], 'v_d_conv_w': out['v_d_conv_w'], 'v_d_w_out': out['v_d_w_out']}


def _loss(weights, diff, rest, loss_target):
    with _jax.named_scope("forward"):
        args = {**rest, TWIN_DIFF_INPUT: diff, **{k: w.astype(_WEIGHT_DTYPES[k]) for k, w in weights.items()}}
        y = _forward(args)
    with _jax.named_scope("loss_head"):
        err = _jnp.square(y.astype(_jnp.float32) - loss_target)
        return 0.5 * _jnp.sum(_jnp.mean(err, axis=-1)) if err.ndim else 0.5 * err


def _adamw(w, g, m, v):
    m = ADAM_B1 * m + (1.0 - ADAM_B1) * g
    v = ADAM_B2 * v + (1.0 - ADAM_B2) * _jnp.square(g)
    m_hat = m / (1.0 - ADAM_B1 ** ADAM_STEP)
    v_hat = v / (1.0 - ADAM_B2 ** ADAM_STEP)
    delta = -ADAM_LR * (m_hat / (_jnp.sqrt(v_hat) + ADAM_EPS) + ADAM_WD * w)
    return delta, m, v


def reference(x, mem, positions, norm_g, ffn_w_gate, ffn_w_up, ffn_w_down, mem_w_kv, a_w_in, a_q_norm, a_kv_norm, a_w_uq, a_w_ukv, a_w_out, b_w_in, b_w_out, c_w_in, c_conv_w, c_conv_b, c_ln_g, c_ln_b, c_w_out, d_w_in, d_conv_w, d_w_out, loss_target, m_norm_g, m_ffn_w_gate, m_ffn_w_up, m_ffn_w_down, m_mem_w_kv, m_a_w_in, m_a_q_norm, m_a_kv_norm, m_a_w_uq, m_a_w_ukv, m_a_w_out, m_b_w_in, m_b_w_out, m_c_w_in, m_c_conv_w, m_c_conv_b, m_c_ln_g, m_c_ln_b, m_c_w_out, m_d_w_in, m_d_conv_w, m_d_w_out, v_norm_g, v_ffn_w_gate, v_ffn_w_up, v_ffn_w_down, v_mem_w_kv, v_a_w_in, v_a_q_norm, v_a_kv_norm, v_a_w_uq, v_a_w_ukv, v_a_w_out, v_b_w_in, v_b_w_out, v_c_w_in, v_c_conv_w, v_c_conv_b, v_c_ln_g, v_c_ln_b, v_c_w_out, v_d_w_in, v_d_conv_w, v_d_w_out):
    given = dict(x=x, mem=mem, positions=positions, norm_g=norm_g, ffn_w_gate=ffn_w_gate, ffn_w_up=ffn_w_up, ffn_w_down=ffn_w_down, mem_w_kv=mem_w_kv, a_w_in=a_w_in, a_q_norm=a_q_norm, a_kv_norm=a_kv_norm, a_w_uq=a_w_uq, a_w_ukv=a_w_ukv, a_w_out=a_w_out, b_w_in=b_w_in, b_w_out=b_w_out, c_w_in=c_w_in, c_conv_w=c_conv_w, c_conv_b=c_conv_b, c_ln_g=c_ln_g, c_ln_b=c_ln_b, c_w_out=c_w_out, d_w_in=d_w_in, d_conv_w=d_conv_w, d_w_out=d_w_out, loss_target=loss_target, m_norm_g=m_norm_g, m_ffn_w_gate=m_ffn_w_gate, m_ffn_w_up=m_ffn_w_up, m_ffn_w_down=m_ffn_w_down, m_mem_w_kv=m_mem_w_kv, m_a_w_in=m_a_w_in, m_a_q_norm=m_a_q_norm, m_a_kv_norm=m_a_kv_norm, m_a_w_uq=m_a_w_uq, m_a_w_ukv=m_a_w_ukv, m_a_w_out=m_a_w_out, m_b_w_in=m_b_w_in, m_b_w_out=m_b_w_out, m_c_w_in=m_c_w_in, m_c_conv_w=m_c_conv_w, m_c_conv_b=m_c_conv_b, m_c_ln_g=m_c_ln_g, m_c_ln_b=m_c_ln_b, m_c_w_out=m_c_w_out, m_d_w_in=m_d_w_in, m_d_conv_w=m_d_conv_w, m_d_w_out=m_d_w_out, v_norm_g=v_norm_g, v_ffn_w_gate=v_ffn_w_gate, v_ffn_w_up=v_ffn_w_up, v_ffn_w_down=v_ffn_w_down, v_mem_w_kv=v_mem_w_kv, v_a_w_in=v_a_w_in, v_a_q_norm=v_a_q_norm, v_a_kv_norm=v_a_kv_norm, v_a_w_uq=v_a_w_uq, v_a_w_ukv=v_a_w_ukv, v_a_w_out=v_a_w_out, v_b_w_in=v_b_w_in, v_b_w_out=v_b_w_out, v_c_w_in=v_c_w_in, v_c_conv_w=v_c_conv_w, v_c_conv_b=v_c_conv_b, v_c_ln_g=v_c_ln_g, v_c_ln_b=v_c_ln_b, v_c_w_out=v_c_w_out, v_d_w_in=v_d_w_in, v_d_conv_w=v_d_conv_w, v_d_w_out=v_d_w_out)
    weights = {n: given[n] for n in TWIN_WEIGHTS}
    shared = {n: given[n] for n in SHARED_INPUTS}
    per_example = {n: given[n] for n in ['x', 'mem', 'positions']}
    grad_fn = _jax.value_and_grad(_loss, argnums=(0, 1))

    def one_microbatch(ex, loss_target):
        ex = dict(ex)
        diff = ex.pop(TWIN_DIFF_INPUT)
        return grad_fn(weights, diff, {**shared, **ex}, loss_target)

    if N_MICROBATCH == 1:
        loss, (grad_w, grad_x) = one_microbatch(per_example, given["loss_target"])
    else:
        def body(carry, xs):
            loss_sum, grad_sum = carry
            l_k, (gw_k, gx_k) = one_microbatch(xs[0], xs[1])
            with _jax.named_scope("update"):
                return (loss_sum + l_k, _jax.tree.map(_jnp.add, grad_sum, gw_k)), gx_k

        init = (_jnp.zeros((), _jnp.float32), _jax.tree.map(_jnp.zeros_like, weights))
        (loss, grad_w), grad_x = _jax.lax.scan(body, init, (per_example, given["loss_target"]))
    with _jax.named_scope("update"):
        delta_w, new_m, new_v = {}, {}, {}
        for n in TWIN_WEIGHTS:
            delta_w[n], new_m[n], new_v[n] = _adamw(weights[n], grad_w[n], given["m_" + n], given["v_" + n])
    return (loss, grad_x, *[grad_w[n] for n in TWIN_WEIGHTS], *[delta_w[n] for n in TWIN_WEIGHTS],
            *[new_m[n] for n in TWIN_WEIGHTS], *[new_v[n] for n in TWIN_WEIGHTS])
```

```python
import functools
import math

import jax
import jax.numpy as jnp
from jax import lax
from jax.experimental import pallas as pl
from jax.experimental.pallas import tpu as pltpu

BF, F32 = jnp.bfloat16, jnp.float32
SDS = jax.ShapeDtypeStruct

D_MODEL, DEPTH, D_FF = 1024, 4, 2816
HEAD_DIM, MEM_HEADS, MEM_WIDTH, N_MEM = 64, 4, 256, 256
MLA_HEADS, MLA_Q_LORA, MLA_KV_LORA, MLA_NOPE, MLA_ROPE, MLA_V = 12, 384, 256, 64, 32, 64
MLA_QK = MLA_NOPE + MLA_ROPE
ROPE_THETA = 10000.0
DIL_GROUPS = ((128, 1), (512, 4), (2048, 16))
DIL_HEADS, DIL_HALF = 8, 64
DIL_QKV = len(DIL_GROUPS) * 3 * DIL_HEADS * HEAD_DIM
ALIBI_MAX = 8.0
CONV_CH, CONV_WIDTH, SC_CH, SC_WIDTH = 768, 31, 768, 3
EPS, NEG = 1e-6, -1e30
ADAM_LR, ADAM_B1, ADAM_B2, ADAM_EPS, ADAM_WD, ADAM_STEP = 0.001, 0.9, 0.999, 1e-08, 0.01, 10

N_DEV = 8
MESH_AXES = ("x", "y", "c")
V7X_VMEM_LIMIT = 48 * 1024 * 1024
HALO = 16
PACK_W = 1024

WEIGHTS = ['norm_g', 'ffn_w_gate', 'ffn_w_up', 'ffn_w_down', 'mem_w_kv', 'a_w_in', 'a_q_norm', 'a_kv_norm',
           'a_w_uq', 'a_w_ukv', 'a_w_out', 'b_w_in', 'b_w_out', 'c_w_in', 'c_conv_w', 'c_conv_b', 'c_ln_g',
           'c_ln_b', 'c_w_out', 'd_w_in', 'd_conv_w', 'd_w_out']
SHARD_AXIS = dict(norm_g=2, ffn_w_gate=3, ffn_w_up=3, ffn_w_down=2, mem_w_kv=1, a_w_in=1, a_q_norm=None,
                  a_kv_norm=None, a_w_uq=2, a_w_ukv=2, a_w_out=1, b_w_in=2, b_w_out=2, c_w_in=2, c_conv_w=2,
                  c_conv_b=1, c_ln_g=1, c_ln_b=1, c_w_out=1, d_w_in=2, d_conv_w=2, d_w_out=1)
SMALL = ['norm_g', 'c_conv_w', 'c_conv_b', 'c_ln_g', 'c_ln_b', 'd_conv_w']
BIG = [n for n in WEIGHTS if n not in SMALL and SHARD_AXIS[n] is not None]
IN_NAMES = (['x', 'mem', 'positions'] + WEIGHTS + ['loss_target'] + ['m_' + n for n in WEIGHTS]
            + ['v_' + n for n in WEIGHTS])


def _params(sem):
    return pltpu.CompilerParams(dimension_semantics=sem, vmem_limit_bytes=V7X_VMEM_LIMIT)


def _dot(a, b, ca=1, cb=0):
    return lax.dot_general(a, b, (((ca,), (cb,)), ((), ())), preferred_element_type=F32)


def mm_call(name, accs, M, N, K, *, tm, tn, tk, epilogue, outs, tiles=(), cols=(), fulls=(), reds=()):
    tm, tn, tk = min(tm, M), min(tn, N), min(tk, K)
    assert M % tm == 0 and N % tn == 0 and K % tk == 0, (name, M, N, K, tm, tn, tk)
    gj, gi, gk = N // tn, M // tm, K // tk
    flat = [p for acc in accs for p in acc]
    in_specs, args = [], []
    for (a, b, ta, tb) in flat:
        in_specs.append(pl.BlockSpec((tk, tm), lambda j, i, k: (k, i)) if ta else pl.BlockSpec((tm, tk), lambda j, i, k: (i, k)))
        in_specs.append(pl.BlockSpec((tn, tk), lambda j, i, k: (j, k)) if tb else pl.BlockSpec((tk, tn), lambda j, i, k: (k, j)))
        args += [a, b]
    for t in tiles:
        in_specs.append(pl.BlockSpec((tm, tn), lambda j, i, k: (i, j)))
        args.append(t)
    for c in cols:
        in_specs.append(pl.BlockSpec((c.shape[0], tn), lambda j, i, k: (0, j)))
        args.append(c)
    for f in fulls:
        in_specs.append(pl.BlockSpec(f.shape, lambda j, i, k: (0,) * f.ndim))
        args.append(f)
    out_shape = [SDS((M, N), dt) for dt in outs] + [SDS((r, N), F32) for r in reds]
    out_specs = ([pl.BlockSpec((tm, tn), lambda j, i, k: (i, j)) for _ in outs]
                 + [pl.BlockSpec((r, tn), lambda j, i, k: (0, j)) for r in reds])
    n_in = len(args)
    n_acc = len(accs)

    def body(*refs):
        in_refs = refs[:n_in]
        out_refs = refs[n_in:n_in + len(outs)]
        red_refs = refs[n_in + len(outs):n_in + len(outs) + len(reds)]
        acc_refs = refs[n_in + len(outs) + len(reds):]
        i, k = pl.program_id(1), pl.program_id(2)
        sums, idx = [], 0
        for acc in accs:
            tot = None
            for (_, _, ta, tb) in acc:
                av = in_refs[2 * idx][...].astype(BF)
                bv = in_refs[2 * idx + 1][...].astype(BF)
                idx += 1
                d = _dot(av, bv, 0 if ta else 1, 1 if tb else 0)
                tot = d if tot is None else tot + d
            sums.append(tot)

        def finish(vals):
            p = 2 * len(flat)
            tv = [r[...] for r in in_refs[p:p + len(tiles)]]
            cv = [r[...] for r in in_refs[p + len(tiles):p + len(tiles) + len(cols)]]
            fv = [r[...] for r in in_refs[p + len(tiles) + len(cols):]]
            ov, rv = epilogue(vals, tv, cv, fv)
            for r, v in zip(out_refs, ov):
                r[...] = v.astype(r.dtype)
            for r, v in zip(red_refs, rv):
                @pl.when(i == 0)
                def _(r=r, v=v):
                    r[...] = v

                @pl.when(i > 0)
                def _(r=r, v=v):
                    r[...] += v

        if gk == 1:
            finish(sums)
        else:
            @pl.when(k == 0)
            def _():
                for r, v in zip(acc_refs, sums):
                    r[...] = v

            @pl.when(k > 0)
            def _():
                for r, v in zip(acc_refs, sums):
                    r[...] += v

            @pl.when(k == gk - 1)
            def _():
                finish([r[...] for r in acc_refs])

    scratch = [] if gk == 1 else [pltpu.VMEM((tm, tn), F32) for _ in range(n_acc)]
    res = pl.pallas_call(body, grid=(gj, gi, gk), in_specs=in_specs, out_specs=out_specs, out_shape=out_shape,
                         scratch_shapes=scratch, name=name,
                         compiler_params=_params(("arbitrary", "arbitrary", "arbitrary")))(*args)
    return res


def _plain(dt):
    return dict(epilogue=lambda a, t, c, f: ([a[0]], []), outs=[dt])


def rows_call(name, fn, S, ts, tiled, fulls, outs, reds=()):
    ts = min(ts, S)
    assert S % ts == 0, (name, S, ts)
    in_specs = ([pl.BlockSpec((ts, t.shape[1]), lambda i: (i, 0)) for t in tiled]
                + [pl.BlockSpec(f.shape, lambda i, n=f.ndim: (0,) * n) for f in fulls])
    out_shape = [SDS((S, w), dt) for (w, dt) in outs] + [SDS(rw, F32) for rw in reds]
    out_specs = ([pl.BlockSpec((ts, w), lambda i: (i, 0)) for (w, _) in outs]
                 + [pl.BlockSpec(rw, lambda i: (0, 0)) for rw in reds])
    nt, nf, no = len(tiled), len(fulls), len(outs)

    def body(*refs):
        i = pl.program_id(0)
        tv = [r[...] for r in refs[:nt]]
        fv = [r[...] for r in refs[nt:nt + nf]]
        ov, rv = fn(i, tv, fv)
        for r, v in zip(refs[nt + nf:nt + nf + no], ov):
            r[...] = v.astype(r.dtype)
        for r, v in zip(refs[nt + nf + no:], rv):
            @pl.when(i == 0)
            def _(r=r, v=v):
                r[...] = v

            @pl.when(i > 0)
            def _(r=r, v=v):
                r[...] += v

    return pl.pallas_call(body, grid=(S // ts,), in_specs=in_specs, out_specs=out_specs, out_shape=out_shape,
                          name=name, compiler_params=_params(("arbitrary",)))(*tiled, *fulls)


def hrows_call(name, fn, H, S, ts, tiled, outs):
    ts = min(ts, S)
    in_specs = [pl.BlockSpec((1, ts, t.shape[2]), lambda h, i: (h, i, 0)) for t in tiled]
    out_shape = [SDS((H, S, w), dt) for (w, dt) in outs]
    out_specs = [pl.BlockSpec((1, ts, w), lambda h, i: (h, i, 0)) for (w, _) in outs]
    nt = len(tiled)

    def body(*refs):
        ov = fn([r[0] for r in refs[:nt]])
        for r, v in zip(refs[nt:], ov):
            r[0] = v.astype(r.dtype)

    return pl.pallas_call(body, grid=(H, S // ts), in_specs=in_specs, out_specs=out_specs, out_shape=out_shape,
                          name=name, compiler_params=_params(("arbitrary", "arbitrary")))(*tiled)


def _rms(x, g):
    r = lax.rsqrt(jnp.mean(x * x, axis=-1, keepdims=True) + EPS)
    return x * r * g


def _rms_bwd(x, g, dy):
    r = lax.rsqrt(jnp.mean(x * x, axis=-1, keepdims=True) + EPS)
    xh = x * r
    dg = jnp.sum(dy * xh, axis=0, keepdims=True)
    dxh = dy * g
    dx = r * (dxh - xh * jnp.mean(dxh * xh, axis=-1, keepdims=True))
    return dx, dg


def _silu(x):
    return x * jax.nn.sigmoid(x)


def _dsilu(x):
    s = jax.nn.sigmoid(x)
    return s * (1.0 + x * (1.0 - s))


def rms_rows(name, x, g):
    S, w = x.shape
    return rows_call(name, lambda i, t, f: ([_rms(t[0].astype(F32), f[0])], []), S, 512, [x], [g], [(w, BF)])[0]


def ffn_fwd(tag, x0, g_in, g_out, wg, wu, wd):
    S = x0.shape[0]
    h = rms_rows(tag + "_rms", x0, g_in)

    def ep_up(a, t, c, f):
        return [a[0], a[1], _silu(a[0]) * a[1]], []

    gate, up, act = mm_call(tag + "_up", [[(h, wg, False, False)], [(h, wu, False, False)]], S, D_FF, D_MODEL,
                            tm=1024, tn=256, tk=1024, epilogue=ep_up, outs=[BF, BF, BF])

    def ep_down(a, t, c, f):
        return [a[0], t[0] + 0.5 * _rms(a[0], c[0])], []

    y, x1 = mm_call(tag + "_down", [[(act, wd, False, False)]], S, D_MODEL, D_FF, tm=512, tn=1024, tk=1408,
                    epilogue=ep_down, outs=[BF, F32], tiles=[x0], cols=[g_out])
    return x1, dict(x0=x0, h=h, gate=gate, up=up, act=act, y=y)


def ffn_bwd(tag, dx1, sv, g_in, g_out, wg, wu, wd):
    S = dx1.shape[0]

    def post(i, t, f):
        dx, dg = _rms_bwd(t[1].astype(F32), f[0], 0.5 * t[0])
        return [dx], [dg]

    da, dg_out = rows_call(tag + "_bpost", post, S, 512, [dx1, sv['y']], [g_out], [(D_MODEL, BF)], [(1, D_MODEL)])

    def ep_act(a, t, c, f):
        g, u = t[0].astype(F32), t[1].astype(F32)
        return [a[0] * u * _dsilu(g), a[0] * _silu(g)], []

    dgate, dup = mm_call(tag + "_bact", [[(da, wd, False, True)]], S, D_FF, D_MODEL, tm=1024, tn=256, tk=1024,
                         epilogue=ep_act, outs=[BF, BF], tiles=[sv['gate'], sv['up']])
    dwd, = mm_call(tag + "_dwd", [[(sv['act'], da, True, False)]], D_FF, D_MODEL, S, tm=1408, tn=1024, tk=512,
                   **_plain(F32))
    dwg, dwu = mm_call(tag + "_dwgu", [[(sv['h'], dgate, True, False)], [(sv['h'], dup, True, False)]],
                       D_MODEL, D_FF, S, tm=1024, tn=256, tk=1024,
                       epilogue=lambda a, t, c, f: ([a[0], a[1]], []), outs=[F32, F32])

    def ep_in(a, t, c, f):
        dx, dg = _rms_bwd(t[0], c[0], a[0])
        return [t[1] + dx], [dg]

    (dx0, dg_in) = mm_call(tag + "_bin", [[(dgate, wg, False, True), (dup, wu, False, True)]], S, D_MODEL, D_FF,
                           tm=512, tn=1024, tk=1408, epilogue=ep_in, outs=[F32], tiles=[sv['x0'], dx1], cols=[g_in],
                           reds=[1])
    return dx0, dict(g_in=dg_in, g_out=dg_out, wg=dwg, wu=dwu, wd=dwd)


def mem_kv_fwd(tag, mem, g6, w_kv):
    def body(mem_ref, g_ref, w_ref, kv_ref):
        kv_ref[...] = _dot(_rms(mem_ref[...], g_ref[...]).astype(BF), w_ref[...])

    return pl.pallas_call(body, out_shape=SDS((N_MEM, 2 * MEM_WIDTH), F32), name=tag + "_memkv",
                          compiler_params=pltpu.CompilerParams(vmem_limit_bytes=V7X_VMEM_LIMIT))(mem, g6, w_kv)


def mem_kv_bwd(tag, mem, g6, w_kv, dkv):
    def body(mem_ref, g_ref, w_ref, dkv_ref, dw_ref, dg_ref):
        x = mem_ref[...]
        r = lax.rsqrt(jnp.mean(x * x, axis=-1, keepdims=True) + EPS)
        xh = x * r
        d = dkv_ref[...].astype(BF)
        dw_ref[...] = _dot((xh * g_ref[...]).astype(BF), d, 0, 0)
        dmem_n = _dot(d, w_ref[...], 1, 1)
        dg_ref[...] = jnp.sum(dmem_n * xh, axis=0, keepdims=True)

    return pl.pallas_call(body, out_shape=(SDS((D_MODEL, 2 * MEM_WIDTH), F32), SDS((1, D_MODEL), F32)),
                          name=tag + "_memkvb",
                          compiler_params=pltpu.CompilerParams(vmem_limit_bytes=V7X_VMEM_LIMIT))(mem, g6, w_kv, dkv)


def _block_diag_kv(kv):
    k = kv[:, :MEM_WIDTH].reshape(N_MEM, MEM_HEADS, HEAD_DIM)
    v = kv[:, MEM_WIDTH:].reshape(N_MEM, MEM_HEADS, HEAD_DIM)
    eye = jnp.eye(MEM_HEADS, dtype=kv.dtype)
    kbd = jnp.einsum('nhd,hg->hdgn', k, eye).reshape(MEM_WIDTH, MEM_HEADS * N_MEM)
    vbd = jnp.einsum('nhd,hg->hngd', v, eye).reshape(MEM_HEADS * N_MEM, MEM_WIDTH)
    return kbd.astype(BF), vbd.astype(BF)


def _mem_probs(q, kbd):
    s = _dot(q, kbd) * (HEAD_DIM ** -0.5)
    ps = []
    for h in range(MEM_HEADS):
        sh = s[:, h * N_MEM:(h + 1) * N_MEM]
        e = jnp.exp(sh - jnp.max(sh, axis=-1, keepdims=True))
        ps.append(e / jnp.sum(e, axis=-1, keepdims=True))
    return ps


def mem_attn_fwd(tag, q_mem, kbd, vbd):
    S = q_mem.shape[0]

    def fn(i, t, f):
        p = jnp.concatenate(_mem_probs(t[0], f[0]), axis=1).astype(BF)
        return [_dot(p, f[1])], []

    return rows_call(tag + "_mem", fn, S, 512, [q_mem], [kbd, vbd], [(MEM_WIDTH, BF)])[0]


def mem_attn_bwd(tag, q_mem, dmo, kbd, vbd):
    S = q_mem.shape[0]

    def fn(i, t, f):
        q, do = t
        ps = _mem_probs(q, f[0])
        dp = _dot(do, f[1], 1, 1)
        dss = []
        for h in range(MEM_HEADS):
            dph = dp[:, h * N_MEM:(h + 1) * N_MEM]
            dss.append(ps[h] * (dph - jnp.sum(dph * ps[h], axis=-1, keepdims=True)) * (HEAD_DIM ** -0.5))
        ds = jnp.concatenate(dss, axis=1).astype(BF)
        p = jnp.concatenate(ps, axis=1).astype(BF)
        return [_dot(ds, f[0], 1, 1)], [_dot(q, ds, 0, 0), _dot(p, do, 0, 0)]

    dq, dkbd, dvbd = rows_call(tag + "_memb", fn, S, 512, [q_mem, dmo], [kbd, vbd], [(MEM_WIDTH, BF)],
                               [(MEM_WIDTH, MEM_HEADS * N_MEM), (MEM_HEADS * N_MEM, MEM_WIDTH)])
    dk = jnp.einsum('hdhn->nhd', dkbd.reshape(MEM_HEADS, HEAD_DIM, MEM_HEADS, N_MEM)).reshape(N_MEM, MEM_WIDTH)
    dv = jnp.einsum('hnhd->nhd', dvbd.reshape(MEM_HEADS, N_MEM, MEM_HEADS, HEAD_DIM)).reshape(N_MEM, MEM_WIDTH)
    return dq, jnp.concatenate([dk, dv], axis=1)


def _rope_tables(positions):
    half = MLA_ROPE // 2
    inv = ROPE_THETA ** (-jnp.arange(half, dtype=F32) / half)
    ang = positions.astype(F32)[:, None] * inv
    cos, sin = jnp.cos(ang), jnp.sin(ang)
    cos32, sin32 = jnp.concatenate([cos, cos], 1), jnp.concatenate([sin, sin], 1)
    idx = jnp.arange(MLA_ROPE)
    p32 = (jnp.where((idx[:, None] == idx[None, :] + half), -1.0, 0.0)
           + jnp.where((idx[:, None] + half == idx[None, :]), 1.0, 0.0)).astype(F32)
    S = positions.shape[0]
    ones = jnp.ones((S, MLA_NOPE), F32)
    cq = jnp.tile(jnp.concatenate([ones, cos32], 1), (1, MLA_HEADS))
    sq = jnp.tile(jnp.concatenate([0.0 * ones, sin32], 1), (1, MLA_HEADS))
    pq = jnp.zeros((MLA_QK, MLA_QK), F32).at[MLA_NOPE:, MLA_NOPE:].set(p32)
    pq = jnp.kron(jnp.eye(MLA_HEADS, dtype=F32), pq)
    return dict(cos32=cos32, sin32=sin32, p32=p32.astype(BF), p32t=p32.T.astype(BF), cq=cq, sq=sq,
                pq=pq.astype(BF), pqt=pq.T.astype(BF))


def flash_fwd(tag, qh, kh, vh, scale, tq=1024, tk=1024):
    H, S, dq = qh.shape
    dv = vh.shape[2]
    tq, tk = min(tq, S), min(tk, S)
    nk = S // tk

    def body(q_ref, k_ref, v_ref, o_ref, lse_ref, m_s, l_s, acc_s):
        j = pl.program_id(2)

        @pl.when(j == 0)
        def _():
            m_s[...] = jnp.full(m_s.shape, NEG, F32)
            l_s[...] = jnp.zeros(l_s.shape, F32)
            acc_s[...] = jnp.zeros(acc_s.shape, F32)

        s = _dot(q_ref[0], k_ref[0], 1, 1) * scale
        m_old = m_s[...]
        m_new = jnp.maximum(m_old, jnp.max(s, axis=-1, keepdims=True))
        alpha = jnp.exp(m_old - m_new)
        p = jnp.exp(s - m_new)
        l_s[...] = alpha * l_s[...] + jnp.sum(p, axis=-1, keepdims=True)
        acc_s[...] = alpha * acc_s[...] + _dot(p.astype(BF), v_ref[0])
        m_s[...] = m_new

        @pl.when(j == nk - 1)
        def _():
            o_ref[0] = (acc_s[...] / l_s[...]).astype(o_ref.dtype)
            lse_ref[0] = m_s[...] + jnp.log(l_s[...])

    return pl.pallas_call(
        body, grid=(H, S // tq, nk),
        in_specs=[pl.BlockSpec((1, tq, dq), lambda h, i, j: (h, i, 0)),
                  pl.BlockSpec((1, tk, dq), lambda h, i, j: (h, j, 0)),
                  pl.BlockSpec((1, tk, dv), lambda h, i, j: (h, j, 0))],
        out_specs=[pl.BlockSpec((1, tq, dv), lambda h, i, j: (h, i, 0)),
                   pl.BlockSpec((1, tq, 1), lambda h, i, j: (h, i, 0))],
        out_shape=[SDS((H, S, dv), BF), SDS((H, S, 1), F32)],
        scratch_shapes=[pltpu.VMEM((tq, 1), F32), pltpu.VMEM((tq, 1), F32), pltpu.VMEM((tq, dv), F32)],
        name=tag + "_flash", compiler_params=_params(("arbitrary", "arbitrary", "arbitrary")))(qh, kh, vh)


def flash_bwd(tag, qh, kh, vh, doh, lse_row, delta_row, scale, tq=512, tk=1024):
    H, S, dq = qh.shape
    dv = vh.shape[2]
    tq, tk = min(tq, S), min(tk, S)

    def body(k_ref, v_ref, q_ref, do_ref, lse_ref, dl_ref, dk_ref, dv_ref, dq_ref):
        j, i = pl.program_id(1), pl.program_id(2)
        k, v, q, do = k_ref[0], v_ref[0], q_ref[0], do_ref[0]
        st = _dot(k, q, 1, 1) * scale
        pt = jnp.exp(st - lse_ref[0])
        dpt = _dot(v, do, 1, 1)
        dst = (pt * (dpt - dl_ref[0]) * scale).astype(BF)
        dvc = _dot(pt.astype(BF), do)
        dkc = _dot(dst, q)
        dqc = _dot(dst, k, 0, 0)

        @pl.when(i == 0)
        def _():
            dk_ref[0] = dkc
            dv_ref[0] = dvc

        @pl.when(i > 0)
        def _():
            dk_ref[0] += dkc
            dv_ref[0] += dvc

        rows = pl.ds(pl.multiple_of(i * tq, tq), tq)

        @pl.when(j == 0)
        def _():
            dq_ref[0, rows, :] = dqc

        @pl.when(j > 0)
        def _():
            dq_ref[0, rows, :] += dqc

    return pl.pallas_call(
        body, grid=(H, S // tk, S // tq),
        in_specs=[pl.BlockSpec((1, tk, dq), lambda h, j, i: (h, j, 0)),
                  pl.BlockSpec((1, tk, dv), lambda h, j, i: (h, j, 0)),
                  pl.BlockSpec((1, tq, dq), lambda h, j, i: (h, i, 0)),
                  pl.BlockSpec((1, tq, dv), lambda h, j, i: (h, i, 0)),
                  pl.BlockSpec((1, 1, tq), lambda h, j, i: (h, 0, i)),
                  pl.BlockSpec((1, 1, tq), lambda h, j, i: (h, 0, i))],
        out_specs=[pl.BlockSpec((1, tk, dq), lambda h, j, i: (h, j, 0)),
                   pl.BlockSpec((1, tk, dv), lambda h, j, i: (h, j, 0)),
                   pl.BlockSpec((1, S, dq), lambda h, j, i: (h, 0, 0))],
        out_shape=[SDS((H, S, dq), F32), SDS((H, S, dv), F32), SDS((H, S, dq), F32)],
        name=tag + "_flashb",
        compiler_params=_params(("arbitrary", "arbitrary", "arbitrary")))(kh, vh, qh, doh, lse_row, delta_row)


def _to_heads(t, H):
    return t.reshape(t.shape[0], H, -1).transpose(1, 0, 2)


def _from_heads(t):
    return t.transpose(1, 0, 2).reshape(t.shape[1], -1)


def mla_fwd(tag, z, rope, q_g, kv_g, w_uq, w_ukv):
    S = z.shape[0]
    o1, o2, o3 = MLA_Q_LORA, MLA_Q_LORA + MLA_KV_LORA, MLA_Q_LORA + MLA_KV_LORA + MLA_ROPE
    c_q, c_kv, k_r, q_mem = z[:, :o1], z[:, o1:o2], z[:, o2:o3], z[:, o3:]

    def pre(i, t, f):
        kr = t[2].astype(F32)
        krot = kr * t[3] + _dot(t[2], f[2]) * t[4]
        return [_rms(t[0].astype(F32), f[0]), _rms(t[1].astype(F32), f[1]), krot], []

    qn, kvn, krot = rows_call(tag + "_pre", pre, S, 512, [c_q, c_kv, k_r, rope['cos32'], rope['sin32']],
                              [q_g, kv_g, rope['p32']], [(MLA_Q_LORA, BF), (MLA_KV_LORA, BF), (MLA_ROPE, BF)])

    def ep_q(a, t, c, f):
        return [a[0] * t[0] + _dot(a[0].astype(BF), f[0]) * t[1]], []

    nq = MLA_HEADS * MLA_QK
    q, = mm_call(tag + "_uq", [[(qn, w_uq, False, False)]], S, nq, MLA_Q_LORA, tm=512, tn=nq, tk=MLA_Q_LORA,
                 epilogue=ep_q, outs=[BF], tiles=[rope['cq'], rope['sq']], fulls=[rope['pq']])
    nkv = MLA_HEADS * (MLA_NOPE + MLA_V)
    kv, = mm_call(tag + "_ukv", [[(kvn, w_ukv, False, False)]], S, nkv, MLA_KV_LORA, tm=1024, tn=nkv, tk=MLA_KV_LORA,
                  **_plain(BF))
    qh = _to_heads(q, MLA_HEADS)
    kv3 = kv.reshape(S, MLA_HEADS, MLA_NOPE + MLA_V)
    kh = jnp.concatenate([kv3[:, :, :MLA_NOPE].transpose(1, 0, 2),
                          jnp.broadcast_to(krot[None], (MLA_HEADS, S, MLA_ROPE))], axis=2)
    vh = kv3[:, :, MLA_NOPE:].transpose(1, 0, 2)
    oh, lse = flash_fwd(tag, qh, kh, vh, MLA_QK ** -0.5)
    sv = dict(c_q=c_q, c_kv=c_kv, k_r=k_r, qn=qn, kvn=kvn, qh=qh, kh=kh, vh=vh, oh=oh, lse=lse)
    return _from_heads(oh), q_mem, sv


def mla_bwd(tag, do, sv, rope, q_g, kv_g, w_uq, w_ukv):
    S = do.shape[0]
    H = MLA_HEADS
    doh = _to_heads(do, H)
    delta, = hrows_call(tag + "_delta",
                        lambda t: [jnp.sum(t[0].astype(F32) * t[1].astype(F32), axis=-1, keepdims=True)],
                        H, S, 1024, [doh, sv['oh']], [(1, F32)])
    dkh, dvh, dqh = flash_bwd(tag, sv['qh'], sv['kh'], sv['vh'], doh, sv['lse'].reshape(H, 1, S),
                              delta.reshape(H, 1, S), MLA_QK ** -0.5)
    nq = H * MLA_QK
    dq_rot = _from_heads(dqh)

    def unrope(i, t, f):
        return [t[0] * t[1] + _dot((t[0] * t[2]).astype(BF), f[0])], []

    dq, = rows_call(tag + "_unrope", unrope, S, 512, [dq_rot, rope['cq'], rope['sq']], [rope['pqt']], [(nq, BF)])

    def ep_norm(a, t, c, f):
        dx, dg = _rms_bwd(t[0].astype(F32), c[0], a[0])
        return [dx], [dg]

    dc_q, dq_g = mm_call(tag + "_bqn", [[(dq, w_uq, False, True)]], S, MLA_Q_LORA, nq, tm=1024, tn=MLA_Q_LORA, tk=nq,
                         epilogue=ep_norm, outs=[BF], tiles=[sv['c_q']], cols=[q_g], reds=[1])
    dw_uq, = mm_call(tag + "_dwuq", [[(sv['qn'], dq, True, False)]], MLA_Q_LORA, nq, S, tm=MLA_Q_LORA, tn=nq, tk=1024,
                     **_plain(F32))
    nkv = H * (MLA_NOPE + MLA_V)
    dkv = jnp.concatenate([dkh[:, :, :MLA_NOPE], dvh], axis=2).transpose(1, 0, 2).reshape(S, nkv).astype(BF)
    dc_kv, dkv_g = mm_call(tag + "_bkvn", [[(dkv, w_ukv, False, True)]], S, MLA_KV_LORA, nkv, tm=1024, tn=MLA_KV_LORA,
                           tk=nkv, epilogue=ep_norm, outs=[BF], tiles=[sv['c_kv']], cols=[kv_g], reds=[1])
    dw_ukv, = mm_call(tag + "_dwukv", [[(sv['kvn'], dkv, True, False)]], MLA_KV_LORA, nkv, S, tm=MLA_KV_LORA, tn=nkv,
                      tk=1024, **_plain(F32))
    dkr_heads = dkh[:, :, MLA_NOPE:].transpose(1, 0, 2).reshape(S, H * MLA_ROPE)
    hsum = jnp.tile(jnp.eye(MLA_ROPE, dtype=F32), (H, 1))

    def krope(i, t, f):
        d = jnp.dot(t[0], f[0], preferred_element_type=F32, precision=lax.Precision.HIGHEST)
        return [d * t[1] + _dot((d * t[2]).astype(BF), f[1])], []

    dk_r, = rows_call(tag + "_bkr", krope, S, 512, [dkr_heads, rope['cos32'], rope['sin32']], [hsum, rope['p32t']],
                      [(MLA_ROPE, BF)])
    grads = dict(a_q_norm=dq_g, a_kv_norm=dkv_g, a_w_uq=dw_uq, a_w_ukv=dw_ukv)
    return [dc_q, dc_kv, dk_r], grads


def _dil_perm(t, dil):
    H, S, d = t.shape
    return t.reshape(H, S // dil, dil, d).transpose(0, 2, 1, 3).reshape(H, S, d)


def _dil_unperm(t, dil):
    H, S, d = t.shape
    return t.reshape(H, dil, S // dil, d).transpose(0, 2, 1, 3).reshape(H, S, d)


def _band(rpos, cpos, S, seg_shift):
    dist = jnp.abs(rpos - cpos)
    valid = ((dist <= DIL_HALF) & (cpos >= 0) & (cpos < S) & (rpos >= 0) & (rpos < S)
             & (jnp.right_shift(rpos, seg_shift) == jnp.right_shift(cpos, seg_shift)))
    return dist, valid


def _halo_specs(blk, halo, n_halo_blocks, w):
    r = blk // halo
    return [pl.BlockSpec((1, halo, w), lambda h, i: (h, jnp.maximum(i * r - 1, 0), 0)),
            pl.BlockSpec((1, blk, w), lambda h, i: (h, i, 0)),
            pl.BlockSpec((1, halo, w), lambda h, i: (h, jnp.minimum((i + 1) * r, n_halo_blocks - 1), 0))]


def dil_fwd(tag, q, k, v, slopes, dil, tq=256):
    H, S, dh = q.shape
    tq = min(tq, S)
    seg_shift = int(math.log2(S // dil))
    hb = DIL_HALF
    scale = dh ** -0.5

    def body(sl_ref, q_ref, kp, kc, kn, vp, vc, vn, o_ref, lse_ref):
        i = pl.program_id(1)
        kcat = jnp.concatenate([kp[0], kc[0], kn[0]], axis=0)
        vcat = jnp.concatenate([vp[0], vc[0], vn[0]], axis=0)
        s = _dot(q_ref[0], kcat, 1, 1) * scale
        shp = (tq, tq + 2 * hb)
        qpos = i * tq + lax.broadcasted_iota(jnp.int32, shp, 0)
        kpos = i * tq - hb + lax.broadcasted_iota(jnp.int32, shp, 1)
        dist, valid = _band(qpos, kpos, S, seg_shift)
        s = s - sl_ref[0][:, :1] * (dist * dil).astype(F32)
        s = jnp.where(valid, s, NEG)
        m = jnp.max(s, axis=-1, keepdims=True)
        lse = m + jnp.log(jnp.sum(jnp.exp(s - m), axis=-1, keepdims=True))
        p = jnp.exp(s - lse).astype(BF)
        o_ref[0] = _dot(p, vcat)
        lse_ref[0] = lse

    return pl.pallas_call(
        body, grid=(H, S // tq),
        in_specs=([pl.BlockSpec((1, 1, 128), lambda h, i: (h, 0, 0)), pl.BlockSpec((1, tq, dh), lambda h, i: (h, i, 0))]
                  + _halo_specs(tq, hb, S // hb, dh) + _halo_specs(tq, hb, S // hb, dh)),
        out_specs=[pl.BlockSpec((1, tq, dh), lambda h, i: (h, i, 0)), pl.BlockSpec((1, tq, 1), lambda h, i: (h, i, 0))],
        out_shape=[SDS((H, S, dh), F32), SDS((H, S, 1), F32)],
        name=tag + "_dil", compiler_params=_params(("arbitrary", "arbitrary")))(slopes, q, k, k, k, v, v, v)


def dil_bwd_q(tag, q, k, v, do, lse, deff, slopes, dil, tq=256):
    H, S, dh = q.shape
    tq = min(tq, S)
    seg_shift = int(math.log2(S // dil))
    hb = DIL_HALF
    scale = dh ** -0.5

    def body(sl_ref, q_ref, do_ref, lse_ref, de_ref, kp, kc, kn, vp, vc, vn, dq_ref):
        i = pl.program_id(1)
        kcat = jnp.concatenate([kp[0], kc[0], kn[0]], axis=0)
        vcat = jnp.concatenate([vp[0], vc[0], vn[0]], axis=0)
        s = _dot(q_ref[0], kcat, 1, 1) * scale
        shp = (tq, tq + 2 * hb)
        qpos = i * tq + lax.broadcasted_iota(jnp.int32, shp, 0)
        kpos = i * tq - hb + lax.broadcasted_iota(jnp.int32, shp, 1)
        dist, valid = _band(qpos, kpos, S, seg_shift)
        s = s - sl_ref[0][:, :1] * (dist * dil).astype(F32)
        p = jnp.where(valid, jnp.exp(jnp.where(valid, s, NEG) - lse_ref[0]), 0.0)
        dp = _dot(do_ref[0], vcat, 1, 1)
        ds = (p * (dp - de_ref[0]) * scale).astype(BF)
        dq_ref[0] = _dot(ds, kcat)

    row = lambda w: pl.BlockSpec((1, tq, w), lambda h, i: (h, i, 0))
    return pl.pallas_call(
        body, grid=(H, S // tq),
        in_specs=([pl.BlockSpec((1, 1, 128), lambda h, i: (h, 0, 0)), row(dh), row(dh), row(1), row(1)]
                  + _halo_specs(tq, hb, S // hb, dh) + _halo_specs(tq, hb, S // hb, dh)),
        out_specs=row(dh), out_shape=SDS((H, S, dh), F32),
        name=tag + "_dilbq",
        compiler_params=_params(("arbitrary", "arbitrary")))(slopes, q, do, lse, deff, k, k, k, v, v, v)


def dil_bwd_kv(tag, q, k, v, do, lse_row, deff_row, slopes, dil, tk=256):
    H, S, dh = q.shape
    tk = min(tk, S)
    seg_shift = int(math.log2(S // dil))
    hq = 128
    scale = dh ** -0.5
    r = tk // hq
    nhb = S // hq
    stat_specs = [pl.BlockSpec((1, 1, hq), lambda h, j: (h, 0, jnp.maximum(j * r - 1, 0))),
                  pl.BlockSpec((1, 1, tk), lambda h, j: (h, 0, j)),
                  pl.BlockSpec((1, 1, hq), lambda h, j: (h, 0, jnp.minimum((j + 1) * r, nhb - 1)))]

    def body(sl_ref, k_ref, v_ref, qp, qc, qn, dop, doc, don, lp, lc, ln, ep, ec, en, dk_ref, dv_ref):
        j = pl.program_id(1)
        qcat = jnp.concatenate([qp[0], qc[0], qn[0]], axis=0)
        docat = jnp.concatenate([dop[0], doc[0], don[0]], axis=0)
        lse = jnp.concatenate([lp[0], lc[0], ln[0]], axis=1)
        deff = jnp.concatenate([ep[0], ec[0], en[0]], axis=1)
        st = _dot(k_ref[0], qcat, 1, 1) * scale
        shp = (tk, tk + 2 * hq)
        kpos = j * tk + lax.broadcasted_iota(jnp.int32, shp, 0)
        qpos = j * tk - hq + lax.broadcasted_iota(jnp.int32, shp, 1)
        dist, valid = _band(kpos, qpos, S, seg_shift)
        st = st - sl_ref[0][:, :1] * (dist * dil).astype(F32)
        pt = jnp.where(valid, jnp.exp(jnp.where(valid, st, NEG) - jnp.where(valid, lse, 0.0)), 0.0)
        dpt = _dot(v_ref[0], docat, 1, 1)
        dst = (pt * (dpt - deff) * scale).astype(BF)
        dv_ref[0] = _dot(pt.astype(BF), docat)
        dk_ref[0] = _dot(dst, qcat)

    cur = lambda w: pl.BlockSpec((1, tk, w), lambda h, j: (h, j, 0))
    return pl.pallas_call(
        body, grid=(H, S // tk),
        in_specs=([pl.BlockSpec((1, 1, 128), lambda h, j: (h, 0, 0)), cur(dh), cur(dh)]
                  + _halo_specs(tk, hq, nhb, dh) + _halo_specs(tk, hq, nhb, dh) + stat_specs + stat_specs),
        out_specs=[cur(dh), cur(dh)], out_shape=[SDS((H, S, dh), F32), SDS((H, S, dh), F32)],
        name=tag + "_dilbkv", compiler_params=_params(("arbitrary", "arbitrary")))(
            slopes, k, v, q, q, q, do, do, do, lse_row, lse_row, lse_row, deff_row, deff_row, deff_row)


def _alibi_slopes():
    n = len(DIL_GROUPS) * DIL_HEADS
    s = 2.0 ** (-ALIBI_MAX * (jnp.arange(n, dtype=F32) + 1.0) / n)
    return jnp.broadcast_to(s.reshape(len(DIL_GROUPS), DIL_HEADS, 1, 1), (len(DIL_GROUPS), DIL_HEADS, 1, 128))


def _merge_weights(lses):
    m = jnp.maximum(jnp.maximum(lses[0], lses[1]), lses[2])
    es = [jnp.exp(l - m) for l in lses]
    tot = es[0] + es[1] + es[2]
    return [e / tot for e in es]


def dilated_fwd(tag, z):
    S = z.shape[0]
    H = DIL_HEADS
    ng = len(DIL_GROUPS)
    zd = z[:, :DIL_QKV].reshape(S, ng, 3, H, HEAD_DIM)
    q_mem = z[:, DIL_QKV:]
    slopes = _alibi_slopes()
    qkv, outs, lses = [], [], []
    for g, (_, dil) in enumerate(DIL_GROUPS):
        t = [_dil_perm(zd[:, g, c].transpose(1, 0, 2), dil) for c in range(3)]
        o, lse = dil_fwd(f"{tag}_g{g}", t[0], t[1], t[2], slopes[g], dil)
        qkv.append(t)
        outs.append(_dil_unperm(o, dil))
        lses.append(_dil_unperm(lse, dil))

    def merge(t):
        w = _merge_weights(t[3:6])
        return [w[0] * t[0] + w[1] * t[1] + w[2] * t[2]]

    oh, = hrows_call(tag + "_merge", merge, H, S, 1024, outs + lses, [(HEAD_DIM, BF)])
    return _from_heads(oh), q_mem, dict(qkv=qkv, outs=outs, lses=lses)


def dilated_bwd(tag, do, sv):
    S = do.shape[0]
    H = DIL_HEADS
    slopes = _alibi_slopes()
    doh = _to_heads(do, H)

    def merge_b(t):
        d = t[0].astype(F32)
        w = _merge_weights(t[4:7])
        dws = [jnp.sum(d * t[1 + g], axis=-1, keepdims=True) for g in range(3)]
        c = w[0] * dws[0] + w[1] * dws[1] + w[2] * dws[2]
        return [w[g] * d for g in range(3)] + [w[g] * c for g in range(3)]

    res = hrows_call(tag + "_mergeb", merge_b, H, S, 1024, [doh] + sv['outs'] + sv['lses'],
                     [(HEAD_DIM, BF)] * 3 + [(1, F32)] * 3)
    parts = []
    for g, (_, dil) in enumerate(DIL_GROUPS):
        q, k, v = sv['qkv'][g]
        do_g = _dil_perm(res[g], dil)
        deff = _dil_perm(res[3 + g], dil)
        lse = _dil_perm(sv['lses'][g], dil)
        dq = dil_bwd_q(f"{tag}_g{g}", q, k, v, do_g, lse, deff, slopes[g], dil)
        dk, dv = dil_bwd_kv(f"{tag}_g{g}", q, k, v, do_g, lse.reshape(H, 1, S), deff.reshape(H, 1, S), slopes[g], dil)
        parts.append(jnp.stack([_dil_unperm(t, dil).transpose(1, 0, 2) for t in (dq, dk, dv)], axis=1))
    dzd = jnp.stack(parts, axis=1)
    return [dzd.reshape(S, DIL_QKV).astype(BF)]


def _halo_specs2(ts, w, S):
    r = ts // HALO
    return [pl.BlockSpec((HALO, w), lambda i: (jnp.maximum(i * r - 1, 0), 0)),
            pl.BlockSpec((ts, w), lambda i: (i, 0)),
            pl.BlockSpec((HALO, w), lambda i: (jnp.minimum((i + 1) * r, S // HALO - 1), 0))]


def _cat_rows(i, ts, S, refs):
    v = jnp.concatenate([r[...].astype(F32) for r in refs], axis=0)
    pos = i * ts - HALO + lax.broadcasted_iota(jnp.int32, (ts + 2 * HALO, 1), 0)
    return v, (pos >= 0) & (pos < S)


def _conv_taps(buf_ref, w, ts, width, flip):
    acc = None
    for k in range(width):
        off = HALO + ((width // 2 - k) if flip else (k - width // 2))
        term = buf_ref[pl.ds(off, ts), :] * w[k:k + 1, :]
        acc = term if acc is None else acc + term
    return acc


def conformer_fwd(tag, z, conv_w, conv_b, ln_g, ln_b, ts=256):
    S = z.shape[0]
    ts = min(ts, S)
    C = CONV_CH
    a, gate, q_mem = z[:, :C], z[:, C:2 * C], z[:, 2 * C:]

    def body(ap, ac, an, gp, gc, gn, w_ref, b_ref, lg_ref, lb_ref, u_ref, u1_ref, buf):
        i = pl.program_id(0)
        av, ok = _cat_rows(i, ts, S, (ap, ac, an))
        gv, _ = _cat_rows(i, ts, S, (gp, gc, gn))
        buf[...] = jnp.where(ok, av * jax.nn.sigmoid(gv), 0.0)
        u1 = _conv_taps(buf, w_ref[...], ts, CONV_WIDTH, False) + b_ref[...]
        mu = jnp.mean(u1, axis=-1, keepdims=True)
        xc = u1 - mu
        y = xc * lax.rsqrt(jnp.mean(xc * xc, axis=-1, keepdims=True) + EPS) * lg_ref[...] + lb_ref[...]
        u_ref[...] = _silu(y).astype(u_ref.dtype)
        u1_ref[...] = u1

    full = lambda t: pl.BlockSpec(t.shape, lambda i: (0, 0))
    u, u1 = pl.pallas_call(
        body, grid=(S // ts,),
        in_specs=_halo_specs2(ts, C, S) + _halo_specs2(ts, C, S) + [full(conv_w), full(conv_b), full(ln_g), full(ln_b)],
        out_specs=[pl.BlockSpec((ts, C), lambda i: (i, 0))] * 2, out_shape=[SDS((S, C), BF), SDS((S, C), F32)],
        scratch_shapes=[pltpu.VMEM((ts + 2 * HALO, C), F32)], name=tag + "_conf",
        compiler_params=_params(("arbitrary",)))(a, a, a, gate, gate, gate, conv_w, conv_b, ln_g, ln_b)
    return u, q_mem, dict(a=a, gate=gate, u1=u1)


def conformer_bwd(tag, du, sv, conv_w, conv_b, ln_g, ln_b, ts=256):
    S = du.shape[0]
    ts = min(ts, S)
    C = CONV_CH

    def ln_b_fn(i, t, f):
        u1, d = t[1], t[0].astype(F32)
        mu = jnp.mean(u1, axis=-1, keepdims=True)
        xc = u1 - mu
        r = lax.rsqrt(jnp.mean(xc * xc, axis=-1, keepdims=True) + EPS)
        xh = xc * r
        dy = d * _dsilu(xh * f[0] + f[1])
        dxh = dy * f[0]
        du1 = r * (dxh - jnp.mean(dxh, axis=-1, keepdims=True) - xh * jnp.mean(dxh * xh, axis=-1, keepdims=True))
        return [du1], [jnp.sum(dy * xh, axis=0, keepdims=True), jnp.sum(dy, axis=0, keepdims=True),
                       jnp.sum(du1, axis=0, keepdims=True)]

    du1, dln_g, dln_b, dconv_b = rows_call(tag + "_confb1", ln_b_fn, S, 512, [du, sv['u1']], [ln_g, ln_b],
                                           [(C, F32)], [(1, C), (1, C), (1, C)])
    wpad = CONV_WIDTH + 1

    def body(dp, dc, dn, ap, ac, an, gp, gc, gn, w_ref, da_ref, dg_ref, dw_ref, ubuf, dbuf):
        i = pl.program_id(0)
        av, ok = _cat_rows(i, ts, S, (ap, ac, an))
        gv, _ = _cat_rows(i, ts, S, (gp, gc, gn))
        dv, _ = _cat_rows(i, ts, S, (dp, dc, dn))
        sg = jax.nn.sigmoid(gv)
        ubuf[...] = jnp.where(ok, av * sg, 0.0)
        dbuf[...] = jnp.where(ok, dv, 0.0)
        du0 = _conv_taps(dbuf, w_ref[...], ts, CONV_WIDTH, True)
        a_c, s_c = av[HALO:HALO + ts], sg[HALO:HALO + ts]
        da_ref[...] = (du0 * s_c).astype(da_ref.dtype)
        dg_ref[...] = (du0 * a_c * s_c * (1.0 - s_c)).astype(dg_ref.dtype)
        d_c = dbuf[pl.ds(HALO, ts), :]

        @pl.when(i == 0)
        def _():
            dw_ref[...] = jnp.zeros(dw_ref.shape, F32)

        for k in range(CONV_WIDTH):
            off = HALO + k - CONV_WIDTH // 2
            dw_ref[k:k + 1, :] += jnp.sum(d_c * ubuf[pl.ds(off, ts), :], axis=0, keepdims=True)

    full = lambda t: pl.BlockSpec(t.shape, lambda i: (0, 0))
    da, dgate, dw = pl.pallas_call(
        body, grid=(S // ts,),
        in_specs=_halo_specs2(ts, C, S) * 3 + [full(conv_w)],
        out_specs=[pl.BlockSpec((ts, C), lambda i: (i, 0))] * 2 + [pl.BlockSpec((wpad, C), lambda i: (0, 0))],
        out_shape=[SDS((S, C), BF), SDS((S, C), BF), SDS((wpad, C), F32)],
        scratch_shapes=[pltpu.VMEM((ts + 2 * HALO, C), F32)] * 2, name=tag + "_confb2",
        compiler_params=_params(("arbitrary",)))(du1, du1, du1, sv['a'], sv['a'], sv['a'], sv['gate'], sv['gate'],
                                                 sv['gate'], conv_w)
    grads = dict(c_conv_w=dw[:CONV_WIDTH], c_conv_b=dconv_b, c_ln_g=dln_g, c_ln_b=dln_b)
    return [da, dgate], grads


def shortconv_fwd(tag, z, conv_w, ts=256):
    S = z.shape[0]
    ts = min(ts, S)
    C = SC_CH
    bg, cg, hx, q_mem = z[:, :C], z[:, C:2 * C], z[:, 2 * C:3 * C], z[:, 3 * C:]

    def body(b_ref, cp, cc, cn, hp, hc, hn, w_ref, o_ref, buf):
        i = pl.program_id(0)
        cv, ok = _cat_rows(i, ts, S, (cp, cc, cn))
        hv, _ = _cat_rows(i, ts, S, (hp, hc, hn))
        buf[...] = jnp.where(ok, cv * hv, 0.0)
        o_ref[...] = (b_ref[...].astype(F32) * _conv_taps(buf, w_ref[...], ts, SC_WIDTH, False)).astype(o_ref.dtype)

    o = pl.pallas_call(
        body, grid=(S // ts,),
        in_specs=[pl.BlockSpec((ts, C), lambda i: (i, 0))] + _halo_specs2(ts, C, S) * 2
        + [pl.BlockSpec(conv_w.shape, lambda i: (0, 0))],
        out_specs=pl.BlockSpec((ts, C), lambda i: (i, 0)), out_shape=SDS((S, C), BF),
        scratch_shapes=[pltpu.VMEM((ts + 2 * HALO, C), F32)], name=tag + "_sconv",
        compiler_params=_params(("arbitrary",)))(bg, cg, cg, cg, hx, hx, hx, conv_w)
    return o, q_mem, dict(bg=bg, cg=cg, hx=hx)


def shortconv_bwd(tag, do, sv, conv_w, ts=256):
    S = do.shape[0]
    ts = min(ts, S)
    C = SC_CH
    wpad = 8

    def body(dp, dc, dn, bp, bc, bn, cp, cc, cn, hp, hc, hn, w_ref, db_ref, dcg_ref, dhx_ref, dw_ref, mbuf, dbuf):
        i = pl.program_id(0)
        dov, ok = _cat_rows(i, ts, S, (dp, dc, dn))
        bv, _ = _cat_rows(i, ts, S, (bp, bc, bn))
        cv, _ = _cat_rows(i, ts, S, (cp, cc, cn))
        hv, _ = _cat_rows(i, ts, S, (hp, hc, hn))
        mbuf[...] = jnp.where(ok, cv * hv, 0.0)
        dbuf[...] = jnp.where(ok, dov * bv, 0.0)
        conv = _conv_taps(mbuf, w_ref[...], ts, SC_WIDTH, False)
        db_ref[...] = (dov[HALO:HALO + ts] * conv).astype(db_ref.dtype)
        dm = _conv_taps(dbuf, w_ref[...], ts, SC_WIDTH, True)
        dcg_ref[...] = (dm * hv[HALO:HALO + ts]).astype(dcg_ref.dtype)
        dhx_ref[...] = (dm * cv[HALO:HALO + ts]).astype(dhx_ref.dtype)
        d_c = dbuf[pl.ds(HALO, ts), :]

        @pl.when(i == 0)
        def _():
            dw_ref[...] = jnp.zeros(dw_ref.shape, F32)

        for k in range(SC_WIDTH):
            off = HALO + k - SC_WIDTH // 2
            dw_ref[k:k + 1, :] += jnp.sum(d_c * mbuf[pl.ds(off, ts), :], axis=0, keepdims=True)

    dbg, dcg, dhx, dw = pl.pallas_call(
        body, grid=(S // ts,),
        in_specs=_halo_specs2(ts, C, S) * 4 + [pl.BlockSpec(conv_w.shape, lambda i: (0, 0))],
        out_specs=[pl.BlockSpec((ts, C), lambda i: (i, 0))] * 3 + [pl.BlockSpec((wpad, C), lambda i: (0, 0))],
        out_shape=[SDS((S, C), BF)] * 3 + [SDS((wpad, C), F32)],
        scratch_shapes=[pltpu.VMEM((ts + 2 * HALO, C), F32)] * 2, name=tag + "_sconvb",
        compiler_params=_params(("arbitrary",)))(do, do, do, sv['bg'], sv['bg'], sv['bg'], sv['cg'], sv['cg'],
                                                 sv['cg'], sv['hx'], sv['hx'], sv['hx'], conv_w)
    return [dbg, dcg, dhx], dict(d_conv_w=dw[:SC_WIDTH])


MIXERS = ('a', 'b', 'c', 'd')


def _mixer_weights(W, m):
    return W[m + '_w_in'][0], W[m + '_w_out'][0]


def layer_fwd(i, x, mem, rope, W):
    tag = f"l{i}"
    g = W['norm_g'][i]
    gr = lambda k: g[k:k + 1]
    m = MIXERS[i % len(MIXERS)]
    sv = {}
    x1, sv['ffn0'] = ffn_fwd(tag + "_f0", x, gr(0), gr(1), W['ffn_w_gate'][i, 0], W['ffn_w_up'][i, 0],
                             W['ffn_w_down'][i, 0])
    w_in, w_out = _mixer_weights(W, m)
    S = x.shape[0]
    h2 = rms_rows(tag + "_rms2", x1, gr(2))
    n_in = w_in.shape[1]
    z, = mm_call(tag + "_in", [[(h2, w_in, False, False)]], S, n_in, D_MODEL, tm=1024,
                 tn=(256 if n_in % 256 == 0 else n_in), tk=1024, **_plain(BF))
    if m == 'a':
        o, q_mem, sv['mix'] = mla_fwd(tag, z, rope, W['a_q_norm'], W['a_kv_norm'], W['a_w_uq'][0], W['a_w_ukv'][0])
    elif m == 'b':
        o, q_mem, sv['mix'] = dilated_fwd(tag, z)
    elif m == 'c':
        o, q_mem, sv['mix'] = conformer_fwd(tag, z, W['c_conv_w'][0], W['c_conv_b'], W['c_ln_g'], W['c_ln_b'])
    else:
        o, q_mem, sv['mix'] = shortconv_fwd(tag, z, W['d_conv_w'][0])
    kv = mem_kv_fwd(tag, mem, gr(6), W['mem_w_kv'][i])
    kbd, vbd = _block_diag_kv(kv)
    mo = mem_attn_fwd(tag, q_mem, kbd, vbd)
    cat = jnp.concatenate([o, mo], axis=1)
    n_out = cat.shape[1]

    def ep_out(a, t, c, f):
        return [a[0], t[0] + _rms(a[0], c[0])], []

    y2, x2 = mm_call(tag + "_out", [[(cat, w_out, False, False)]], S, D_MODEL, n_out, tm=512, tn=1024, tk=n_out,
                     epilogue=ep_out, outs=[BF, F32], tiles=[x1], cols=[gr(3)])
    sv.update(x1=x1, h2=h2, q_mem=q_mem, kbd=kbd, vbd=vbd, cat=cat, y2=y2)
    x3, sv['ffn1'] = ffn_fwd(tag + "_f1", x2, gr(4), gr(5), W['ffn_w_gate'][i, 1], W['ffn_w_up'][i, 1],
                             W['ffn_w_down'][i, 1])
    return x3, sv


def layer_bwd(i, dx3, sv, mem, rope, W):
    tag = f"l{i}"
    g = W['norm_g'][i]
    gr = lambda k: g[k:k + 1]
    m = MIXERS[i % len(MIXERS)]
    S = dx3.shape[0]
    dx2, f1 = ffn_bwd(tag + "_f1", dx3, sv['ffn1'], gr(4), gr(5), W['ffn_w_gate'][i, 1], W['ffn_w_up'][i, 1],
                      W['ffn_w_down'][i, 1])
    w_in, w_out = _mixer_weights(W, m)
    n_in, n_out = w_in.shape[1], w_out.shape[0]

    def post(ii, t, f):
        dx, dg = _rms_bwd(t[1].astype(F32), f[0], t[0])
        return [dx], [dg]

    dy2, dg3 = rows_call(tag + "_bpost", post, S, 512, [dx2, sv['y2']], [gr(3)], [(D_MODEL, BF)], [(1, D_MODEL)])
    dcat, = mm_call(tag + "_bout", [[(dy2, w_out, False, True)]], S, n_out, D_MODEL, tm=1024,
                    tn=(256 if n_out % 256 == 0 else n_out), tk=1024, **_plain(BF))
    dw_out, = mm_call(tag + "_dwout", [[(sv['cat'], dy2, True, False)]], n_out, D_MODEL, S, tm=n_out, tn=1024, tk=512,
                      **_plain(F32))
    n_o = n_out - MEM_WIDTH
    do, dmo = dcat[:, :n_o], dcat[:, n_o:]
    dq_mem, dkv = mem_attn_bwd(tag, sv['q_mem'], dmo, sv['kbd'], sv['vbd'])
    dw_kv, dg6 = mem_kv_bwd(tag, mem, gr(6), W['mem_w_kv'][i], dkv)
    if m == 'a':
        dzs, mg = mla_bwd(tag, do, sv['mix'], rope, W['a_q_norm'], W['a_kv_norm'], W['a_w_uq'][0], W['a_w_ukv'][0])
    elif m == 'b':
        dzs, mg = dilated_bwd(tag, do, sv['mix']), {}
    elif m == 'c':
        dzs, mg = conformer_bwd(tag, do, sv['mix'], W['c_conv_w'][0], W['c_conv_b'], W['c_ln_g'], W['c_ln_b'])
    else:
        dzs, mg = shortconv_bwd(tag, do, sv['mix'], W['d_conv_w'][0])
    dz = jnp.concatenate(dzs + [dq_mem], axis=1)
    dw_in, = mm_call(tag + "_dwin", [[(sv['h2'], dz, True, False)]], D_MODEL, n_in, S, tm=1024,
                     tn=(256 if n_in % 256 == 0 else n_in), tk=512, **_plain(F32))

    def ep_in(a, t, c, f):
        dx, dg = _rms_bwd(t[0], c[0], a[0])
        return [t[1] + dx], [dg]

    dx1, dg2 = mm_call(tag + "_bin", [[(dz, w_in, False, True)]], S, D_MODEL, n_in, tm=512, tn=1024,
                       tk=(n_in // 2 if n_in % 256 == 0 else n_in), epilogue=ep_in, outs=[F32],
                       tiles=[sv['x1'], dx2], cols=[gr(2)], reds=[1])
    dx0, f0 = ffn_bwd(tag + "_f0", dx1, sv['ffn0'], gr(0), gr(1), W['ffn_w_gate'][i, 0], W['ffn_w_up'][i, 0],
                      W['ffn_w_down'][i, 0])
    dg = jnp.concatenate([f0['g_in'], f0['g_out'], dg2, dg3, f1['g_in'], f1['g_out'], dg6], axis=0)
    grads = dict(norm_g=dg, ffn_w_gate=jnp.stack([f0['wg'], f1['wg']]), ffn_w_up=jnp.stack([f0['wu'], f1['wu']]),
                 ffn_w_down=jnp.stack([f0['wd'], f1['wd']]), mem_w_kv=dw_kv)
    grads[m + '_w_in'] = dw_in[None]
    grads[m + '_w_out'] = dw_out[None]
    for k, v in mg.items():
        grads[k] = v if k in ('a_q_norm', 'a_kv_norm', 'c_conv_b', 'c_ln_g', 'c_ln_b') else v[None]
    return dx0, grads


def local_step(x, mem, positions, target, W):
    S = x.shape[0]
    rope = _rope_tables(positions)
    saved = []
    for i in range(DEPTH):
        x, sv = layer_fwd(i, x, mem, rope, W)
        saved.append(sv)

    def loss_fn(i, t, f):
        e = t[0] - t[1]
        part = jnp.sum(0.5 * jnp.mean(e * e, axis=-1, keepdims=True), axis=0, keepdims=True)
        return [e * (1.0 / D_MODEL)], [jnp.broadcast_to(part, (1, 128))]

    dx, loss = rows_call("loss", loss_fn, S, 512, [x, target], [], [(D_MODEL, F32)], [(1, 128)])
    per_layer = []
    for i in reversed(range(DEPTH)):
        dx, gr = layer_bwd(i, dx, saved[i], mem, rope, W)
        per_layer.append(gr)
    per_layer = per_layer[::-1]
    grads = {}
    for n in ('norm_g', 'ffn_w_gate', 'ffn_w_up', 'ffn_w_down', 'mem_w_kv'):
        grads[n] = jnp.stack([p[n] for p in per_layer])
    for p in per_layer:
        for n, v in p.items():
            if n not in grads:
                grads[n] = v
    return loss[0, 0], dx, grads


def _local_shape(full_shape, axis):
    s = list(full_shape)
    if axis is not None:
        s[axis] //= N_DEV
    return tuple(s)


def _pad_to(n, m):
    return -(-n // m) * m


def pack_local(arrs, dtype, row_mult):
    flat = jnp.concatenate([a.reshape(-1).astype(dtype) for a in arrs])
    n = _pad_to(flat.shape[0], PACK_W * row_mult)
    return jnp.pad(flat, (0, n - flat.shape[0])).reshape(n // PACK_W, PACK_W)


def unpack_local(buf, names, shapes):
    flat = buf.reshape(-1)
    out, off = {}, 0
    for n in names:
        sz = math.prod(shapes[n])
        out[n] = flat[off:off + sz].reshape(shapes[n])
        off += sz
    return out


def unpack_gathered(buf, names, shapes):
    flat = buf.reshape(N_DEV, -1)
    out, off = {}, 0
    for n in names:
        ax = SHARD_AXIS[n]
        sz = math.prod(shapes[n])
        t = flat[:, off:off + sz].reshape((N_DEV,) + tuple(shapes[n]))
        t = jnp.moveaxis(t, 0, ax)
        full = list(shapes[n])
        full[ax] *= N_DEV
        out[n] = t.reshape(full)
        off += sz
    return out


def pack_grads(grads, names, row_mult):
    parts = []
    for n in names:
        g = grads[n].astype(F32)
        ax = SHARD_AXIS[n]
        if ax is None:
            parts.append(jnp.broadcast_to(g.reshape(1, -1), (N_DEV, g.size)))
        else:
            shp = list(g.shape)
            t = g.reshape(shp[:ax] + [N_DEV, shp[ax] // N_DEV] + shp[ax + 1:])
            parts.append(jnp.moveaxis(t, ax, 0).reshape(N_DEV, -1))
    flat = jnp.concatenate(parts, axis=1)
    n = _pad_to(flat.shape[1], PACK_W * row_mult)
    return jnp.pad(flat, ((0, 0), (0, n - flat.shape[1]))).reshape(N_DEV, n // PACK_W, PACK_W)


def _peers():
    x, y, c = lax.axis_index("x"), lax.axis_index("y"), lax.axis_index("c")
    me = 4 * x + 2 * y + c
    peers = []
    for k in range(1, N_DEV):
        px = 1 - x if k & 4 else x
        py = 1 - y if k & 2 else y
        pc = 1 - c if k & 1 else c
        peers.append(((px, py, pc), 4 * px + 2 * py + pc))
    return me, peers


def all_gather(name, local):
    rows, w = local.shape

    def body(src, out, send_sems, recv_sems, local_sem):
        me, peers = _peers()
        mine = pltpu.make_async_copy(src, out.at[me], local_sem)
        mine.start()
        sends = []
        for k, (dev, _) in enumerate(peers):
            cp = pltpu.make_async_remote_copy(src_ref=src, dst_ref=out.at[me], send_sem=send_sems.at[k],
                                              recv_sem=recv_sems.at[k], device_id=dev,
                                              device_id_type=pl.DeviceIdType.MESH)
            cp.start()
            sends.append(cp)
        for k, (dev, idx) in enumerate(peers):
            pltpu.make_async_remote_copy(src_ref=src, dst_ref=out.at[idx], send_sem=send_sems.at[k],
                                         recv_sem=recv_sems.at[k], device_id=dev,
                                         device_id_type=pl.DeviceIdType.MESH).wait_recv()
        for cp in sends:
            cp.wait_send()
        mine.wait()

    return pl.pallas_call(
        body, out_shape=SDS((N_DEV, rows, w), local.dtype),
        in_specs=[pl.BlockSpec(memory_space=pltpu.HBM)], out_specs=pl.BlockSpec(memory_space=pltpu.HBM),
        scratch_shapes=[pltpu.SemaphoreType.DMA((N_DEV - 1,)), pltpu.SemaphoreType.DMA((N_DEV - 1,)),
                        pltpu.SemaphoreType.DMA], name=name)(local)


def exchange_slices(name, parts):
    _, rows, w = parts.shape

    def body(src, out, send_sems, recv_sems, local_sem):
        me, peers = _peers()
        mine = pltpu.make_async_copy(src.at[me], out.at[me], local_sem)
        mine.start()
        sends = []
        for k, (dev, idx) in enumerate(peers):
            cp = pltpu.make_async_remote_copy(src_ref=src.at[idx], dst_ref=out.at[me], send_sem=send_sems.at[k],
                                              recv_sem=recv_sems.at[k], device_id=dev,
                                              device_id_type=pl.DeviceIdType.MESH)
            cp.start()
            sends.append(cp)
        for k, (dev, idx) in enumerate(peers):
            pltpu.make_async_remote_copy(src_ref=src.at[idx], dst_ref=out.at[idx], send_sem=send_sems.at[k],
                                         recv_sem=recv_sems.at[k], device_id=dev,
                                         device_id_type=pl.DeviceIdType.MESH).wait_recv()
        for cp in sends:
            cp.wait_send()
        mine.wait()

    return pl.pallas_call(
        body, out_shape=SDS(parts.shape, parts.dtype),
        in_specs=[pl.BlockSpec(memory_space=pltpu.HBM)], out_specs=pl.BlockSpec(memory_space=pltpu.HBM),
        scratch_shapes=[pltpu.SemaphoreType.DMA((N_DEV - 1,)), pltpu.SemaphoreType.DMA((N_DEV - 1,)),
                        pltpu.SemaphoreType.DMA], name=name)(parts)


ADAM_ROWS = 128


def sum_adamw(parts, w, m, v):
    _, rows, width = parts.shape
    tr = ADAM_ROWS

    def body(p_ref, w_ref, m_ref, v_ref, g_out, d_out, m_out, v_out):
        g = p_ref[0]
        for q in range(1, N_DEV):
            g = g + p_ref[q]
        m_new = ADAM_B1 * m_ref[...] + (1.0 - ADAM_B1) * g
        v_new = ADAM_B2 * v_ref[...] + (1.0 - ADAM_B2) * jnp.square(g)
        m_hat = m_new / (1.0 - ADAM_B1 ** ADAM_STEP)
        v_hat = v_new / (1.0 - ADAM_B2 ** ADAM_STEP)
        g_out[...] = g
        d_out[...] = -ADAM_LR * (m_hat / (jnp.sqrt(v_hat) + ADAM_EPS) + ADAM_WD * w_ref[...])
        m_out[...] = m_new
        v_out[...] = v_new

    blk = pl.BlockSpec((tr, width), lambda i: (i, 0))
    return pl.pallas_call(
        body, grid=(rows // tr,),
        in_specs=[pl.BlockSpec((N_DEV, tr, width), lambda i: (0, i, 0)), blk, blk, blk],
        out_specs=[blk] * 4, out_shape=[SDS((rows, width), F32)] * 4, name="sum_adamw",
        compiler_params=_params(("arbitrary",)))(parts, w, m, v)


def kernel(x, mem, positions, norm_g, ffn_w_gate, ffn_w_up, ffn_w_down, mem_w_kv, a_w_in, a_q_norm, a_kv_norm, a_w_uq, a_w_ukv, a_w_out, b_w_in, b_w_out, c_w_in, c_conv_w, c_conv_b, c_ln_g, c_ln_b, c_w_out, d_w_in, d_conv_w, d_w_out, loss_target, m_norm_g, m_ffn_w_gate, m_ffn_w_up, m_ffn_w_down, m_mem_w_kv, m_a_w_in, m_a_q_norm, m_a_kv_norm, m_a_w_uq, m_a_w_ukv, m_a_w_out, m_b_w_in, m_b_w_out, m_c_w_in, m_c_conv_w, m_c_conv_b, m_c_ln_g, m_c_ln_b, m_c_w_out, m_d_w_in, m_d_conv_w, m_d_w_out, v_norm_g, v_ffn_w_gate, v_ffn_w_up, v_ffn_w_down, v_mem_w_kv, v_a_w_in, v_a_q_norm, v_a_kv_norm, v_a_w_uq, v_a_w_ukv, v_a_w_out, v_b_w_in, v_b_w_out, v_c_w_in, v_c_conv_w, v_c_conv_b, v_c_ln_g, v_c_ln_b, v_c_w_out, v_d_w_in, v_d_conv_w, v_d_w_out):
    a = dict(zip(IN_NAMES, (x, mem, positions, norm_g, ffn_w_gate, ffn_w_up, ffn_w_down, mem_w_kv, a_w_in, a_q_norm, a_kv_norm, a_w_uq, a_w_ukv, a_w_out, b_w_in, b_w_out, c_w_in, c_conv_w, c_conv_b, c_ln_g, c_ln_b, c_w_out, d_w_in, d_conv_w, d_w_out, loss_target, m_norm_g, m_ffn_w_gate, m_ffn_w_up, m_ffn_w_down, m_mem_w_kv, m_a_w_in, m_a_q_norm, m_a_kv_norm, m_a_w_uq, m_a_w_ukv, m_a_w_out, m_b_w_in, m_b_w_out, m_c_w_in, m_c_conv_w, m_c_conv_b, m_c_ln_g, m_c_ln_b, m_c_w_out, m_d_w_in, m_d_conv_w, m_d_w_out, v_norm_g, v_ffn_w_gate, v_ffn_w_up, v_ffn_w_down, v_mem_w_kv, v_a_w_in, v_a_q_norm, v_a_kv_norm, v_a_w_uq, v_a_w_ukv, v_a_w_out, v_b_w_in, v_b_w_out, v_c_w_in, v_c_conv_w, v_c_conv_b, v_c_ln_g, v_c_ln_b, v_c_w_out, v_d_w_in, v_d_conv_w, v_d_w_out)))
    shapes = {n: tuple(a[n].shape) for n in WEIGHTS}
    big = all_gather("gather_big", pack_local([a[n] for n in BIG], BF, 16))
    small = all_gather("gather_small", pack_local([a[n] for n in SMALL], F32, 8))
    W = unpack_gathered(big, BIG, shapes)
    W.update(unpack_gathered(small, SMALL, shapes))
    W['a_q_norm'], W['a_kv_norm'] = a['a_q_norm'], a['a_kv_norm']
    loss, grad_x, grads = local_step(a['x'][0], a['mem'][0], a['positions'][0], a['loss_target'][0], W)
    parts = exchange_slices("scatter_grads", pack_grads(grads, WEIGHTS, ADAM_ROWS))
    packed = [pack_local([a[p + n] for n in WEIGHTS], F32, ADAM_ROWS) for p in ('', 'm_', 'v_')]
    res = sum_adamw(parts, *packed)
    outs = [unpack_local(r, WEIGHTS, shapes) for r in res]
    loss = lax.psum(loss, MESH_AXES)
    return (loss, grad_x[None], *[o[n] for o in outs for n in WEIGHTS])
```

```python
import functools
import math

import jax
import jax.numpy as jnp
from jax import lax
from jax.experimental import pallas as pl
from jax.experimental.pallas import tpu as pltpu

BF, F32 = jnp.bfloat16, jnp.float32
SDS = jax.ShapeDtypeStruct

D_MODEL, DEPTH, D_FF = 1024, 4, 2816
HEAD_DIM, MEM_HEADS, MEM_WIDTH, N_MEM = 64, 4, 256, 256
MLA_HEADS, MLA_Q_LORA, MLA_KV_LORA, MLA_NOPE, MLA_ROPE, MLA_V = 12, 384, 256, 64, 32, 64
MLA_QK = MLA_NOPE + MLA_ROPE
ROPE_THETA = 10000.0
DIL_GROUPS = ((128, 1), (512, 4), (2048, 16))
DIL_HEADS, DIL_HALF = 8, 64
DIL_QKV = len(DIL_GROUPS) * 3 * DIL_HEADS * HEAD_DIM
ALIBI_MAX = 8.0
CONV_CH, CONV_WIDTH, SC_CH, SC_WIDTH = 768, 31, 768, 3
EPS, NEG = 1e-6, -1e30
ADAM_LR, ADAM_B1, ADAM_B2, ADAM_EPS, ADAM_WD, ADAM_STEP = 0.001, 0.9, 0.999, 1e-08, 0.01, 10

N_DEV = 8
MESH_AXES = ("x", "y", "c")
FF_SHARD = D_FF // N_DEV
FF_SLOT = 384
D_FFP = N_DEV * FF_SLOT
LOG2E = 1.4426950408889634
V7X_VMEM_LIMIT = 48 * 1024 * 1024
HALO = 16
PACK_W = 1024

WEIGHTS = ['norm_g', 'ffn_w_gate', 'ffn_w_up', 'ffn_w_down', 'mem_w_kv', 'a_w_in', 'a_q_norm', 'a_kv_norm',
           'a_w_uq', 'a_w_ukv', 'a_w_out', 'b_w_in', 'b_w_out', 'c_w_in', 'c_conv_w', 'c_conv_b', 'c_ln_g',
           'c_ln_b', 'c_w_out', 'd_w_in', 'd_conv_w', 'd_w_out']
SHARD_AXIS = dict(norm_g=2, ffn_w_gate=3, ffn_w_up=3, ffn_w_down=2, mem_w_kv=1, a_w_in=1, a_q_norm=None,
                  a_kv_norm=None, a_w_uq=2, a_w_ukv=2, a_w_out=1, b_w_in=2, b_w_out=2, c_w_in=2, c_conv_w=2,
                  c_conv_b=1, c_ln_g=1, c_ln_b=1, c_w_out=1, d_w_in=2, d_conv_w=2, d_w_out=1)
SMALL = ['norm_g', 'c_conv_w', 'c_conv_b', 'c_ln_g', 'c_ln_b', 'd_conv_w']
BIG = [n for n in WEIGHTS if n not in SMALL and SHARD_AXIS[n] is not None]
IN_NAMES = (['x', 'mem', 'positions'] + WEIGHTS + ['loss_target'] + ['m_' + n for n in WEIGHTS]
            + ['v_' + n for n in WEIGHTS])


def _params(sem):
    return pltpu.CompilerParams(dimension_semantics=sem, vmem_limit_bytes=V7X_VMEM_LIMIT)


def _dot(a, b, ca=1, cb=0):
    return lax.dot_general(a, b, (((ca,), (cb,)), ((), ())), preferred_element_type=F32)


def mm_call(name, accs, M, N, K, *, tm, tn, tk, epilogue, outs, tiles=(), cols=(), fulls=(), reds=()):
    tm, tn, tk = min(tm, M), min(tn, N), min(tk, K)
    assert M % tm == 0 and N % tn == 0 and K % tk == 0, (name, M, N, K, tm, tn, tk)
    gj, gi, gk = N // tn, M // tm, K // tk
    flat = [p for acc in accs for p in acc]
    in_specs, args = [], []
    for (a, b, ta, tb) in flat:
        in_specs.append(pl.BlockSpec((tk, tm), lambda j, i, k: (k, i)) if ta else pl.BlockSpec((tm, tk), lambda j, i, k: (i, k)))
        in_specs.append(pl.BlockSpec((tn, tk), lambda j, i, k: (j, k)) if tb else pl.BlockSpec((tk, tn), lambda j, i, k: (k, j)))
        args += [a, b]
    for t in tiles:
        in_specs.append(pl.BlockSpec((tm, tn), lambda j, i, k: (i, j)))
        args.append(t)
    for c in cols:
        in_specs.append(pl.BlockSpec((c.shape[0], tn), lambda j, i, k: (0, j)))
        args.append(c)
    for f in fulls:
        in_specs.append(pl.BlockSpec(f.shape, lambda j, i, k: (0,) * f.ndim))
        args.append(f)
    out_shape = [SDS((M, N), dt) for dt in outs] + [SDS((r, N), F32) for r in reds]
    out_specs = ([pl.BlockSpec((tm, tn), lambda j, i, k: (i, j)) for _ in outs]
                 + [pl.BlockSpec((r, tn), lambda j, i, k: (0, j)) for r in reds])
    n_in = len(args)
    n_acc = len(accs)

    def body(*refs):
        in_refs = refs[:n_in]
        out_refs = refs[n_in:n_in + len(outs)]
        red_refs = refs[n_in + len(outs):n_in + len(outs) + len(reds)]
        acc_refs = refs[n_in + len(outs) + len(reds):]
        i, k = pl.program_id(1), pl.program_id(2)
        sums, idx = [], 0
        for acc in accs:
            tot = None
            for (_, _, ta, tb) in acc:
                av = in_refs[2 * idx][...].astype(BF)
                bv = in_refs[2 * idx + 1][...].astype(BF)
                idx += 1
                d = _dot(av, bv, 0 if ta else 1, 1 if tb else 0)
                tot = d if tot is None else tot + d
            sums.append(tot)

        def finish(vals):
            p = 2 * len(flat)
            tv = [r[...] for r in in_refs[p:p + len(tiles)]]
            cv = [r[...] for r in in_refs[p + len(tiles):p + len(tiles) + len(cols)]]
            fv = [r[...] for r in in_refs[p + len(tiles) + len(cols):]]
            ov, rv = epilogue(vals, tv, cv, fv)
            for r, v in zip(out_refs, ov):
                r[...] = v.astype(r.dtype)
            for r, v in zip(red_refs, rv):
                @pl.when(i == 0)
                def _(r=r, v=v):
                    r[...] = v

                @pl.when(i > 0)
                def _(r=r, v=v):
                    r[...] += v

        if gk == 1:
            finish(sums)
        else:
            @pl.when(k == 0)
            def _():
                for r, v in zip(acc_refs, sums):
                    r[...] = v

            @pl.when(k > 0)
            def _():
                for r, v in zip(acc_refs, sums):
                    r[...] += v

            @pl.when(k == gk - 1)
            def _():
                finish([r[...] for r in acc_refs])

    scratch = [] if gk == 1 else [pltpu.VMEM((tm, tn), F32) for _ in range(n_acc)]
    res = pl.pallas_call(body, grid=(gj, gi, gk), in_specs=in_specs, out_specs=out_specs, out_shape=out_shape,
                         scratch_shapes=scratch, name=name,
                         compiler_params=_params(("arbitrary", "arbitrary", "arbitrary")))(*args)
    return res


def _plain(dt):
    return dict(epilogue=lambda a, t, c, f: ([a[0]], []), outs=[dt])


def rows_call(name, fn, S, ts, tiled, fulls, outs, reds=()):
    ts = min(ts, S)
    assert S % ts == 0, (name, S, ts)
    in_specs = ([pl.BlockSpec((ts, t.shape[1]), lambda i: (i, 0)) for t in tiled]
                + [pl.BlockSpec(f.shape, lambda i, n=f.ndim: (0,) * n) for f in fulls])
    out_shape = [SDS((S, w), dt) for (w, dt) in outs] + [SDS(rw, F32) for rw in reds]
    out_specs = ([pl.BlockSpec((ts, w), lambda i: (i, 0)) for (w, _) in outs]
                 + [pl.BlockSpec(rw, lambda i: (0, 0)) for rw in reds])
    nt, nf, no = len(tiled), len(fulls), len(outs)

    def body(*refs):
        i = pl.program_id(0)
        tv = [r[...] for r in refs[:nt]]
        fv = [r[...] for r in refs[nt:nt + nf]]
        ov, rv = fn(i, tv, fv)
        for r, v in zip(refs[nt + nf:nt + nf + no], ov):
            r[...] = v.astype(r.dtype)
        for r, v in zip(refs[nt + nf + no:], rv):
            @pl.when(i == 0)
            def _(r=r, v=v):
                r[...] = v

            @pl.when(i > 0)
            def _(r=r, v=v):
                r[...] += v

    return pl.pallas_call(body, grid=(S // ts,), in_specs=in_specs, out_specs=out_specs, out_shape=out_shape,
                          name=name, compiler_params=_params(("arbitrary",)))(*tiled, *fulls)


def hrows_call(name, fn, H, S, ts, tiled, outs):
    ts = min(ts, S)
    in_specs = [pl.BlockSpec((1, ts, t.shape[2]), lambda h, i: (h, i, 0)) for t in tiled]
    out_shape = [SDS((H, S, w), dt) for (w, dt) in outs]
    out_specs = [pl.BlockSpec((1, ts, w), lambda h, i: (h, i, 0)) for (w, _) in outs]
    nt = len(tiled)

    def body(*refs):
        ov = fn([r[0] for r in refs[:nt]])
        for r, v in zip(refs[nt:], ov):
            r[0] = v.astype(r.dtype)

    return pl.pallas_call(body, grid=(H, S // ts), in_specs=in_specs, out_specs=out_specs, out_shape=out_shape,
                          name=name, compiler_params=_params(("arbitrary", "arbitrary")))(*tiled)


def _rms(x, g):
    r = lax.rsqrt(jnp.mean(x * x, axis=-1, keepdims=True) + EPS)
    return x * r * g


def _rms_bwd(x, g, dy):
    r = lax.rsqrt(jnp.mean(x * x, axis=-1, keepdims=True) + EPS)
    xh = x * r
    dg = jnp.sum(dy * xh, axis=0, keepdims=True)
    dxh = dy * g
    dx = r * (dxh - xh * jnp.mean(dxh * xh, axis=-1, keepdims=True))
    return dx, dg


def _silu(x):
    return x * jax.nn.sigmoid(x)


def _dsilu(x):
    s = jax.nn.sigmoid(x)
    return s * (1.0 + x * (1.0 - s))


def rms_rows(name, x, g):
    S, w = x.shape
    return rows_call(name, lambda i, t, f: ([_rms(t[0].astype(F32), f[0])], []), S, 512, [x], [g], [(w, BF)])[0]


def ffn_fwd(tag, x0, g_in, g_out, wg, wu, wd):
    S = x0.shape[0]
    h = rms_rows(tag + "_rms", x0, g_in)

    def ep_up(a, t, c, f):
        return [a[0], a[1], _silu(a[0]) * a[1]], []

    gate, up, act = mm_call(tag + "_up", [[(h, wg, False, False)], [(h, wu, False, False)]], S, D_FFP, D_MODEL,
                            tm=1024, tn=512, tk=1024, epilogue=ep_up, outs=[BF, BF, BF])

    def ep_down(a, t, c, f):
        return [a[0], t[0] + 0.5 * _rms(a[0], c[0])], []

    y, x1 = mm_call(tag + "_down", [[(act, wd, False, False)]], S, D_MODEL, D_FFP, tm=512, tn=1024, tk=1024,
                    epilogue=ep_down, outs=[BF, F32], tiles=[x0], cols=[g_out])
    return x1, dict(x0=x0, h=h, gate=gate, up=up, act=act, y=y)


def ffn_bwd(tag, dx1, sv, g_in, g_out, wg, wu, wd):
    S = dx1.shape[0]

    def post(i, t, f):
        dx, dg = _rms_bwd(t[1].astype(F32), f[0], 0.5 * t[0])
        return [dx], [dg]

    da, dg_out = rows_call(tag + "_bpost", post, S, 512, [dx1, sv['y']], [g_out], [(D_MODEL, BF)], [(1, D_MODEL)])

    def ep_act(a, t, c, f):
        g, u = t[0].astype(F32), t[1].astype(F32)
        return [a[0] * u * _dsilu(g), a[0] * _silu(g)], []

    dgate, dup = mm_call(tag + "_bact", [[(da, wd, False, True)]], S, D_FFP, D_MODEL, tm=1024, tn=512, tk=1024,
                         epilogue=ep_act, outs=[BF, BF], tiles=[sv['gate'], sv['up']])
    dwd, = mm_call(tag + "_dwd", [[(sv['act'], da, True, False)]], D_FFP, D_MODEL, S, tm=1024, tn=1024, tk=512,
                   **_plain(F32))
    dwg, dwu = mm_call(tag + "_dwgu", [[(sv['h'], dgate, True, False)], [(sv['h'], dup, True, False)]],
                       D_MODEL, D_FFP, S, tm=1024, tn=512, tk=1024,
                       epilogue=lambda a, t, c, f: ([a[0], a[1]], []), outs=[F32, F32])

    def ep_in(a, t, c, f):
        dx, dg = _rms_bwd(t[0], c[0], a[0])
        return [t[1] + dx], [dg]

    (dx0, dg_in) = mm_call(tag + "_bin", [[(dgate, wg, False, True), (dup, wu, False, True)]], S, D_MODEL, D_FFP,
                           tm=512, tn=1024, tk=1024, epilogue=ep_in, outs=[F32], tiles=[sv['x0'], dx1], cols=[g_in],
                           reds=[1])
    return dx0, dict(g_in=dg_in, g_out=dg_out, wg=dwg, wu=dwu, wd=dwd)


def mem_kv_fwd(tag, mem, g6, w_kv):
    def body(mem_ref, g_ref, w_ref, kv_ref):
        kv_ref[...] = _dot(_rms(mem_ref[...], g_ref[...]).astype(BF), w_ref[...])

    return pl.pallas_call(body, out_shape=SDS((N_MEM, 2 * MEM_WIDTH), F32), name=tag + "_memkv",
                          compiler_params=pltpu.CompilerParams(vmem_limit_bytes=V7X_VMEM_LIMIT))(mem, g6, w_kv)


def mem_kv_bwd(tag, mem, g6, w_kv, dkv):
    def body(mem_ref, g_ref, w_ref, dkv_ref, dw_ref, dg_ref):
        x = mem_ref[...]
        r = lax.rsqrt(jnp.mean(x * x, axis=-1, keepdims=True) + EPS)
        xh = x * r
        d = dkv_ref[...].astype(BF)
        dw_ref[...] = _dot((xh * g_ref[...]).astype(BF), d, 0, 0)
        dmem_n = _dot(d, w_ref[...], 1, 1)
        dg_ref[...] = jnp.sum(dmem_n * xh, axis=0, keepdims=True)

    return pl.pallas_call(body, out_shape=(SDS((D_MODEL, 2 * MEM_WIDTH), F32), SDS((1, D_MODEL), F32)),
                          name=tag + "_memkvb",
                          compiler_params=pltpu.CompilerParams(vmem_limit_bytes=V7X_VMEM_LIMIT))(mem, g6, w_kv, dkv)


def _block_diag_kv(kv):
    k = kv[:, :MEM_WIDTH].reshape(N_MEM, MEM_HEADS, HEAD_DIM)
    v = kv[:, MEM_WIDTH:].reshape(N_MEM, MEM_HEADS, HEAD_DIM)
    eye = jnp.eye(MEM_HEADS, dtype=kv.dtype)
    kbd = jnp.einsum('nhd,hg->hdgn', k, eye).reshape(MEM_WIDTH, MEM_HEADS * N_MEM)
    vbd = jnp.einsum('nhd,hg->hngd', v, eye).reshape(MEM_HEADS * N_MEM, MEM_WIDTH)
    return kbd.astype(BF), vbd.astype(BF)


def _mem_probs(q, kbd):
    s = _dot(q, kbd) * (HEAD_DIM ** -0.5)
    ps = []
    for h in range(MEM_HEADS):
        sh = s[:, h * N_MEM:(h + 1) * N_MEM]
        e = jnp.exp(sh - jnp.max(sh, axis=-1, keepdims=True))
        ps.append(e / jnp.sum(e, axis=-1, keepdims=True))
    return ps


def mem_attn_fwd(tag, q_mem, kbd, vbd):
    S = q_mem.shape[0]

    def fn(i, t, f):
        p = jnp.concatenate(_mem_probs(t[0], f[0]), axis=1).astype(BF)
        return [_dot(p, f[1])], []

    return rows_call(tag + "_mem", fn, S, 512, [q_mem], [kbd, vbd], [(MEM_WIDTH, BF)])[0]


def mem_attn_bwd(tag, q_mem, dmo, kbd, vbd):
    S = q_mem.shape[0]

    def fn(i, t, f):
        q, do = t
        ps = _mem_probs(q, f[0])
        dp = _dot(do, f[1], 1, 1)
        dss = []
        for h in range(MEM_HEADS):
            dph = dp[:, h * N_MEM:(h + 1) * N_MEM]
            dss.append(ps[h] * (dph - jnp.sum(dph * ps[h], axis=-1, keepdims=True)) * (HEAD_DIM ** -0.5))
        ds = jnp.concatenate(dss, axis=1).astype(BF)
        p = jnp.concatenate(ps, axis=1).astype(BF)
        return [_dot(ds, f[0], 1, 1)], [_dot(q, ds, 0, 0), _dot(p, do, 0, 0)]

    dq, dkbd, dvbd = rows_call(tag + "_memb", fn, S, 512, [q_mem, dmo], [kbd, vbd], [(MEM_WIDTH, BF)],
                               [(MEM_WIDTH, MEM_HEADS * N_MEM), (MEM_HEADS * N_MEM, MEM_WIDTH)])
    dk = jnp.einsum('hdhn->nhd', dkbd.reshape(MEM_HEADS, HEAD_DIM, MEM_HEADS, N_MEM)).reshape(N_MEM, MEM_WIDTH)
    dv = jnp.einsum('hnhd->nhd', dvbd.reshape(MEM_HEADS, N_MEM, MEM_HEADS, HEAD_DIM)).reshape(N_MEM, MEM_WIDTH)
    return dq, jnp.concatenate([dk, dv], axis=1)


def _rope_tables(positions):
    half = MLA_ROPE // 2
    inv = ROPE_THETA ** (-jnp.arange(half, dtype=F32) / half)
    ang = positions.astype(F32)[:, None] * inv
    cos, sin = jnp.cos(ang), jnp.sin(ang)
    cos32, sin32 = jnp.concatenate([cos, cos], 1), jnp.concatenate([sin, sin], 1)
    idx = jnp.arange(MLA_ROPE)
    p32 = (jnp.where((idx[:, None] == idx[None, :] + half), -1.0, 0.0)
           + jnp.where((idx[:, None] + half == idx[None, :]), 1.0, 0.0)).astype(F32)
    S = positions.shape[0]
    ones = jnp.ones((S, MLA_NOPE), F32)
    cq = jnp.tile(jnp.concatenate([ones, cos32], 1), (1, MLA_HEADS))
    sq = jnp.tile(jnp.concatenate([0.0 * ones, sin32], 1), (1, MLA_HEADS))
    pq = jnp.zeros((MLA_QK, MLA_QK), F32).at[MLA_NOPE:, MLA_NOPE:].set(p32)
    pq = jnp.kron(jnp.eye(MLA_HEADS, dtype=F32), pq)
    return dict(cos32=cos32, sin32=sin32, p32=p32.astype(BF), p32t=p32.T.astype(BF), cq=cq, sq=sq,
                pq=pq.astype(BF), pqt=pq.T.astype(BF))


def flash_fwd(tag, qh, kh, vh1, dv, tq=1024, tk=1024):
    H, S, dq = qh.shape
    dv1 = vh1.shape[2]
    tq, tk = min(tq, S), min(tk, S)
    nk = S // tk

    def body(q_ref, k_ref, v_ref, o_ref, lse_ref, m_s, acc_s):
        j = pl.program_id(2)

        @pl.when(j == 0)
        def _():
            m_s[...] = jnp.full(m_s.shape, NEG, F32)
            acc_s[...] = jnp.zeros(acc_s.shape, F32)

        s = _dot(q_ref[0], k_ref[0], 1, 1)
        m_old = m_s[...]
        m_new = jnp.maximum(m_old, jnp.max(s, axis=-1, keepdims=True))
        p = jnp.exp2((s - m_new).astype(BF))
        acc_s[...] = jnp.exp2(m_old - m_new) * acc_s[...] + _dot(p, v_ref[0])
        m_s[...] = m_new

        @pl.when(j == nk - 1)
        def _():
            acc = acc_s[...]
            l = acc[:, dv:dv + 1]
            o_ref[0] = (acc[:, :dv] / l).astype(o_ref.dtype)
            lse_ref[0] = m_s[...] + jnp.log2(l)

    return pl.pallas_call(
        body, grid=(H, S // tq, nk),
        in_specs=[pl.BlockSpec((1, tq, dq), lambda h, i, j: (h, i, 0)),
                  pl.BlockSpec((1, tk, dq), lambda h, i, j: (h, j, 0)),
                  pl.BlockSpec((1, tk, dv1), lambda h, i, j: (h, j, 0))],
        out_specs=[pl.BlockSpec((1, tq, dv), lambda h, i, j: (h, i, 0)),
                   pl.BlockSpec((1, tq, 1), lambda h, i, j: (h, i, 0))],
        out_shape=[SDS((H, S, dv), BF), SDS((H, S, 1), F32)],
        scratch_shapes=[pltpu.VMEM((tq, 1), F32), pltpu.VMEM((tq, dv1), F32)],
        name=tag + "_flash", compiler_params=_params(("arbitrary", "arbitrary", "arbitrary")))(qh, kh, vh1)


def flash_bwd(tag, qh, kh, vh, doh, lse_row, delta_row, tq=512, tk=1024):
    H, S, dq = qh.shape
    dv = vh.shape[2]
    tq, tk = min(tq, S), min(tk, S)

    def body(k_ref, v_ref, q_ref, do_ref, lse_ref, dl_ref, dk_ref, dv_ref, dq_ref):
        j, i = pl.program_id(1), pl.program_id(2)
        k, v, q, do = k_ref[0], v_ref[0], q_ref[0], do_ref[0]
        pt = jnp.exp2((_dot(k, q, 1, 1) - lse_ref[0]).astype(BF))
        dst = pt * (_dot(v, do, 1, 1) - dl_ref[0]).astype(BF)
        dvc = _dot(pt, do)
        dkc = _dot(dst, q) * (1.0 / LOG2E)
        dqc = _dot(dst, k, 0, 0)

        @pl.when(i == 0)
        def _():
            dk_ref[0] = dkc
            dv_ref[0] = dvc

        @pl.when(i > 0)
        def _():
            dk_ref[0] += dkc
            dv_ref[0] += dvc

        rows = pl.ds(pl.multiple_of(i * tq, tq), tq)

        @pl.when(j == 0)
        def _():
            dq_ref[0, rows, :] = dqc

        @pl.when(j > 0)
        def _():
            dq_ref[0, rows, :] += dqc

    return pl.pallas_call(
        body, grid=(H, S // tk, S // tq),
        in_specs=[pl.BlockSpec((1, tk, dq), lambda h, j, i: (h, j, 0)),
                  pl.BlockSpec((1, tk, dv), lambda h, j, i: (h, j, 0)),
                  pl.BlockSpec((1, tq, dq), lambda h, j, i: (h, i, 0)),
                  pl.BlockSpec((1, tq, dv), lambda h, j, i: (h, i, 0)),
                  pl.BlockSpec((1, 1, tq), lambda h, j, i: (h, 0, i)),
                  pl.BlockSpec((1, 1, tq), lambda h, j, i: (h, 0, i))],
        out_specs=[pl.BlockSpec((1, tk, dq), lambda h, j, i: (h, j, 0)),
                   pl.BlockSpec((1, tk, dv), lambda h, j, i: (h, j, 0)),
                   pl.BlockSpec((1, S, dq), lambda h, j, i: (h, 0, 0))],
        out_shape=[SDS((H, S, dq), F32), SDS((H, S, dv), F32), SDS((H, S, dq), F32)],
        name=tag + "_flashb",
        compiler_params=_params(("arbitrary", "arbitrary", "arbitrary")))(kh, vh, qh, doh, lse_row, delta_row)


def _to_heads(t, H):
    return t.reshape(t.shape[0], H, -1).transpose(1, 0, 2)


def _from_heads(t):
    return t.transpose(1, 0, 2).reshape(t.shape[1], -1)


def mla_fwd(tag, z, rope, q_g, kv_g, w_uq, w_ukv):
    S = z.shape[0]
    o1, o2, o3 = MLA_Q_LORA, MLA_Q_LORA + MLA_KV_LORA, MLA_Q_LORA + MLA_KV_LORA + MLA_ROPE
    c_q, c_kv, k_r, q_mem = z[:, :o1], z[:, o1:o2], z[:, o2:o3], z[:, o3:]

    def pre(i, t, f):
        kr = t[2].astype(F32)
        krot = kr * t[3] + _dot(t[2], f[2]) * t[4]
        return [_rms(t[0].astype(F32), f[0]), _rms(t[1].astype(F32), f[1]), krot], []

    qn, kvn, krot = rows_call(tag + "_pre", pre, S, 512, [c_q, c_kv, k_r, rope['cos32'], rope['sin32']],
                              [q_g, kv_g, rope['p32']], [(MLA_Q_LORA, BF), (MLA_KV_LORA, BF), (MLA_ROPE, BF)])

    qscale = MLA_QK ** -0.5 * LOG2E

    def ep_q(a, t, c, f):
        return [(a[0] * t[0] + _dot(a[0].astype(BF), f[0]) * t[1]) * qscale], []

    nq = MLA_HEADS * MLA_QK
    q, = mm_call(tag + "_uq", [[(qn, w_uq, False, False)]], S, nq, MLA_Q_LORA, tm=512, tn=nq, tk=MLA_Q_LORA,
                 epilogue=ep_q, outs=[BF], tiles=[rope['cq'], rope['sq']], fulls=[rope['pq']])
    nkv = MLA_HEADS * (MLA_NOPE + MLA_V)
    kv, = mm_call(tag + "_ukv", [[(kvn, w_ukv, False, False)]], S, nkv, MLA_KV_LORA, tm=1024, tn=nkv, tk=MLA_KV_LORA,
                  **_plain(BF))
    qh = _to_heads(q, MLA_HEADS)
    kv3 = kv.reshape(S, MLA_HEADS, MLA_NOPE + MLA_V)
    kh = jnp.concatenate([kv3[:, :, :MLA_NOPE].transpose(1, 0, 2),
                          jnp.broadcast_to(krot[None], (MLA_HEADS, S, MLA_ROPE))], axis=2)
    vh = kv3[:, :, MLA_NOPE:].transpose(1, 0, 2)
    ones = jnp.ones((MLA_HEADS, S, 1), BF)
    vh1 = jnp.concatenate([vh, ones, jnp.zeros((MLA_HEADS, S, 128 - MLA_V - 1), BF)], axis=2)
    oh, lse = flash_fwd(tag, qh, kh, vh1, MLA_V)
    sv = dict(c_q=c_q, c_kv=c_kv, k_r=k_r, qn=qn, kvn=kvn, qh=qh, kh=kh, vh=vh, oh=oh, lse=lse)
    return _from_heads(oh), q_mem, sv


def mla_bwd(tag, do, sv, rope, q_g, kv_g, w_uq, w_ukv):
    S = do.shape[0]
    H = MLA_HEADS
    doh = _to_heads(do, H)
    delta, = hrows_call(tag + "_delta",
                        lambda t: [jnp.sum(t[0].astype(F32) * t[1].astype(F32), axis=-1, keepdims=True)],
                        H, S, 1024, [doh, sv['oh']], [(1, F32)])
    dkh, dvh, dqh = flash_bwd(tag, sv['qh'], sv['kh'], sv['vh'], doh, sv['lse'].reshape(H, 1, S),
                              delta.reshape(H, 1, S))
    nq = H * MLA_QK
    dq_rot = _from_heads(dqh)
    scale = MLA_QK ** -0.5

    def unrope(i, t, f):
        d = t[0] * scale
        return [d * t[1] + _dot((d * t[2]).astype(BF), f[0])], []

    dq, = rows_call(tag + "_unrope", unrope, S, 512, [dq_rot, rope['cq'], rope['sq']], [rope['pqt']], [(nq, BF)])

    def ep_norm(a, t, c, f):
        dx, dg = _rms_bwd(t[0].astype(F32), c[0], a[0])
        return [dx], [dg]

    dc_q, dq_g = mm_call(tag + "_bqn", [[(dq, w_uq, False, True)]], S, MLA_Q_LORA, nq, tm=1024, tn=MLA_Q_LORA, tk=nq,
                         epilogue=ep_norm, outs=[BF], tiles=[sv['c_q']], cols=[q_g], reds=[1])
    dw_uq, = mm_call(tag + "_dwuq", [[(sv['qn'], dq, True, False)]], MLA_Q_LORA, nq, S, tm=MLA_Q_LORA, tn=nq, tk=1024,
                     **_plain(F32))
    nkv = H * (MLA_NOPE + MLA_V)
    dkv = jnp.concatenate([dkh[:, :, :MLA_NOPE], dvh], axis=2).transpose(1, 0, 2).reshape(S, nkv).astype(BF)
    dc_kv, dkv_g = mm_call(tag + "_bkvn", [[(dkv, w_ukv, False, True)]], S, MLA_KV_LORA, nkv, tm=1024, tn=MLA_KV_LORA,
                           tk=nkv, epilogue=ep_norm, outs=[BF], tiles=[sv['c_kv']], cols=[kv_g], reds=[1])
    dw_ukv, = mm_call(tag + "_dwukv", [[(sv['kvn'], dkv, True, False)]], MLA_KV_LORA, nkv, S, tm=MLA_KV_LORA, tn=nkv,
                      tk=1024, **_plain(F32))
    dkr_heads = dkh[:, :, MLA_NOPE:].transpose(1, 0, 2).reshape(S, H * MLA_ROPE)
    hsum = jnp.tile(jnp.eye(MLA_ROPE, dtype=F32), (H, 1))

    def krope(i, t, f):
        d = jnp.dot(t[0], f[0], preferred_element_type=F32, precision=lax.Precision.HIGHEST)
        return [d * t[1] + _dot((d * t[2]).astype(BF), f[1])], []

    dk_r, = rows_call(tag + "_bkr", krope, S, 512, [dkr_heads, rope['cos32'], rope['sin32']], [hsum, rope['p32t']],
                      [(MLA_ROPE, BF)])
    grads = dict(a_q_norm=dq_g, a_kv_norm=dkv_g, a_w_uq=dw_uq, a_w_ukv=dw_ukv)
    return [dc_q, dc_kv, dk_r], grads


def _dil_perm(t, dil):
    H, S, d = t.shape
    return t.reshape(H, S // dil, dil, d).transpose(0, 2, 1, 3).reshape(H, S, d)


def _dil_unperm(t, dil):
    H, S, d = t.shape
    return t.reshape(H, dil, S // dil, d).transpose(0, 2, 1, 3).reshape(H, S, d)


def _band(rpos, cpos, S, seg_shift):
    dist = jnp.abs(rpos - cpos)
    valid = ((dist <= DIL_HALF) & (cpos >= 0) & (cpos < S) & (rpos >= 0) & (rpos < S)
             & (jnp.right_shift(rpos, seg_shift) == jnp.right_shift(cpos, seg_shift)))
    return dist, valid


def _halo_specs(blk, halo, n_halo_blocks, w):
    r = blk // halo
    return [pl.BlockSpec((1, halo, w), lambda h, i: (h, jnp.maximum(i * r - 1, 0), 0)),
            pl.BlockSpec((1, blk, w), lambda h, i: (h, i, 0)),
            pl.BlockSpec((1, halo, w), lambda h, i: (h, jnp.minimum((i + 1) * r, n_halo_blocks - 1), 0))]


def dil_fwd(tag, q, k, v, slopes, dil, tq=256):
    H, S, dh = q.shape
    tq = min(tq, S)
    seg_shift = int(math.log2(S // dil))
    hb = DIL_HALF
    scale = dh ** -0.5

    def body(sl_ref, q_ref, kp, kc, kn, vp, vc, vn, o_ref, lse_ref):
        i = pl.program_id(1)
        kcat = jnp.concatenate([kp[0], kc[0], kn[0]], axis=0)
        vcat = jnp.concatenate([vp[0], vc[0], vn[0]], axis=0)
        s = _dot(q_ref[0], kcat, 1, 1) * scale
        shp = (tq, tq + 2 * hb)
        qpos = i * tq + lax.broadcasted_iota(jnp.int32, shp, 0)
        kpos = i * tq - hb + lax.broadcasted_iota(jnp.int32, shp, 1)
        dist, valid = _band(qpos, kpos, S, seg_shift)
        s = s - sl_ref[0][:, :1] * (dist * dil).astype(F32)
        s = jnp.where(valid, s, NEG)
        m = jnp.max(s, axis=-1, keepdims=True)
        lse = m + jnp.log(jnp.sum(jnp.exp(s - m), axis=-1, keepdims=True))
        p = jnp.exp(s - lse).astype(BF)
        o_ref[0] = _dot(p, vcat)
        lse_ref[0] = lse

    return pl.pallas_call(
        body, grid=(H, S // tq),
        in_specs=([pl.BlockSpec((1, 1, 128), lambda h, i: (h, 0, 0)), pl.BlockSpec((1, tq, dh), lambda h, i: (h, i, 0))]
                  + _halo_specs(tq, hb, S // hb, dh) + _halo_specs(tq, hb, S // hb, dh)),
        out_specs=[pl.BlockSpec((1, tq, dh), lambda h, i: (h, i, 0)), pl.BlockSpec((1, tq, 1), lambda h, i: (h, i, 0))],
        out_shape=[SDS((H, S, dh), F32), SDS((H, S, 1), F32)],
        name=tag + "_dil", compiler_params=_params(("arbitrary", "arbitrary")))(slopes, q, k, k, k, v, v, v)


def dil_bwd_q(tag, q, k, v, do, lse, deff, slopes, dil, tq=256):
    H, S, dh = q.shape
    tq = min(tq, S)
    seg_shift = int(math.log2(S // dil))
    hb = DIL_HALF
    scale = dh ** -0.5

    def body(sl_ref, q_ref, do_ref, lse_ref, de_ref, kp, kc, kn, vp, vc, vn, dq_ref):
        i = pl.program_id(1)
        kcat = jnp.concatenate([kp[0], kc[0], kn[0]], axis=0)
        vcat = jnp.concatenate([vp[0], vc[0], vn[0]], axis=0)
        s = _dot(q_ref[0], kcat, 1, 1) * scale
        shp = (tq, tq + 2 * hb)
        qpos = i * tq + lax.broadcasted_iota(jnp.int32, shp, 0)
        kpos = i * tq - hb + lax.broadcasted_iota(jnp.int32, shp, 1)
        dist, valid = _band(qpos, kpos, S, seg_shift)
        s = s - sl_ref[0][:, :1] * (dist * dil).astype(F32)
        p = jnp.where(valid, jnp.exp(jnp.where(valid, s, NEG) - lse_ref[0]), 0.0)
        dp = _dot(do_ref[0], vcat, 1, 1)
        ds = (p * (dp - de_ref[0]) * scale).astype(BF)
        dq_ref[0] = _dot(ds, kcat)

    row = lambda w: pl.BlockSpec((1, tq, w), lambda h, i: (h, i, 0))
    return pl.pallas_call(
        body, grid=(H, S // tq),
        in_specs=([pl.BlockSpec((1, 1, 128), lambda h, i: (h, 0, 0)), row(dh), row(dh), row(1), row(1)]
                  + _halo_specs(tq, hb, S // hb, dh) + _halo_specs(tq, hb, S // hb, dh)),
        out_specs=row(dh), out_shape=SDS((H, S, dh), F32),
        name=tag + "_dilbq",
        compiler_params=_params(("arbitrary", "arbitrary")))(slopes, q, do, lse, deff, k, k, k, v, v, v)


def dil_bwd_kv(tag, q, k, v, do, lse_row, deff_row, slopes, dil, tk=256):
    H, S, dh = q.shape
    tk = min(tk, S)
    seg_shift = int(math.log2(S // dil))
    hq = 128
    scale = dh ** -0.5
    r = tk // hq
    nhb = S // hq
    stat_specs = [pl.BlockSpec((1, 1, hq), lambda h, j: (h, 0, jnp.maximum(j * r - 1, 0))),
                  pl.BlockSpec((1, 1, tk), lambda h, j: (h, 0, j)),
                  pl.BlockSpec((1, 1, hq), lambda h, j: (h, 0, jnp.minimum((j + 1) * r, nhb - 1)))]

    def body(sl_ref, k_ref, v_ref, qp, qc, qn, dop, doc, don, lp, lc, ln, ep, ec, en, dk_ref, dv_ref):
        j = pl.program_id(1)
        qcat = jnp.concatenate([qp[0], qc[0], qn[0]], axis=0)
        docat = jnp.concatenate([dop[0], doc[0], don[0]], axis=0)
        lse = jnp.concatenate([lp[0], lc[0], ln[0]], axis=1)
        deff = jnp.concatenate([ep[0], ec[0], en[0]], axis=1)
        st = _dot(k_ref[0], qcat, 1, 1) * scale
        shp = (tk, tk + 2 * hq)
        kpos = j * tk + lax.broadcasted_iota(jnp.int32, shp, 0)
        qpos = j * tk - hq + lax.broadcasted_iota(jnp.int32, shp, 1)
        dist, valid = _band(kpos, qpos, S, seg_shift)
        st = st - sl_ref[0][:, :1] * (dist * dil).astype(F32)
        pt = jnp.where(valid, jnp.exp(jnp.where(valid, st, NEG) - jnp.where(valid, lse, 0.0)), 0.0)
        dpt = _dot(v_ref[0], docat, 1, 1)
        dst = (pt * (dpt - deff) * scale).astype(BF)
        dv_ref[0] = _dot(pt.astype(BF), docat)
        dk_ref[0] = _dot(dst, qcat)

    cur = lambda w: pl.BlockSpec((1, tk, w), lambda h, j: (h, j, 0))
    return pl.pallas_call(
        body, grid=(H, S // tk),
        in_specs=([pl.BlockSpec((1, 1, 128), lambda h, j: (h, 0, 0)), cur(dh), cur(dh)]
                  + _halo_specs(tk, hq, nhb, dh) + _halo_specs(tk, hq, nhb, dh) + stat_specs + stat_specs),
        out_specs=[cur(dh), cur(dh)], out_shape=[SDS((H, S, dh), F32), SDS((H, S, dh), F32)],
        name=tag + "_dilbkv", compiler_params=_params(("arbitrary", "arbitrary")))(
            slopes, k, v, q, q, q, do, do, do, lse_row, lse_row, lse_row, deff_row, deff_row, deff_row)


def _alibi_slopes():
    n = len(DIL_GROUPS) * DIL_HEADS
    s = 2.0 ** (-ALIBI_MAX * (jnp.arange(n, dtype=F32) + 1.0) / n)
    return jnp.broadcast_to(s.reshape(len(DIL_GROUPS), DIL_HEADS, 1, 1), (len(DIL_GROUPS), DIL_HEADS, 1, 128))


def _merge_weights(lses):
    m = jnp.maximum(jnp.maximum(lses[0], lses[1]), lses[2])
    es = [jnp.exp(l - m) for l in lses]
    tot = es[0] + es[1] + es[2]
    return [e / tot for e in es]


def dilated_fwd(tag, z):
    S = z.shape[0]
    H = DIL_HEADS
    ng = len(DIL_GROUPS)
    zd = z[:, :DIL_QKV].reshape(S, ng, 3, H, HEAD_DIM)
    q_mem = z[:, DIL_QKV:]
    slopes = _alibi_slopes()
    qkv, outs, lses = [], [], []
    for g, (_, dil) in enumerate(DIL_GROUPS):
        t = [_dil_perm(zd[:, g, c].transpose(1, 0, 2), dil) for c in range(3)]
        o, lse = dil_fwd(f"{tag}_g{g}", t[0], t[1], t[2], slopes[g], dil)
        qkv.append(t)
        outs.append(_dil_unperm(o, dil))
        lses.append(_dil_unperm(lse, dil))

    def merge(t):
        w = _merge_weights(t[3:6])
        return [w[0] * t[0] + w[1] * t[1] + w[2] * t[2]]

    oh, = hrows_call(tag + "_merge", merge, H, S, 1024, outs + lses, [(HEAD_DIM, BF)])
    return _from_heads(oh), q_mem, dict(qkv=qkv, outs=outs, lses=lses)


def dilated_bwd(tag, do, sv):
    S = do.shape[0]
    H = DIL_HEADS
    slopes = _alibi_slopes()
    doh = _to_heads(do, H)

    def merge_b(t):
        d = t[0].astype(F32)
        w = _merge_weights(t[4:7])
        dws = [jnp.sum(d * t[1 + g], axis=-1, keepdims=True) for g in range(3)]
        c = w[0] * dws[0] + w[1] * dws[1] + w[2] * dws[2]
        return [w[g] * d for g in range(3)] + [w[g] * c for g in range(3)]

    res = hrows_call(tag + "_mergeb", merge_b, H, S, 1024, [doh] + sv['outs'] + sv['lses'],
                     [(HEAD_DIM, BF)] * 3 + [(1, F32)] * 3)
    parts = []
    for g, (_, dil) in enumerate(DIL_GROUPS):
        q, k, v = sv['qkv'][g]
        do_g = _dil_perm(res[g], dil)
        deff = _dil_perm(res[3 + g], dil)
        lse = _dil_perm(sv['lses'][g], dil)
        dq = dil_bwd_q(f"{tag}_g{g}", q, k, v, do_g, lse, deff, slopes[g], dil)
        dk, dv = dil_bwd_kv(f"{tag}_g{g}", q, k, v, do_g, lse.reshape(H, 1, S), deff.reshape(H, 1, S), slopes[g], dil)
        parts.append(jnp.stack([_dil_unperm(t, dil).transpose(1, 0, 2) for t in (dq, dk, dv)], axis=1))
    dzd = jnp.stack(parts, axis=1)
    return [dzd.reshape(S, DIL_QKV).astype(BF)]


def _halo_specs2(ts, w, S):
    r = ts // HALO
    return [pl.BlockSpec((HALO, w), lambda i: (jnp.maximum(i * r - 1, 0), 0)),
            pl.BlockSpec((ts, w), lambda i: (i, 0)),
            pl.BlockSpec((HALO, w), lambda i: (jnp.minimum((i + 1) * r, S // HALO - 1), 0))]


def _cat_rows(i, ts, S, refs):
    v = jnp.concatenate([r[...].astype(F32) for r in refs], axis=0)
    pos = i * ts - HALO + lax.broadcasted_iota(jnp.int32, (ts + 2 * HALO, 1), 0)
    return v, (pos >= 0) & (pos < S)


def _conv_taps(buf_ref, w, ts, width, flip):
    acc = None
    for k in range(width):
        off = HALO + ((width // 2 - k) if flip else (k - width // 2))
        term = buf_ref[pl.ds(off, ts), :] * w[k:k + 1, :]
        acc = term if acc is None else acc + term
    return acc


def conformer_fwd(tag, z, conv_w, conv_b, ln_g, ln_b, ts=256):
    S = z.shape[0]
    ts = min(ts, S)
    C = CONV_CH
    a, gate, q_mem = z[:, :C], z[:, C:2 * C], z[:, 2 * C:]

    def body(ap, ac, an, gp, gc, gn, w_ref, b_ref, lg_ref, lb_ref, u_ref, u1_ref, buf):
        i = pl.program_id(0)
        av, ok = _cat_rows(i, ts, S, (ap, ac, an))
        gv, _ = _cat_rows(i, ts, S, (gp, gc, gn))
        buf[...] = jnp.where(ok, av * jax.nn.sigmoid(gv), 0.0)
        u1 = _conv_taps(buf, w_ref[...], ts, CONV_WIDTH, False) + b_ref[...]
        mu = jnp.mean(u1, axis=-1, keepdims=True)
        xc = u1 - mu
        y = xc * lax.rsqrt(jnp.mean(xc * xc, axis=-1, keepdims=True) + EPS) * lg_ref[...] + lb_ref[...]
        u_ref[...] = _silu(y).astype(u_ref.dtype)
        u1_ref[...] = u1

    full = lambda t: pl.BlockSpec(t.shape, lambda i: (0, 0))
    u, u1 = pl.pallas_call(
        body, grid=(S // ts,),
        in_specs=_halo_specs2(ts, C, S) + _halo_specs2(ts, C, S) + [full(conv_w), full(conv_b), full(ln_g), full(ln_b)],
        out_specs=[pl.BlockSpec((ts, C), lambda i: (i, 0))] * 2, out_shape=[SDS((S, C), BF), SDS((S, C), F32)],
        scratch_shapes=[pltpu.VMEM((ts + 2 * HALO, C), F32)], name=tag + "_conf",
        compiler_params=_params(("arbitrary",)))(a, a, a, gate, gate, gate, conv_w, conv_b, ln_g, ln_b)
    return u, q_mem, dict(a=a, gate=gate, u1=u1)


def conformer_bwd(tag, du, sv, conv_w, conv_b, ln_g, ln_b, ts=256):
    S = du.shape[0]
    ts = min(ts, S)
    C = CONV_CH

    def ln_b_fn(i, t, f):
        u1, d = t[1], t[0].astype(F32)
        mu = jnp.mean(u1, axis=-1, keepdims=True)
        xc = u1 - mu
        r = lax.rsqrt(jnp.mean(xc * xc, axis=-1, keepdims=True) + EPS)
        xh = xc * r
        dy = d * _dsilu(xh * f[0] + f[1])
        dxh = dy * f[0]
        du1 = r * (dxh - jnp.mean(dxh, axis=-1, keepdims=True) - xh * jnp.mean(dxh * xh, axis=-1, keepdims=True))
        return [du1], [jnp.sum(dy * xh, axis=0, keepdims=True), jnp.sum(dy, axis=0, keepdims=True),
                       jnp.sum(du1, axis=0, keepdims=True)]

    du1, dln_g, dln_b, dconv_b = rows_call(tag + "_confb1", ln_b_fn, S, 512, [du, sv['u1']], [ln_g, ln_b],
                                           [(C, F32)], [(1, C), (1, C), (1, C)])
    wpad = CONV_WIDTH + 1

    def body(dp, dc, dn, ap, ac, an, gp, gc, gn, w_ref, da_ref, dg_ref, dw_ref, ubuf, dbuf):
        i = pl.program_id(0)
        av, ok = _cat_rows(i, ts, S, (ap, ac, an))
        gv, _ = _cat_rows(i, ts, S, (gp, gc, gn))
        dv, _ = _cat_rows(i, ts, S, (dp, dc, dn))
        sg = jax.nn.sigmoid(gv)
        ubuf[...] = jnp.where(ok, av * sg, 0.0)
        dbuf[...] = jnp.where(ok, dv, 0.0)
        du0 = _conv_taps(dbuf, w_ref[...], ts, CONV_WIDTH, True)
        a_c, s_c = av[HALO:HALO + ts], sg[HALO:HALO + ts]
        da_ref[...] = (du0 * s_c).astype(da_ref.dtype)
        dg_ref[...] = (du0 * a_c * s_c * (1.0 - s_c)).astype(dg_ref.dtype)
        d_c = dbuf[pl.ds(HALO, ts), :]

        @pl.when(i == 0)
        def _():
            dw_ref[...] = jnp.zeros(dw_ref.shape, F32)

        for k in range(CONV_WIDTH):
            off = HALO + k - CONV_WIDTH // 2
            dw_ref[k:k + 1, :] += jnp.sum(d_c * ubuf[pl.ds(off, ts), :], axis=0, keepdims=True)

    full = lambda t: pl.BlockSpec(t.shape, lambda i: (0, 0))
    da, dgate, dw = pl.pallas_call(
        body, grid=(S // ts,),
        in_specs=_halo_specs2(ts, C, S) * 3 + [full(conv_w)],
        out_specs=[pl.BlockSpec((ts, C), lambda i: (i, 0))] * 2 + [pl.BlockSpec((wpad, C), lambda i: (0, 0))],
        out_shape=[SDS((S, C), BF), SDS((S, C), BF), SDS((wpad, C), F32)],
        scratch_shapes=[pltpu.VMEM((ts + 2 * HALO, C), F32)] * 2, name=tag + "_confb2",
        compiler_params=_params(("arbitrary",)))(du1, du1, du1, sv['a'], sv['a'], sv['a'], sv['gate'], sv['gate'],
                                                 sv['gate'], conv_w)
    grads = dict(c_conv_w=dw[:CONV_WIDTH], c_conv_b=dconv_b, c_ln_g=dln_g, c_ln_b=dln_b)
    return [da, dgate], grads


def shortconv_fwd(tag, z, conv_w, ts=256):
    S = z.shape[0]
    ts = min(ts, S)
    C = SC_CH
    bg, cg, hx, q_mem = z[:, :C], z[:, C:2 * C], z[:, 2 * C:3 * C], z[:, 3 * C:]

    def body(b_ref, cp, cc, cn, hp, hc, hn, w_ref, o_ref, buf):
        i = pl.program_id(0)
        cv, ok = _cat_rows(i, ts, S, (cp, cc, cn))
        hv, _ = _cat_rows(i, ts, S, (hp, hc, hn))
        buf[...] = jnp.where(ok, cv * hv, 0.0)
        o_ref[...] = (b_ref[...].astype(F32) * _conv_taps(buf, w_ref[...], ts, SC_WIDTH, False)).astype(o_ref.dtype)

    o = pl.pallas_call(
        body, grid=(S // ts,),
        in_specs=[pl.BlockSpec((ts, C), lambda i: (i, 0))] + _halo_specs2(ts, C, S) * 2
        + [pl.BlockSpec(conv_w.shape, lambda i: (0, 0))],
        out_specs=pl.BlockSpec((ts, C), lambda i: (i, 0)), out_shape=SDS((S, C), BF),
        scratch_shapes=[pltpu.VMEM((ts + 2 * HALO, C), F32)], name=tag + "_sconv",
        compiler_params=_params(("arbitrary",)))(bg, cg, cg, cg, hx, hx, hx, conv_w)
    return o, q_mem, dict(bg=bg, cg=cg, hx=hx)


def shortconv_bwd(tag, do, sv, conv_w, ts=256):
    S = do.shape[0]
    ts = min(ts, S)
    C = SC_CH
    wpad = 8

    def body(dp, dc, dn, bp, bc, bn, cp, cc, cn, hp, hc, hn, w_ref, db_ref, dcg_ref, dhx_ref, dw_ref, mbuf, dbuf):
        i = pl.program_id(0)
        dov, ok = _cat_rows(i, ts, S, (dp, dc, dn))
        bv, _ = _cat_rows(i, ts, S, (bp, bc, bn))
        cv, _ = _cat_rows(i, ts, S, (cp, cc, cn))
        hv, _ = _cat_rows(i, ts, S, (hp, hc, hn))
        mbuf[...] = jnp.where(ok, cv * hv, 0.0)
        dbuf[...] = jnp.where(ok, dov * bv, 0.0)
        conv = _conv_taps(mbuf, w_ref[...], ts, SC_WIDTH, False)
        db_ref[...] = (dov[HALO:HALO + ts] * conv).astype(db_ref.dtype)
        dm = _conv_taps(dbuf, w_ref[...], ts, SC_WIDTH, True)
        dcg_ref[...] = (dm * hv[HALO:HALO + ts]).astype(dcg_ref.dtype)
        dhx_ref[...] = (dm * cv[HALO:HALO + ts]).astype(dhx_ref.dtype)
        d_c = dbuf[pl.ds(HALO, ts), :]

        @pl.when(i == 0)
        def _():
            dw_ref[...] = jnp.zeros(dw_ref.shape, F32)

        for k in range(SC_WIDTH):
            off = HALO + k - SC_WIDTH // 2
            dw_ref[k:k + 1, :] += jnp.sum(d_c * mbuf[pl.ds(off, ts), :], axis=0, keepdims=True)

    dbg, dcg, dhx, dw = pl.pallas_call(
        body, grid=(S // ts,),
        in_specs=_halo_specs2(ts, C, S) * 4 + [pl.BlockSpec(conv_w.shape, lambda i: (0, 0))],
        out_specs=[pl.BlockSpec((ts, C), lambda i: (i, 0))] * 3 + [pl.BlockSpec((wpad, C), lambda i: (0, 0))],
        out_shape=[SDS((S, C), BF)] * 3 + [SDS((wpad, C), F32)],
        scratch_shapes=[pltpu.VMEM((ts + 2 * HALO, C), F32)] * 2, name=tag + "_sconvb",
        compiler_params=_params(("arbitrary",)))(do, do, do, sv['bg'], sv['bg'], sv['bg'], sv['cg'], sv['cg'],
                                                 sv['cg'], sv['hx'], sv['hx'], sv['hx'], conv_w)
    return [dbg, dcg, dhx], dict(d_conv_w=dw[:SC_WIDTH])


MIXERS = ('a', 'b', 'c', 'd')


def _mixer_weights(W, m):
    return W[m + '_w_in'][0], W[m + '_w_out'][0]


def layer_fwd(i, x, mem, rope, W):
    tag = f"l{i}"
    g = W['norm_g'][i]
    gr = lambda k: g[k:k + 1]
    m = MIXERS[i % len(MIXERS)]
    sv = {}
    x1, sv['ffn0'] = ffn_fwd(tag + "_f0", x, gr(0), gr(1), W['ffn_w_gate'][i, 0], W['ffn_w_up'][i, 0],
                             W['ffn_w_down'][i, 0])
    w_in, w_out = _mixer_weights(W, m)
    S = x.shape[0]
    h2 = rms_rows(tag + "_rms2", x1, gr(2))
    n_in = w_in.shape[1]
    z, = mm_call(tag + "_in", [[(h2, w_in, False, False)]], S, n_in, D_MODEL, tm=1024,
                 tn=(256 if n_in % 256 == 0 else n_in), tk=1024, **_plain(BF))
    if m == 'a':
        o, q_mem, sv['mix'] = mla_fwd(tag, z, rope, W['a_q_norm'], W['a_kv_norm'], W['a_w_uq'][0], W['a_w_ukv'][0])
    elif m == 'b':
        o, q_mem, sv['mix'] = dilated_fwd(tag, z)
    elif m == 'c':
        o, q_mem, sv['mix'] = conformer_fwd(tag, z, W['c_conv_w'][0], W['c_conv_b'], W['c_ln_g'], W['c_ln_b'])
    else:
        o, q_mem, sv['mix'] = shortconv_fwd(tag, z, W['d_conv_w'][0])
    kv = mem_kv_fwd(tag, mem, gr(6), W['mem_w_kv'][i])
    kbd, vbd = _block_diag_kv(kv)
    mo = mem_attn_fwd(tag, q_mem, kbd, vbd)
    cat = jnp.concatenate([o, mo], axis=1)
    n_out = cat.shape[1]

    def ep_out(a, t, c, f):
        return [a[0], t[0] + _rms(a[0], c[0])], []

    y2, x2 = mm_call(tag + "_out", [[(cat, w_out, False, False)]], S, D_MODEL, n_out, tm=512, tn=1024, tk=n_out,
                     epilogue=ep_out, outs=[BF, F32], tiles=[x1], cols=[gr(3)])
    sv.update(x1=x1, h2=h2, q_mem=q_mem, kbd=kbd, vbd=vbd, cat=cat, y2=y2)
    x3, sv['ffn1'] = ffn_fwd(tag + "_f1", x2, gr(4), gr(5), W['ffn_w_gate'][i, 1], W['ffn_w_up'][i, 1],
                             W['ffn_w_down'][i, 1])
    return x3, sv


def layer_bwd(i, dx3, sv, mem, rope, W):
    tag = f"l{i}"
    g = W['norm_g'][i]
    gr = lambda k: g[k:k + 1]
    m = MIXERS[i % len(MIXERS)]
    S = dx3.shape[0]
    dx2, f1 = ffn_bwd(tag + "_f1", dx3, sv['ffn1'], gr(4), gr(5), W['ffn_w_gate'][i, 1], W['ffn_w_up'][i, 1],
                      W['ffn_w_down'][i, 1])
    w_in, w_out = _mixer_weights(W, m)
    n_in, n_out = w_in.shape[1], w_out.shape[0]

    def post(ii, t, f):
        dx, dg = _rms_bwd(t[1].astype(F32), f[0], t[0])
        return [dx], [dg]

    dy2, dg3 = rows_call(tag + "_bpost", post, S, 512, [dx2, sv['y2']], [gr(3)], [(D_MODEL, BF)], [(1, D_MODEL)])
    dcat, = mm_call(tag + "_bout", [[(dy2, w_out, False, True)]], S, n_out, D_MODEL, tm=1024,
                    tn=(256 if n_out % 256 == 0 else n_out), tk=1024, **_plain(BF))
    dw_out, = mm_call(tag + "_dwout", [[(sv['cat'], dy2, True, False)]], n_out, D_MODEL, S, tm=n_out, tn=1024, tk=512,
                      **_plain(F32))
    n_o = n_out - MEM_WIDTH
    do, dmo = dcat[:, :n_o], dcat[:, n_o:]
    dq_mem, dkv = mem_attn_bwd(tag, sv['q_mem'], dmo, sv['kbd'], sv['vbd'])
    dw_kv, dg6 = mem_kv_bwd(tag, mem, gr(6), W['mem_w_kv'][i], dkv)
    if m == 'a':
        dzs, mg = mla_bwd(tag, do, sv['mix'], rope, W['a_q_norm'], W['a_kv_norm'], W['a_w_uq'][0], W['a_w_ukv'][0])
    elif m == 'b':
        dzs, mg = dilated_bwd(tag, do, sv['mix']), {}
    elif m == 'c':
        dzs, mg = conformer_bwd(tag, do, sv['mix'], W['c_conv_w'][0], W['c_conv_b'], W['c_ln_g'], W['c_ln_b'])
    else:
        dzs, mg = shortconv_bwd(tag, do, sv['mix'], W['d_conv_w'][0])
    dz = jnp.concatenate(dzs + [dq_mem], axis=1)
    dw_in, = mm_call(tag + "_dwin", [[(sv['h2'], dz, True, False)]], D_MODEL, n_in, S, tm=1024,
                     tn=(256 if n_in % 256 == 0 else n_in), tk=512, **_plain(F32))

    def ep_in(a, t, c, f):
        dx, dg = _rms_bwd(t[0], c[0], a[0])
        return [t[1] + dx], [dg]

    dx1, dg2 = mm_call(tag + "_bin", [[(dz, w_in, False, True)]], S, D_MODEL, n_in, tm=512, tn=1024,
                       tk=(n_in // 2 if n_in % 256 == 0 else n_in), epilogue=ep_in, outs=[F32],
                       tiles=[sv['x1'], dx2], cols=[gr(2)], reds=[1])
    dx0, f0 = ffn_bwd(tag + "_f0", dx1, sv['ffn0'], gr(0), gr(1), W['ffn_w_gate'][i, 0], W['ffn_w_up'][i, 0],
                      W['ffn_w_down'][i, 0])
    dg = jnp.concatenate([f0['g_in'], f0['g_out'], dg2, dg3, f1['g_in'], f1['g_out'], dg6], axis=0)
    grads = dict(norm_g=dg, ffn_w_gate=jnp.stack([f0['wg'], f1['wg']]), ffn_w_up=jnp.stack([f0['wu'], f1['wu']]),
                 ffn_w_down=jnp.stack([f0['wd'], f1['wd']]), mem_w_kv=dw_kv)
    grads[m + '_w_in'] = dw_in[None]
    grads[m + '_w_out'] = dw_out[None]
    for k, v in mg.items():
        grads[k] = v if k in ('a_q_norm', 'a_kv_norm', 'c_conv_b', 'c_ln_g', 'c_ln_b') else v[None]
    return dx0, grads


def local_step(x, mem, positions, target, W):
    S = x.shape[0]
    rope = _rope_tables(positions)
    saved = []
    for i in range(DEPTH):
        x, sv = layer_fwd(i, x, mem, rope, W)
        saved.append(sv)

    def loss_fn(i, t, f):
        e = t[0] - t[1]
        part = jnp.sum(0.5 * jnp.mean(e * e, axis=-1, keepdims=True), axis=0, keepdims=True)
        return [e * (1.0 / D_MODEL)], [jnp.broadcast_to(part, (1, 128))]

    dx, loss = rows_call("loss", loss_fn, S, 512, [x, target], [], [(D_MODEL, F32)], [(1, 128)])
    per_layer = []
    for i in reversed(range(DEPTH)):
        dx, gr = layer_bwd(i, dx, saved[i], mem, rope, W)
        per_layer.append(gr)
    per_layer = per_layer[::-1]
    grads = {}
    for n in ('norm_g', 'ffn_w_gate', 'ffn_w_up', 'ffn_w_down', 'mem_w_kv'):
        grads[n] = jnp.stack([p[n] for p in per_layer])
    for p in per_layer:
        for n, v in p.items():
            if n not in grads:
                grads[n] = v
    return loss[0, 0], dx, grads


def _local_shape(full_shape, axis):
    s = list(full_shape)
    if axis is not None:
        s[axis] //= N_DEV
    return tuple(s)


def _pad_to(n, m):
    return -(-n // m) * m


def pack_local(arrs, dtype, row_mult):
    flat = jnp.concatenate([a.reshape(-1).astype(dtype) for a in arrs])
    n = _pad_to(flat.shape[0], PACK_W * row_mult)
    return jnp.pad(flat, (0, n - flat.shape[0])).reshape(n // PACK_W, PACK_W)


def unpack_local(buf, names, shapes):
    flat = buf.reshape(-1)
    out, off = {}, 0
    for n in names:
        sz = math.prod(shapes[n])
        out[n] = flat[off:off + sz].reshape(shapes[n])
        off += sz
    return out


def unpack_gathered(buf, names, shapes):
    flat = buf.reshape(N_DEV, -1)
    out, off = {}, 0
    for n in names:
        ax = SHARD_AXIS[n]
        sz = math.prod(shapes[n])
        t = flat[:, off:off + sz].reshape((N_DEV,) + tuple(shapes[n]))
        t = jnp.moveaxis(t, 0, ax)
        full = list(shapes[n])
        full[ax] *= N_DEV
        out[n] = t.reshape(full)
        off += sz
    return out


def pack_grads(grads, names, row_mult):
    parts = []
    for n in names:
        g = grads[n].astype(F32)
        ax = SHARD_AXIS[n]
        if ax is None:
            parts.append(jnp.broadcast_to(g.reshape(1, -1), (N_DEV, g.size)))
        else:
            shp = list(g.shape)
            t = g.reshape(shp[:ax] + [N_DEV, shp[ax] // N_DEV] + shp[ax + 1:])
            parts.append(jnp.moveaxis(t, ax, 0).reshape(N_DEV, -1))
    flat = jnp.concatenate(parts, axis=1)
    n = _pad_to(flat.shape[1], PACK_W * row_mult)
    return jnp.pad(flat, ((0, 0), (0, n - flat.shape[1]))).reshape(N_DEV, n // PACK_W, PACK_W)


COL_SHARDED = ['a_w_uq', 'a_w_ukv', 'b_w_in', 'b_w_out', 'c_w_in', 'd_w_in']
ROW_SHARDED_1024 = ['a_w_out', 'c_w_out', 'd_w_out']
FLAT = SMALL + ['a_q_norm', 'a_kv_norm']


def wire_weights(a):
    n_ff = DEPTH * 2 * D_MODEL
    pad = lambda t: jnp.pad(t.reshape(n_ff, FF_SHARD), ((0, 0), (0, FF_SLOT - FF_SHARD)))
    w384 = jnp.concatenate([pad(a['ffn_w_gate']), pad(a['ffn_w_up'])], axis=0)
    w1024 = jnp.concatenate([a['ffn_w_down'].reshape(DEPTH * 2 * FF_SHARD, D_MODEL)]
                            + [a[n][0] for n in ROW_SHARDED_1024], axis=0)
    rest = [a['mem_w_kv'].reshape(-1, 2 * MEM_WIDTH), a['a_w_in'][0]] + [a[n][0] for n in COL_SHARDED]
    return [t.astype(BF) for t in [w384, w1024] + rest]


def unwire_weights(g):
    n_ff = DEPTH * 2 * D_MODEL
    W = {}
    t = g[0].transpose(1, 0, 2).reshape(2 * n_ff, D_FFP)
    W['ffn_w_gate'] = t[:n_ff].reshape(DEPTH, 2, D_MODEL, D_FFP)
    W['ffn_w_up'] = t[n_ff:].reshape(DEPTH, 2, D_MODEL, D_FFP)
    n_dn = DEPTH * 2 * FF_SHARD
    dn = g[1][:, :n_dn].reshape(N_DEV, DEPTH, 2, FF_SHARD, D_MODEL)
    dn = jnp.pad(dn, ((0, 0), (0, 0), (0, 0), (0, FF_SLOT - FF_SHARD), (0, 0)))
    W['ffn_w_down'] = dn.transpose(1, 2, 0, 3, 4).reshape(DEPTH, 2, D_FFP, D_MODEL)
    rows = D_MODEL // N_DEV
    for k, n in enumerate(ROW_SHARDED_1024):
        W[n] = g[1][:, n_dn + k * rows:n_dn + (k + 1) * rows].reshape(1, D_MODEL, D_MODEL)
    W['mem_w_kv'] = g[2].reshape(N_DEV, DEPTH, rows, 2 * MEM_WIDTH).transpose(1, 0, 2, 3).reshape(DEPTH, D_MODEL, -1)
    W['a_w_in'] = g[3].reshape(1, D_MODEL, -1)
    for k, n in enumerate(COL_SHARDED):
        t = g[4 + k]
        W[n] = t.transpose(1, 0, 2).reshape(1, t.shape[1], N_DEV * t.shape[2])
    return W


def wire_grads(grads):
    n_ff = DEPTH * 2 * D_MODEL
    slot = lambda t: t.reshape(n_ff, N_DEV, FF_SLOT).transpose(1, 0, 2)
    p384 = jnp.concatenate([slot(grads['ffn_w_gate']), slot(grads['ffn_w_up'])], axis=1)
    dn = grads['ffn_w_down'].reshape(DEPTH, 2, N_DEV, FF_SLOT, D_MODEL)[:, :, :, :FF_SHARD]
    dn = dn.transpose(2, 0, 1, 3, 4).reshape(N_DEV, DEPTH * 2 * FF_SHARD, D_MODEL)
    rows = D_MODEL // N_DEV
    p1024 = jnp.concatenate([dn] + [grads[n].reshape(N_DEV, rows, D_MODEL) for n in ROW_SHARDED_1024], axis=1)
    kv = grads['mem_w_kv'].reshape(DEPTH, N_DEV, rows, 2 * MEM_WIDTH).transpose(1, 0, 2, 3)
    rest = [kv.reshape(N_DEV, DEPTH * rows, -1), grads['a_w_in'].reshape(N_DEV, rows, -1)]
    for n in COL_SHARDED:
        t = grads[n][0]
        rest.append(t.reshape(t.shape[0], N_DEV, -1).transpose(1, 0, 2))
    return [t.astype(BF) for t in [p384, p1024] + rest]


def _peers():
    x, y, c = lax.axis_index("x"), lax.axis_index("y"), lax.axis_index("c")
    me = 4 * x + 2 * y + c
    peers = []
    for k in range(1, N_DEV):
        px = 1 - x if k & 4 else x
        py = 1 - y if k & 2 else y
        pc = 1 - c if k & 1 else c
        peers.append(((px, py, pc), 4 * px + 2 * py + pc))
    return me, peers


def _exchange(name, arrays, scatter):
    n = len(arrays)

    def body(*refs):
        srcs, outs = refs[:n], refs[n:2 * n]
        send_sems, recv_sems, local_sems = refs[2 * n:]
        me, peers = _peers()
        pending = []
        for a, (src, out) in enumerate(zip(srcs, outs)):
            mine = pltpu.make_async_copy(src.at[me] if scatter else src, out.at[me], local_sems.at[a])
            mine.start()
            pending.append(mine)
        sends = []
        for a, (src, out) in enumerate(zip(srcs, outs)):
            for k, (dev, idx) in enumerate(peers):
                cp = pltpu.make_async_remote_copy(src_ref=src.at[idx] if scatter else src, dst_ref=out.at[me],
                                                  send_sem=send_sems.at[a, k], recv_sem=recv_sems.at[a, k],
                                                  device_id=dev, device_id_type=pl.DeviceIdType.MESH)
                cp.start()
                sends.append(cp)
        for a, (src, out) in enumerate(zip(srcs, outs)):
            for k, (dev, idx) in enumerate(peers):
                pltpu.make_async_remote_copy(src_ref=src.at[idx] if scatter else src, dst_ref=out.at[idx],
                                             send_sem=send_sems.at[a, k], recv_sem=recv_sems.at[a, k],
                                             device_id=dev, device_id_type=pl.DeviceIdType.MESH).wait_recv()
        for cp in sends:
            cp.wait_send()
        for mine in pending:
            mine.wait()

    out_shape = [SDS(t.shape if scatter else (N_DEV,) + t.shape, t.dtype) for t in arrays]
    hbm = pl.BlockSpec(memory_space=pltpu.HBM)
    return pl.pallas_call(
        body, out_shape=out_shape, in_specs=[hbm] * n, out_specs=[hbm] * n,
        scratch_shapes=[pltpu.SemaphoreType.DMA((n, N_DEV - 1)), pltpu.SemaphoreType.DMA((n, N_DEV - 1)),
                        pltpu.SemaphoreType.DMA((n,))], name=name)(*arrays)


ADAM_ROWS = 128


def sum_adamw(name, parts, row_off, w, m, v, tr):
    rows, wl = w.shape
    wp = parts.shape[2]
    tr = min(tr, rows)
    assert rows % tr == 0 and row_off % tr == 0, (name, rows, row_off, tr)
    off = row_off // tr

    def body(p_ref, w_ref, m_ref, v_ref, g_out, d_out, m_out, v_out):
        g = p_ref[0].astype(F32)
        for q in range(1, N_DEV):
            g = g + p_ref[q].astype(F32)
        g = g[:, :wl]
        m_new = ADAM_B1 * m_ref[...] + (1.0 - ADAM_B1) * g
        v_new = ADAM_B2 * v_ref[...] + (1.0 - ADAM_B2) * jnp.square(g)
        m_hat = m_new / (1.0 - ADAM_B1 ** ADAM_STEP)
        v_hat = v_new / (1.0 - ADAM_B2 ** ADAM_STEP)
        g_out[...] = g
        d_out[...] = -ADAM_LR * (m_hat / (jnp.sqrt(v_hat) + ADAM_EPS) + ADAM_WD * w_ref[...])
        m_out[...] = m_new
        v_out[...] = v_new

    blk = pl.BlockSpec((tr, wl), lambda i: (i, 0))
    return pl.pallas_call(
        body, grid=(rows // tr,),
        in_specs=[pl.BlockSpec((N_DEV, tr, wp), lambda i: (0, i + off, 0)), blk, blk, blk],
        out_specs=[blk] * 4, out_shape=[SDS((rows, wl), F32)] * 4, name=name,
        compiler_params=_params(("arbitrary",)))(parts, w, m, v)


def kernel(x, mem, positions, norm_g, ffn_w_gate, ffn_w_up, ffn_w_down, mem_w_kv, a_w_in, a_q_norm, a_kv_norm, a_w_uq, a_w_ukv, a_w_out, b_w_in, b_w_out, c_w_in, c_conv_w, c_conv_b, c_ln_g, c_ln_b, c_w_out, d_w_in, d_conv_w, d_w_out, loss_target, m_norm_g, m_ffn_w_gate, m_ffn_w_up, m_ffn_w_down, m_mem_w_kv, m_a_w_in, m_a_q_norm, m_a_kv_norm, m_a_w_uq, m_a_w_ukv, m_a_w_out, m_b_w_in, m_b_w_out, m_c_w_in, m_c_conv_w, m_c_conv_b, m_c_ln_g, m_c_ln_b, m_c_w_out, m_d_w_in, m_d_conv_w, m_d_w_out, v_norm_g, v_ffn_w_gate, v_ffn_w_up, v_ffn_w_down, v_mem_w_kv, v_a_w_in, v_a_q_norm, v_a_kv_norm, v_a_w_uq, v_a_w_ukv, v_a_w_out, v_b_w_in, v_b_w_out, v_c_w_in, v_c_conv_w, v_c_conv_b, v_c_ln_g, v_c_ln_b, v_c_w_out, v_d_w_in, v_d_conv_w, v_d_w_out):
    a = dict(zip(IN_NAMES, (x, mem, positions, norm_g, ffn_w_gate, ffn_w_up, ffn_w_down, mem_w_kv, a_w_in, a_q_norm, a_kv_norm, a_w_uq, a_w_ukv, a_w_out, b_w_in, b_w_out, c_w_in, c_conv_w, c_conv_b, c_ln_g, c_ln_b, c_w_out, d_w_in, d_conv_w, d_w_out, loss_target, m_norm_g, m_ffn_w_gate, m_ffn_w_up, m_ffn_w_down, m_mem_w_kv, m_a_w_in, m_a_q_norm, m_a_kv_norm, m_a_w_uq, m_a_w_ukv, m_a_w_out, m_b_w_in, m_b_w_out, m_c_w_in, m_c_conv_w, m_c_conv_b, m_c_ln_g, m_c_ln_b, m_c_w_out, m_d_w_in, m_d_conv_w, m_d_w_out, v_norm_g, v_ffn_w_gate, v_ffn_w_up, v_ffn_w_down, v_mem_w_kv, v_a_w_in, v_a_q_norm, v_a_kv_norm, v_a_w_uq, v_a_w_ukv, v_a_w_out, v_b_w_in, v_b_w_out, v_c_w_in, v_c_conv_w, v_c_conv_b, v_c_ln_g, v_c_ln_b, v_c_w_out, v_d_w_in, v_d_conv_w, v_d_w_out)))
    shapes = {n: tuple(a[n].shape) for n in WEIGHTS}
    gathered = _exchange("gather_weights", wire_weights(a) + [pack_local([a[n] for n in SMALL], F32, 8)], False)
    W = unwire_weights(gathered[:-1])
    W.update(unpack_gathered(gathered[-1], SMALL, shapes))
    W['a_q_norm'], W['a_kv_norm'] = a['a_q_norm'], a['a_kv_norm']
    loss, grad_x, grads = local_step(a['x'][0], a['mem'][0], a['positions'][0], a['loss_target'][0], W)
    parts = _exchange("scatter_grads", wire_grads(grads) + [pack_grads(grads, FLAT, ADAM_ROWS)], True)
    res = {}

    def adam(n, p, row_off, rows, tr):
        view = [a[pre + n].reshape(rows, -1) for pre in ('', 'm_', 'v_')]
        res[n] = [t.reshape(shapes[n]) for t in sum_adamw("adamw_" + n, p, row_off, *view, tr)]

    n_ff = DEPTH * 2 * D_MODEL
    adam('ffn_w_gate', parts[0], 0, n_ff, 256)
    adam('ffn_w_up', parts[0], n_ff, n_ff, 256)
    n_dn = DEPTH * 2 * FF_SHARD
    adam('ffn_w_down', parts[1], 0, n_dn, 128)
    rows = D_MODEL // N_DEV
    for k, n in enumerate(ROW_SHARDED_1024):
        adam(n, parts[1], n_dn + k * rows, rows, 128)
    adam('mem_w_kv', parts[2], 0, DEPTH * rows, 128)
    adam('a_w_in', parts[3], 0, rows, 128)
    for k, n in enumerate(COL_SHARDED):
        adam(n, parts[4 + k], 0, shapes[n][1], 128)
    packed = [pack_local([a[pre + n] for n in FLAT], F32, ADAM_ROWS) for pre in ('', 'm_', 'v_')]
    flat = [unpack_local(r, FLAT, shapes) for r in sum_adamw("adamw_flat", parts[-1], 0, *packed, ADAM_ROWS)]
    for n in FLAT:
        res[n] = [f[n] for f in flat]
    loss = lax.psum(loss, MESH_AXES)
    return (loss, grad_x[None], *[res[n][k] for k in range(4) for n in WEIGHTS])
```

```python
import functools
import math

import jax
import jax.numpy as jnp
from jax import lax
from jax.experimental import pallas as pl
from jax.experimental.pallas import tpu as pltpu

BF, F32 = jnp.bfloat16, jnp.float32
SDS = jax.ShapeDtypeStruct

D_MODEL, DEPTH, D_FF = 1024, 4, 2816
HEAD_DIM, MEM_HEADS, MEM_WIDTH, N_MEM = 64, 4, 256, 256
MLA_HEADS, MLA_Q_LORA, MLA_KV_LORA, MLA_NOPE, MLA_ROPE, MLA_V = 12, 384, 256, 64, 32, 64
MLA_QK = MLA_NOPE + MLA_ROPE
ROPE_THETA = 10000.0
DIL_GROUPS = ((128, 1), (512, 4), (2048, 16))
DIL_HEADS, DIL_HALF = 8, 64
DIL_QKV = len(DIL_GROUPS) * 3 * DIL_HEADS * HEAD_DIM
ALIBI_MAX = 8.0
CONV_CH, CONV_WIDTH, SC_CH, SC_WIDTH = 768, 31, 768, 3
EPS, NEG = 1e-6, -1e30
ADAM_LR, ADAM_B1, ADAM_B2, ADAM_EPS, ADAM_WD, ADAM_STEP = 0.001, 0.9, 0.999, 1e-08, 0.01, 10

N_DEV = 8
MESH_AXES = ("x", "y", "c")
FF_SHARD = D_FF // N_DEV
FF_SLOT = 384
D_FFP = N_DEV * FF_SLOT
LOG2E = 1.4426950408889634
V7X_VMEM_LIMIT = 48 * 1024 * 1024
HALO = 16
PACK_W = 1024

WEIGHTS = ['norm_g', 'ffn_w_gate', 'ffn_w_up', 'ffn_w_down', 'mem_w_kv', 'a_w_in', 'a_q_norm', 'a_kv_norm',
           'a_w_uq', 'a_w_ukv', 'a_w_out', 'b_w_in', 'b_w_out', 'c_w_in', 'c_conv_w', 'c_conv_b', 'c_ln_g',
           'c_ln_b', 'c_w_out', 'd_w_in', 'd_conv_w', 'd_w_out']
SHARD_AXIS = dict(norm_g=2, ffn_w_gate=3, ffn_w_up=3, ffn_w_down=2, mem_w_kv=1, a_w_in=1, a_q_norm=None,
                  a_kv_norm=None, a_w_uq=2, a_w_ukv=2, a_w_out=1, b_w_in=2, b_w_out=2, c_w_in=2, c_conv_w=2,
                  c_conv_b=1, c_ln_g=1, c_ln_b=1, c_w_out=1, d_w_in=2, d_conv_w=2, d_w_out=1)
SMALL = ['norm_g', 'c_conv_w', 'c_conv_b', 'c_ln_g', 'c_ln_b', 'd_conv_w']
BIG = [n for n in WEIGHTS if n not in SMALL and SHARD_AXIS[n] is not None]
IN_NAMES = (['x', 'mem', 'positions'] + WEIGHTS + ['loss_target'] + ['m_' + n for n in WEIGHTS]
            + ['v_' + n for n in WEIGHTS])


def _params(sem):
    return pltpu.CompilerParams(dimension_semantics=sem, vmem_limit_bytes=V7X_VMEM_LIMIT)


def _dot(a, b, ca=1, cb=0):
    return lax.dot_general(a, b, (((ca,), (cb,)), ((), ())), preferred_element_type=F32)


def mm_call(name, accs, M, N, K, *, tm, tn, tk, epilogue, outs, tiles=(), cols=(), fulls=(), reds=()):
    tm, tn, tk = min(tm, M), min(tn, N), min(tk, K)
    assert M % tm == 0 and N % tn == 0 and K % tk == 0, (name, M, N, K, tm, tn, tk)
    gj, gi, gk = N // tn, M // tm, K // tk
    flat = [p for acc in accs for p in acc]
    in_specs, args = [], []
    for (a, b, ta, tb) in flat:
        in_specs.append(pl.BlockSpec((tk, tm), lambda j, i, k: (k, i)) if ta else pl.BlockSpec((tm, tk), lambda j, i, k: (i, k)))
        in_specs.append(pl.BlockSpec((tn, tk), lambda j, i, k: (j, k)) if tb else pl.BlockSpec((tk, tn), lambda j, i, k: (k, j)))
        args += [a, b]
    for t in tiles:
        in_specs.append(pl.BlockSpec((tm, tn), lambda j, i, k: (i, j)))
        args.append(t)
    for c in cols:
        in_specs.append(pl.BlockSpec((c.shape[0], tn), lambda j, i, k: (0, j)))
        args.append(c)
    for f in fulls:
        in_specs.append(pl.BlockSpec(f.shape, lambda j, i, k: (0,) * f.ndim))
        args.append(f)
    out_shape = [SDS((M, N), dt) for dt in outs] + [SDS((r, N), F32) for r in reds]
    out_specs = ([pl.BlockSpec((tm, tn), lambda j, i, k: (i, j)) for _ in outs]
                 + [pl.BlockSpec((r, tn), lambda j, i, k: (0, j)) for r in reds])
    n_in = len(args)
    n_acc = len(accs)

    def body(*refs):
        in_refs = refs[:n_in]
        out_refs = refs[n_in:n_in + len(outs)]
        red_refs = refs[n_in + len(outs):n_in + len(outs) + len(reds)]
        acc_refs = refs[n_in + len(outs) + len(reds):]
        i, k = pl.program_id(1), pl.program_id(2)
        sums, idx = [], 0
        for acc in accs:
            tot = None
            for (_, _, ta, tb) in acc:
                av = in_refs[2 * idx][...].astype(BF)
                bv = in_refs[2 * idx + 1][...].astype(BF)
                idx += 1
                d = _dot(av, bv, 0 if ta else 1, 1 if tb else 0)
                tot = d if tot is None else tot + d
            sums.append(tot)

        def finish(vals):
            p = 2 * len(flat)
            tv = [r[...] for r in in_refs[p:p + len(tiles)]]
            cv = [r[...] for r in in_refs[p + len(tiles):p + len(tiles) + len(cols)]]
            fv = [r[...] for r in in_refs[p + len(tiles) + len(cols):]]
            ov, rv = epilogue(vals, tv, cv, fv)
            for r, v in zip(out_refs, ov):
                r[...] = v.astype(r.dtype)
            for r, v in zip(red_refs, rv):
                @pl.when(i == 0)
                def _(r=r, v=v):
                    r[...] = v

                @pl.when(i > 0)
                def _(r=r, v=v):
                    r[...] += v

        if gk == 1:
            finish(sums)
        else:
            @pl.when(k == 0)
            def _():
                for r, v in zip(acc_refs, sums):
                    r[...] = v

            @pl.when(k > 0)
            def _():
                for r, v in zip(acc_refs, sums):
                    r[...] += v

            @pl.when(k == gk - 1)
            def _():
                finish([r[...] for r in acc_refs])

    scratch = [] if gk == 1 else [pltpu.VMEM((tm, tn), F32) for _ in range(n_acc)]
    res = pl.pallas_call(body, grid=(gj, gi, gk), in_specs=in_specs, out_specs=out_specs, out_shape=out_shape,
                         scratch_shapes=scratch, name=name,
                         compiler_params=_params(("arbitrary", "arbitrary", "arbitrary")))(*args)
    return res


def _plain(dt):
    return dict(epilogue=lambda a, t, c, f: ([a[0]], []), outs=[dt])


def rows_call(name, fn, S, ts, tiled, fulls, outs, reds=()):
    ts = min(ts, S)
    assert S % ts == 0, (name, S, ts)
    in_specs = ([pl.BlockSpec((ts, t.shape[1]), lambda i: (i, 0)) for t in tiled]
                + [pl.BlockSpec(f.shape, lambda i, n=f.ndim: (0,) * n) for f in fulls])
    out_shape = [SDS((S, w), dt) for (w, dt) in outs] + [SDS(rw, F32) for rw in reds]
    out_specs = ([pl.BlockSpec((ts, w), lambda i: (i, 0)) for (w, _) in outs]
                 + [pl.BlockSpec(rw, lambda i: (0, 0)) for rw in reds])
    nt, nf, no = len(tiled), len(fulls), len(outs)

    def body(*refs):
        i = pl.program_id(0)
        tv = [r[...] for r in refs[:nt]]
        fv = [r[...] for r in refs[nt:nt + nf]]
        ov, rv = fn(i, tv, fv)
        for r, v in zip(refs[nt + nf:nt + nf + no], ov):
            r[...] = v.astype(r.dtype)
        for r, v in zip(refs[nt + nf + no:], rv):
            @pl.when(i == 0)
            def _(r=r, v=v):
                r[...] = v

            @pl.when(i > 0)
            def _(r=r, v=v):
                r[...] += v

    return pl.pallas_call(body, grid=(S // ts,), in_specs=in_specs, out_specs=out_specs, out_shape=out_shape,
                          name=name, compiler_params=_params(("arbitrary",)))(*tiled, *fulls)


def hrows_call(name, fn, H, S, ts, tiled, outs):
    ts = min(ts, S)
    in_specs = [pl.BlockSpec((1, ts, t.shape[2]), lambda h, i: (h, i, 0)) for t in tiled]
    out_shape = [SDS((H, S, w), dt) for (w, dt) in outs]
    out_specs = [pl.BlockSpec((1, ts, w), lambda h, i: (h, i, 0)) for (w, _) in outs]
    nt = len(tiled)

    def body(*refs):
        ov = fn([r[0] for r in refs[:nt]])
        for r, v in zip(refs[nt:], ov):
            r[0] = v.astype(r.dtype)

    return pl.pallas_call(body, grid=(H, S // ts), in_specs=in_specs, out_specs=out_specs, out_shape=out_shape,
                          name=name, compiler_params=_params(("arbitrary", "arbitrary")))(*tiled)


def _rms(x, g):
    r = lax.rsqrt(jnp.mean(x * x, axis=-1, keepdims=True) + EPS)
    return x * r * g


def _rms_bwd(x, g, dy):
    r = lax.rsqrt(jnp.mean(x * x, axis=-1, keepdims=True) + EPS)
    xh = x * r
    dg = jnp.sum(dy * xh, axis=0, keepdims=True)
    dxh = dy * g
    dx = r * (dxh - xh * jnp.mean(dxh * xh, axis=-1, keepdims=True))
    return dx, dg


def _silu(x):
    return x * jax.nn.sigmoid(x)


def _dsilu(x):
    s = jax.nn.sigmoid(x)
    return s * (1.0 + x * (1.0 - s))


def rms_rows(name, x, g):
    S, w = x.shape
    return rows_call(name, lambda i, t, f: ([_rms(t[0].astype(F32), f[0])], []), S, 512, [x], [g], [(w, BF)])[0]


def ffn_fwd(tag, x0, g_in, g_out, wg, wu, wd):
    S = x0.shape[0]
    h = rms_rows(tag + "_rms", x0, g_in)

    def ep_up(a, t, c, f):
        return [a[0], a[1], _silu(a[0]) * a[1]], []

    gate, up, act = mm_call(tag + "_up", [[(h, wg, False, False)], [(h, wu, False, False)]], S, D_FFP, D_MODEL,
                            tm=1024, tn=512, tk=1024, epilogue=ep_up, outs=[BF, BF, BF])

    def ep_down(a, t, c, f):
        return [a[0], t[0] + 0.5 * _rms(a[0], c[0])], []

    y, x1 = mm_call(tag + "_down", [[(act, wd, False, False)]], S, D_MODEL, D_FFP, tm=512, tn=1024, tk=1024,
                    epilogue=ep_down, outs=[BF, F32], tiles=[x0], cols=[g_out])
    return x1, dict(x0=x0, h=h, gate=gate, up=up, act=act, y=y)


def ffn_bwd(tag, dx1, sv, g_in, g_out, wg, wu, wd):
    S = dx1.shape[0]

    def post(i, t, f):
        dx, dg = _rms_bwd(t[1].astype(F32), f[0], 0.5 * t[0])
        return [dx], [dg]

    da, dg_out = rows_call(tag + "_bpost", post, S, 512, [dx1, sv['y']], [g_out], [(D_MODEL, BF)], [(1, D_MODEL)])

    def ep_act(a, t, c, f):
        g, u = t[0].astype(F32), t[1].astype(F32)
        return [a[0] * u * _dsilu(g), a[0] * _silu(g)], []

    dgate, dup = mm_call(tag + "_bact", [[(da, wd, False, True)]], S, D_FFP, D_MODEL, tm=1024, tn=512, tk=1024,
                         epilogue=ep_act, outs=[BF, BF], tiles=[sv['gate'], sv['up']])
    dwd, = mm_call(tag + "_dwd", [[(sv['act'], da, True, False)]], D_FFP, D_MODEL, S, tm=1024, tn=1024, tk=512,
                   **_plain(F32))
    dwg, dwu = mm_call(tag + "_dwgu", [[(sv['h'], dgate, True, False)], [(sv['h'], dup, True, False)]],
                       D_MODEL, D_FFP, S, tm=1024, tn=512, tk=1024,
                       epilogue=lambda a, t, c, f: ([a[0], a[1]], []), outs=[F32, F32])

    def ep_in(a, t, c, f):
        dx, dg = _rms_bwd(t[0], c[0], a[0])
        return [t[1] + dx], [dg]

    (dx0, dg_in) = mm_call(tag + "_bin", [[(dgate, wg, False, True), (dup, wu, False, True)]], S, D_MODEL, D_FFP,
                           tm=512, tn=1024, tk=1024, epilogue=ep_in, outs=[F32], tiles=[sv['x0'], dx1], cols=[g_in],
                           reds=[1])
    return dx0, dict(g_in=dg_in, g_out=dg_out, wg=dwg, wu=dwu, wd=dwd)


def mem_kv_fwd(tag, mem, g6, w_kv):
    def body(mem_ref, g_ref, w_ref, kv_ref):
        kv_ref[...] = _dot(_rms(mem_ref[...], g_ref[...]).astype(BF), w_ref[...])

    return pl.pallas_call(body, out_shape=SDS((N_MEM, 2 * MEM_WIDTH), F32), name=tag + "_memkv",
                          compiler_params=pltpu.CompilerParams(vmem_limit_bytes=V7X_VMEM_LIMIT))(mem, g6, w_kv)


def mem_kv_bwd(tag, mem, g6, w_kv, dkv):
    def body(mem_ref, g_ref, w_ref, dkv_ref, dw_ref, dg_ref):
        x = mem_ref[...]
        r = lax.rsqrt(jnp.mean(x * x, axis=-1, keepdims=True) + EPS)
        xh = x * r
        d = dkv_ref[...].astype(BF)
        dw_ref[...] = _dot((xh * g_ref[...]).astype(BF), d, 0, 0)
        dmem_n = _dot(d, w_ref[...], 1, 1)
        dg_ref[...] = jnp.sum(dmem_n * xh, axis=0, keepdims=True)

    return pl.pallas_call(body, out_shape=(SDS((D_MODEL, 2 * MEM_WIDTH), F32), SDS((1, D_MODEL), F32)),
                          name=tag + "_memkvb",
                          compiler_params=pltpu.CompilerParams(vmem_limit_bytes=V7X_VMEM_LIMIT))(mem, g6, w_kv, dkv)


def _block_diag_kv(kv):
    k = kv[:, :MEM_WIDTH].reshape(N_MEM, MEM_HEADS, HEAD_DIM)
    v = kv[:, MEM_WIDTH:].reshape(N_MEM, MEM_HEADS, HEAD_DIM)
    eye = jnp.eye(MEM_HEADS, dtype=kv.dtype)
    kbd = jnp.einsum('nhd,hg->hdgn', k, eye).reshape(MEM_WIDTH, MEM_HEADS * N_MEM)
    vbd = jnp.einsum('nhd,hg->hngd', v, eye).reshape(MEM_HEADS * N_MEM, MEM_WIDTH)
    return kbd.astype(BF), vbd.astype(BF)


def _mem_probs(q, kbd):
    s = _dot(q, kbd) * (HEAD_DIM ** -0.5)
    ps = []
    for h in range(MEM_HEADS):
        sh = s[:, h * N_MEM:(h + 1) * N_MEM]
        e = jnp.exp(sh - jnp.max(sh, axis=-1, keepdims=True))
        ps.append(e / jnp.sum(e, axis=-1, keepdims=True))
    return ps


def mem_attn_fwd(tag, q_mem, kbd, vbd):
    S = q_mem.shape[0]

    def fn(i, t, f):
        p = jnp.concatenate(_mem_probs(t[0], f[0]), axis=1).astype(BF)
        return [_dot(p, f[1])], []

    return rows_call(tag + "_mem", fn, S, 512, [q_mem], [kbd, vbd], [(MEM_WIDTH, BF)])[0]


def mem_attn_bwd(tag, q_mem, dmo, kbd, vbd):
    S = q_mem.shape[0]

    def fn(i, t, f):
        q, do = t
        ps = _mem_probs(q, f[0])
        dp = _dot(do, f[1], 1, 1)
        dss = []
        for h in range(MEM_HEADS):
            dph = dp[:, h * N_MEM:(h + 1) * N_MEM]
            dss.append(ps[h] * (dph - jnp.sum(dph * ps[h], axis=-1, keepdims=True)) * (HEAD_DIM ** -0.5))
        ds = jnp.concatenate(dss, axis=1).astype(BF)
        p = jnp.concatenate(ps, axis=1).astype(BF)
        return [_dot(ds, f[0], 1, 1)], [_dot(q, ds, 0, 0), _dot(p, do, 0, 0)]

    dq, dkbd, dvbd = rows_call(tag + "_memb", fn, S, 512, [q_mem, dmo], [kbd, vbd], [(MEM_WIDTH, BF)],
                               [(MEM_WIDTH, MEM_HEADS * N_MEM), (MEM_HEADS * N_MEM, MEM_WIDTH)])
    dk = jnp.einsum('hdhn->nhd', dkbd.reshape(MEM_HEADS, HEAD_DIM, MEM_HEADS, N_MEM)).reshape(N_MEM, MEM_WIDTH)
    dv = jnp.einsum('hnhd->nhd', dvbd.reshape(MEM_HEADS, N_MEM, MEM_HEADS, HEAD_DIM)).reshape(N_MEM, MEM_WIDTH)
    return dq, jnp.concatenate([dk, dv], axis=1)


def _rope_tables(positions):
    half = MLA_ROPE // 2
    inv = ROPE_THETA ** (-jnp.arange(half, dtype=F32) / half)
    ang = positions.astype(F32)[:, None] * inv
    cos, sin = jnp.cos(ang), jnp.sin(ang)
    cos32, sin32 = jnp.concatenate([cos, cos], 1), jnp.concatenate([sin, sin], 1)
    idx = jnp.arange(MLA_ROPE)
    p32 = (jnp.where((idx[:, None] == idx[None, :] + half), -1.0, 0.0)
           + jnp.where((idx[:, None] + half == idx[None, :]), 1.0, 0.0)).astype(F32)
    S = positions.shape[0]
    ones = jnp.ones((S, MLA_NOPE), F32)
    cq = jnp.tile(jnp.concatenate([ones, cos32], 1), (1, MLA_HEADS))
    sq = jnp.tile(jnp.concatenate([0.0 * ones, sin32], 1), (1, MLA_HEADS))
    pq = jnp.zeros((MLA_QK, MLA_QK), F32).at[MLA_NOPE:, MLA_NOPE:].set(p32)
    pq = jnp.kron(jnp.eye(MLA_HEADS, dtype=F32), pq)
    return dict(cos32=cos32, sin32=sin32, p32=p32.astype(BF), p32t=p32.T.astype(BF), cq=cq, sq=sq,
                pq=pq.astype(BF), pqt=pq.T.astype(BF))


def flash_fwd(tag, qh, kh, vh1, dv, tq=1024, tk=2048):
    H, S, dq = qh.shape
    dv1 = vh1.shape[2]
    tq, tk = min(tq, S), min(tk, S)
    nk = S // tk

    def body(q_ref, k_ref, v_ref, o_ref, lse_ref, m_s, acc_s):
        j = pl.program_id(2)

        @pl.when(j == 0)
        def _():
            m_s[...] = jnp.full(m_s.shape, NEG, F32)
            acc_s[...] = jnp.zeros(acc_s.shape, F32)

        s = _dot(q_ref[0], k_ref[0], 1, 1)
        m_old = m_s[...]
        m_new = jnp.maximum(m_old, jnp.max(s, axis=-1, keepdims=True))
        p = jnp.exp2((s - m_new).astype(BF))
        acc_s[...] = jnp.exp2(m_old - m_new) * acc_s[...] + _dot(p, v_ref[0])
        m_s[...] = m_new

        @pl.when(j == nk - 1)
        def _():
            acc = acc_s[...]
            l = acc[:, dv:dv + 1]
            o_ref[0] = (acc[:, :dv] / l).astype(o_ref.dtype)
            lse_ref[0] = m_s[...] + jnp.log2(l)

    return pl.pallas_call(
        body, grid=(H, S // tq, nk),
        in_specs=[pl.BlockSpec((1, tq, dq), lambda h, i, j: (h, i, 0)),
                  pl.BlockSpec((1, tk, dq), lambda h, i, j: (h, j, 0)),
                  pl.BlockSpec((1, tk, dv1), lambda h, i, j: (h, j, 0))],
        out_specs=[pl.BlockSpec((1, tq, dv), lambda h, i, j: (h, i, 0)),
                   pl.BlockSpec((1, tq, 1), lambda h, i, j: (h, i, 0))],
        out_shape=[SDS((H, S, dv), BF), SDS((H, S, 1), F32)],
        scratch_shapes=[pltpu.VMEM((tq, 1), F32), pltpu.VMEM((tq, dv1), F32)],
        name=tag + "_flash", compiler_params=_params(("arbitrary", "arbitrary", "arbitrary")))(qh, kh, vh1)


def flash_bwd(tag, qh, kh, vh, doh, lse_row, delta_row, tq=1024, tk=1024):
    H, S, dq = qh.shape
    dv = vh.shape[2]
    tq, tk = min(tq, S), min(tk, S)

    def body(k_ref, v_ref, q_ref, do_ref, lse_ref, dl_ref, dk_ref, dv_ref, dq_ref):
        j, i = pl.program_id(1), pl.program_id(2)
        k, v, q, do = k_ref[0], v_ref[0], q_ref[0], do_ref[0]
        pt = jnp.exp2((_dot(k, q, 1, 1) - lse_ref[0]).astype(BF))
        dst = pt * (_dot(v, do, 1, 1) - dl_ref[0]).astype(BF)
        dvc = _dot(pt, do)
        dkc = _dot(dst, q) * (1.0 / LOG2E)
        dqc = _dot(dst, k, 0, 0)

        @pl.when(i == 0)
        def _():
            dk_ref[0] = dkc
            dv_ref[0] = dvc

        @pl.when(i > 0)
        def _():
            dk_ref[0] += dkc
            dv_ref[0] += dvc

        rows = pl.ds(pl.multiple_of(i * tq, tq), tq)

        @pl.when(j == 0)
        def _():
            dq_ref[0, rows, :] = dqc

        @pl.when(j > 0)
        def _():
            dq_ref[0, rows, :] += dqc

    return pl.pallas_call(
        body, grid=(H, S // tk, S // tq),
        in_specs=[pl.BlockSpec((1, tk, dq), lambda h, j, i: (h, j, 0)),
                  pl.BlockSpec((1, tk, dv), lambda h, j, i: (h, j, 0)),
                  pl.BlockSpec((1, tq, dq), lambda h, j, i: (h, i, 0)),
                  pl.BlockSpec((1, tq, dv), lambda h, j, i: (h, i, 0)),
                  pl.BlockSpec((1, 1, tq), lambda h, j, i: (h, 0, i)),
                  pl.BlockSpec((1, 1, tq), lambda h, j, i: (h, 0, i))],
        out_specs=[pl.BlockSpec((1, tk, dq), lambda h, j, i: (h, j, 0)),
                   pl.BlockSpec((1, tk, dv), lambda h, j, i: (h, j, 0)),
                   pl.BlockSpec((1, S, dq), lambda h, j, i: (h, 0, 0))],
        out_shape=[SDS((H, S, dq), F32), SDS((H, S, dv), F32), SDS((H, S, dq), F32)],
        name=tag + "_flashb",
        compiler_params=_params(("arbitrary", "arbitrary", "arbitrary")))(kh, vh, qh, doh, lse_row, delta_row)


def _to_heads(t, H):
    return t.reshape(t.shape[0], H, -1).transpose(1, 0, 2)


def _from_heads(t):
    return t.transpose(1, 0, 2).reshape(t.shape[1], -1)


def mla_fwd(tag, z, rope, q_g, kv_g, w_uq, w_ukv):
    S = z.shape[0]
    o1, o2, o3 = MLA_Q_LORA, MLA_Q_LORA + MLA_KV_LORA, MLA_Q_LORA + MLA_KV_LORA + MLA_ROPE
    c_q, c_kv, k_r, q_mem = z[:, :o1], z[:, o1:o2], z[:, o2:o3], z[:, o3:]

    def pre(i, t, f):
        kr = t[2].astype(F32)
        krot = kr * t[3] + _dot(t[2], f[2]) * t[4]
        return [_rms(t[0].astype(F32), f[0]), _rms(t[1].astype(F32), f[1]), krot], []

    qn, kvn, krot = rows_call(tag + "_pre", pre, S, 512, [c_q, c_kv, k_r, rope['cos32'], rope['sin32']],
                              [q_g, kv_g, rope['p32']], [(MLA_Q_LORA, BF), (MLA_KV_LORA, BF), (MLA_ROPE, BF)])

    qscale = MLA_QK ** -0.5 * LOG2E

    def ep_q(a, t, c, f):
        return [(a[0] * t[0] + _dot(a[0].astype(BF), f[0]) * t[1]) * qscale], []

    nq = MLA_HEADS * MLA_QK
    q, = mm_call(tag + "_uq", [[(qn, w_uq, False, False)]], S, nq, MLA_Q_LORA, tm=512, tn=nq, tk=MLA_Q_LORA,
                 epilogue=ep_q, outs=[BF], tiles=[rope['cq'], rope['sq']], fulls=[rope['pq']])
    nkv = MLA_HEADS * (MLA_NOPE + MLA_V)
    kv, = mm_call(tag + "_ukv", [[(kvn, w_ukv, False, False)]], S, nkv, MLA_KV_LORA, tm=1024, tn=nkv, tk=MLA_KV_LORA,
                  **_plain(BF))
    qh = _to_heads(q, MLA_HEADS)
    kv3 = kv.reshape(S, MLA_HEADS, MLA_NOPE + MLA_V)
    kh = jnp.concatenate([kv3[:, :, :MLA_NOPE].transpose(1, 0, 2),
                          jnp.broadcast_to(krot[None], (MLA_HEADS, S, MLA_ROPE))], axis=2)
    vh = kv3[:, :, MLA_NOPE:].transpose(1, 0, 2)
    ones = jnp.ones((MLA_HEADS, S, 1), BF)
    vh1 = jnp.concatenate([vh, ones, jnp.zeros((MLA_HEADS, S, 128 - MLA_V - 1), BF)], axis=2)
    oh, lse = flash_fwd(tag, qh, kh, vh1, MLA_V)
    sv = dict(c_q=c_q, c_kv=c_kv, k_r=k_r, qn=qn, kvn=kvn, qh=qh, kh=kh, vh=vh, oh=oh, lse=lse)
    return _from_heads(oh), q_mem, sv


def mla_bwd(tag, do, sv, rope, q_g, kv_g, w_uq, w_ukv):
    S = do.shape[0]
    H = MLA_HEADS
    doh = _to_heads(do, H)
    delta, = hrows_call(tag + "_delta",
                        lambda t: [jnp.sum(t[0].astype(F32) * t[1].astype(F32), axis=-1, keepdims=True)],
                        H, S, 1024, [doh, sv['oh']], [(1, F32)])
    dkh, dvh, dqh = flash_bwd(tag, sv['qh'], sv['kh'], sv['vh'], doh, sv['lse'].reshape(H, 1, S),
                              delta.reshape(H, 1, S))
    nq = H * MLA_QK
    dq_rot = _from_heads(dqh)
    scale = MLA_QK ** -0.5

    def unrope(i, t, f):
        d = t[0] * scale
        return [d * t[1] + _dot((d * t[2]).astype(BF), f[0])], []

    dq, = rows_call(tag + "_unrope", unrope, S, 512, [dq_rot, rope['cq'], rope['sq']], [rope['pqt']], [(nq, BF)])

    def ep_norm(a, t, c, f):
        dx, dg = _rms_bwd(t[0].astype(F32), c[0], a[0])
        return [dx], [dg]

    dc_q, dq_g = mm_call(tag + "_bqn", [[(dq, w_uq, False, True)]], S, MLA_Q_LORA, nq, tm=1024, tn=MLA_Q_LORA, tk=nq,
                         epilogue=ep_norm, outs=[BF], tiles=[sv['c_q']], cols=[q_g], reds=[1])
    dw_uq, = mm_call(tag + "_dwuq", [[(sv['qn'], dq, True, False)]], MLA_Q_LORA, nq, S, tm=MLA_Q_LORA, tn=nq, tk=1024,
                     **_plain(F32))
    nkv = H * (MLA_NOPE + MLA_V)
    dkv = jnp.concatenate([dkh[:, :, :MLA_NOPE], dvh], axis=2).transpose(1, 0, 2).reshape(S, nkv).astype(BF)
    dc_kv, dkv_g = mm_call(tag + "_bkvn", [[(dkv, w_ukv, False, True)]], S, MLA_KV_LORA, nkv, tm=1024, tn=MLA_KV_LORA,
                           tk=nkv, epilogue=ep_norm, outs=[BF], tiles=[sv['c_kv']], cols=[kv_g], reds=[1])
    dw_ukv, = mm_call(tag + "_dwukv", [[(sv['kvn'], dkv, True, False)]], MLA_KV_LORA, nkv, S, tm=MLA_KV_LORA, tn=nkv,
                      tk=1024, **_plain(F32))
    dkr_heads = dkh[:, :, MLA_NOPE:].transpose(1, 0, 2).reshape(S, H * MLA_ROPE)
    hsum = jnp.tile(jnp.eye(MLA_ROPE, dtype=F32), (H, 1))

    def krope(i, t, f):
        d = jnp.dot(t[0], f[0], preferred_element_type=F32, precision=lax.Precision.HIGHEST)
        return [d * t[1] + _dot((d * t[2]).astype(BF), f[1])], []

    dk_r, = rows_call(tag + "_bkr", krope, S, 512, [dkr_heads, rope['cos32'], rope['sin32']], [hsum, rope['p32t']],
                      [(MLA_ROPE, BF)])
    grads = dict(a_q_norm=dq_g, a_kv_norm=dkv_g, a_w_uq=dw_uq, a_w_ukv=dw_ukv)
    return [dc_q, dc_kv, dk_r], grads


def _dil_perm(t, dil):
    H, S, d = t.shape
    return t.reshape(H, S // dil, dil, d).transpose(0, 2, 1, 3).reshape(H, S, d)


def _dil_unperm(t, dil):
    H, S, d = t.shape
    return t.reshape(H, dil, S // dil, d).transpose(0, 2, 1, 3).reshape(H, S, d)


def _band(rpos, cpos, S, seg_shift):
    dist = jnp.abs(rpos - cpos)
    valid = ((dist <= DIL_HALF) & (cpos >= 0) & (cpos < S) & (rpos >= 0) & (rpos < S)
             & (jnp.right_shift(rpos, seg_shift) == jnp.right_shift(cpos, seg_shift)))
    return dist, valid


def _halo_specs(blk, halo, n_halo_blocks, w, nh):
    r = blk // halo
    return [pl.BlockSpec((nh, halo, w), lambda h, i: (h, jnp.maximum(i * r - 1, 0), 0)),
            pl.BlockSpec((nh, blk, w), lambda h, i: (h, i, 0)),
            pl.BlockSpec((nh, halo, w), lambda h, i: (h, jnp.minimum((i + 1) * r, n_halo_blocks - 1), 0))]


def dil_fwd(tag, q, k, v, slopes, dil, tq=256):
    H, S, dh = q.shape
    nh = H
    tq = min(tq, S)
    seg_shift = int(math.log2(S // dil))
    hb = DIL_HALF
    scale = dh ** -0.5

    def body(sl_ref, q_ref, kp, kc, kn, vp, vc, vn, o_ref, lse_ref):
        i = pl.program_id(1)
        shp = (tq, tq + 2 * hb)
        qpos = i * tq + lax.broadcasted_iota(jnp.int32, shp, 0)
        kpos = i * tq - hb + lax.broadcasted_iota(jnp.int32, shp, 1)
        dist, valid = _band(qpos, kpos, S, seg_shift)
        tokens = (dist * dil).astype(F32)
        for h in range(nh):
            kcat = jnp.concatenate([kp[h], kc[h], kn[h]], axis=0)
            vcat = jnp.concatenate([vp[h], vc[h], vn[h]], axis=0)
            s = _dot(q_ref[h], kcat, 1, 1) * scale - sl_ref[h][:, :1] * tokens
            s = jnp.where(valid, s, NEG)
            m = jnp.max(s, axis=-1, keepdims=True)
            lse = m + jnp.log(jnp.sum(jnp.exp(s - m), axis=-1, keepdims=True))
            o_ref[h] = _dot(jnp.exp(s - lse).astype(BF), vcat)
            lse_ref[h] = lse

    return pl.pallas_call(
        body, grid=(H // nh, S // tq),
        in_specs=([pl.BlockSpec((nh, 1, 128), lambda h, i: (h, 0, 0)), pl.BlockSpec((nh, tq, dh), lambda h, i: (h, i, 0))]
                  + _halo_specs(tq, hb, S // hb, dh, nh) * 2),
        out_specs=[pl.BlockSpec((nh, tq, dh), lambda h, i: (h, i, 0)), pl.BlockSpec((nh, tq, 1), lambda h, i: (h, i, 0))],
        out_shape=[SDS((H, S, dh), F32), SDS((H, S, 1), F32)],
        name=tag + "_dil", compiler_params=_params(("arbitrary", "arbitrary")))(slopes, q, k, k, k, v, v, v)


def dil_bwd_q(tag, q, k, v, do, lse, deff, slopes, dil, tq=256):
    H, S, dh = q.shape
    nh = H
    tq = min(tq, S)
    seg_shift = int(math.log2(S // dil))
    hb = DIL_HALF
    scale = dh ** -0.5

    def body(sl_ref, q_ref, do_ref, lse_ref, de_ref, kp, kc, kn, vp, vc, vn, dq_ref):
        i = pl.program_id(1)
        shp = (tq, tq + 2 * hb)
        qpos = i * tq + lax.broadcasted_iota(jnp.int32, shp, 0)
        kpos = i * tq - hb + lax.broadcasted_iota(jnp.int32, shp, 1)
        dist, valid = _band(qpos, kpos, S, seg_shift)
        tokens = (dist * dil).astype(F32)
        for h in range(nh):
            kcat = jnp.concatenate([kp[h], kc[h], kn[h]], axis=0)
            vcat = jnp.concatenate([vp[h], vc[h], vn[h]], axis=0)
            s = _dot(q_ref[h], kcat, 1, 1) * scale - sl_ref[h][:, :1] * tokens
            p = jnp.where(valid, jnp.exp(jnp.where(valid, s, NEG) - lse_ref[h]), 0.0)
            dp = _dot(do_ref[h], vcat, 1, 1)
            ds = (p * (dp - de_ref[h]) * scale).astype(BF)
            dq_ref[h] = _dot(ds, kcat)

    row = lambda w: pl.BlockSpec((nh, tq, w), lambda h, i: (h, i, 0))
    return pl.pallas_call(
        body, grid=(H // nh, S // tq),
        in_specs=([pl.BlockSpec((nh, 1, 128), lambda h, i: (h, 0, 0)), row(dh), row(dh), row(1), row(1)]
                  + _halo_specs(tq, hb, S // hb, dh, nh) * 2),
        out_specs=row(dh), out_shape=SDS((H, S, dh), F32),
        name=tag + "_dilbq",
        compiler_params=_params(("arbitrary", "arbitrary")))(slopes, q, do, lse, deff, k, k, k, v, v, v)


def dil_bwd_kv(tag, q, k, v, do, lse_row, deff_row, slopes, dil, tk=256):
    H, S, dh = q.shape
    nh = H
    tk = min(tk, S)
    seg_shift = int(math.log2(S // dil))
    hq = 128
    scale = dh ** -0.5
    r = tk // hq
    nhb = S // hq
    stat_specs = [pl.BlockSpec((nh, 1, hq), lambda h, j: (h, 0, jnp.maximum(j * r - 1, 0))),
                  pl.BlockSpec((nh, 1, tk), lambda h, j: (h, 0, j)),
                  pl.BlockSpec((nh, 1, hq), lambda h, j: (h, 0, jnp.minimum((j + 1) * r, nhb - 1)))]

    def body(sl_ref, k_ref, v_ref, qp, qc, qn, dop, doc, don, lp, lc, ln, ep, ec, en, dk_ref, dv_ref):
        j = pl.program_id(1)
        shp = (tk, tk + 2 * hq)
        kpos = j * tk + lax.broadcasted_iota(jnp.int32, shp, 0)
        qpos = j * tk - hq + lax.broadcasted_iota(jnp.int32, shp, 1)
        dist, valid = _band(kpos, qpos, S, seg_shift)
        tokens = (dist * dil).astype(F32)
        for h in range(nh):
            qcat = jnp.concatenate([qp[h], qc[h], qn[h]], axis=0)
            docat = jnp.concatenate([dop[h], doc[h], don[h]], axis=0)
            lse = jnp.concatenate([lp[h], lc[h], ln[h]], axis=1)
            deff = jnp.concatenate([ep[h], ec[h], en[h]], axis=1)
            st = _dot(k_ref[h], qcat, 1, 1) * scale - sl_ref[h][:, :1] * tokens
            pt = jnp.where(valid, jnp.exp(jnp.where(valid, st, NEG) - jnp.where(valid, lse, 0.0)), 0.0)
            dpt = _dot(v_ref[h], docat, 1, 1)
            dst = (pt * (dpt - deff) * scale).astype(BF)
            dv_ref[h] = _dot(pt.astype(BF), docat)
            dk_ref[h] = _dot(dst, qcat)

    cur = lambda w: pl.BlockSpec((nh, tk, w), lambda h, j: (h, j, 0))
    return pl.pallas_call(
        body, grid=(H // nh, S // tk),
        in_specs=([pl.BlockSpec((nh, 1, 128), lambda h, j: (h, 0, 0)), cur(dh), cur(dh)]
                  + _halo_specs(tk, hq, nhb, dh, nh) * 2 + stat_specs * 2),
        out_specs=[cur(dh), cur(dh)], out_shape=[SDS((H, S, dh), F32), SDS((H, S, dh), F32)],
        name=tag + "_dilbkv", compiler_params=_params(("arbitrary", "arbitrary")))(
            slopes, k, v, q, q, q, do, do, do, lse_row, lse_row, lse_row, deff_row, deff_row, deff_row)


def _alibi_slopes():
    n = len(DIL_GROUPS) * DIL_HEADS
    s = 2.0 ** (-ALIBI_MAX * (jnp.arange(n, dtype=F32) + 1.0) / n)
    return jnp.broadcast_to(s.reshape(len(DIL_GROUPS), DIL_HEADS, 1, 1), (len(DIL_GROUPS), DIL_HEADS, 1, 128))


def _merge_weights(lses):
    m = jnp.maximum(jnp.maximum(lses[0], lses[1]), lses[2])
    es = [jnp.exp(l - m) for l in lses]
    tot = es[0] + es[1] + es[2]
    return [e / tot for e in es]


def dilated_fwd(tag, z):
    S = z.shape[0]
    H = DIL_HEADS
    ng = len(DIL_GROUPS)
    zd = z[:, :DIL_QKV].reshape(S, ng, 3, H, HEAD_DIM)
    q_mem = z[:, DIL_QKV:]
    slopes = _alibi_slopes()
    qkv, outs, lses = [], [], []
    for g, (_, dil) in enumerate(DIL_GROUPS):
        t = [_dil_perm(zd[:, g, c].transpose(1, 0, 2), dil) for c in range(3)]
        o, lse = dil_fwd(f"{tag}_g{g}", t[0], t[1], t[2], slopes[g], dil)
        qkv.append(t)
        outs.append(_dil_unperm(o, dil))
        lses.append(_dil_unperm(lse, dil))

    def merge(t):
        w = _merge_weights(t[3:6])
        return [w[0] * t[0] + w[1] * t[1] + w[2] * t[2]]

    oh, = hrows_call(tag + "_merge", merge, H, S, 1024, outs + lses, [(HEAD_DIM, BF)])
    return _from_heads(oh), q_mem, dict(qkv=qkv, outs=outs, lses=lses)


def dilated_bwd(tag, do, sv):
    S = do.shape[0]
    H = DIL_HEADS
    slopes = _alibi_slopes()
    doh = _to_heads(do, H)

    def merge_b(t):
        d = t[0].astype(F32)
        w = _merge_weights(t[4:7])
        dws = [jnp.sum(d * t[1 + g], axis=-1, keepdims=True) for g in range(3)]
        c = w[0] * dws[0] + w[1] * dws[1] + w[2] * dws[2]
        return [w[g] * d for g in range(3)] + [w[g] * c for g in range(3)]

    res = hrows_call(tag + "_mergeb", merge_b, H, S, 1024, [doh] + sv['outs'] + sv['lses'],
                     [(HEAD_DIM, BF)] * 3 + [(1, F32)] * 3)
    parts = []
    for g, (_, dil) in enumerate(DIL_GROUPS):
        q, k, v = sv['qkv'][g]
        do_g = _dil_perm(res[g], dil)
        deff = _dil_perm(res[3 + g], dil)
        lse = _dil_perm(sv['lses'][g], dil)
        dq = dil_bwd_q(f"{tag}_g{g}", q, k, v, do_g, lse, deff, slopes[g], dil)
        dk, dv = dil_bwd_kv(f"{tag}_g{g}", q, k, v, do_g, lse.reshape(H, 1, S), deff.reshape(H, 1, S), slopes[g], dil)
        parts.append(jnp.stack([_dil_unperm(t, dil).transpose(1, 0, 2) for t in (dq, dk, dv)], axis=1))
    dzd = jnp.stack(parts, axis=1)
    return [dzd.reshape(S, DIL_QKV).astype(BF)]


def _halo_specs2(ts, w, S):
    r = ts // HALO
    return [pl.BlockSpec((HALO, w), lambda i: (jnp.maximum(i * r - 1, 0), 0)),
            pl.BlockSpec((ts, w), lambda i: (i, 0)),
            pl.BlockSpec((HALO, w), lambda i: (jnp.minimum((i + 1) * r, S // HALO - 1), 0))]


def _cat_rows(i, ts, S, refs):
    v = jnp.concatenate([r[...].astype(F32) for r in refs], axis=0)
    pos = i * ts - HALO + lax.broadcasted_iota(jnp.int32, (ts + 2 * HALO, 1), 0)
    return v, (pos >= 0) & (pos < S)


def _conv_taps(buf_ref, w, ts, width, flip):
    acc = None
    for k in range(width):
        off = HALO + ((width // 2 - k) if flip else (k - width // 2))
        term = buf_ref[pl.ds(off, ts), :] * w[k:k + 1, :]
        acc = term if acc is None else acc + term
    return acc


def conformer_fwd(tag, z, conv_w, conv_b, ln_g, ln_b, ts=256):
    S = z.shape[0]
    ts = min(ts, S)
    C = CONV_CH
    a, gate, q_mem = z[:, :C], z[:, C:2 * C], z[:, 2 * C:]

    def body(ap, ac, an, gp, gc, gn, w_ref, b_ref, lg_ref, lb_ref, u_ref, u1_ref, buf):
        i = pl.program_id(0)
        av, ok = _cat_rows(i, ts, S, (ap, ac, an))
        gv, _ = _cat_rows(i, ts, S, (gp, gc, gn))
        buf[...] = jnp.where(ok, av * jax.nn.sigmoid(gv), 0.0)
        u1 = _conv_taps(buf, w_ref[...], ts, CONV_WIDTH, False) + b_ref[...]
        mu = jnp.mean(u1, axis=-1, keepdims=True)
        xc = u1 - mu
        y = xc * lax.rsqrt(jnp.mean(xc * xc, axis=-1, keepdims=True) + EPS) * lg_ref[...] + lb_ref[...]
        u_ref[...] = _silu(y).astype(u_ref.dtype)
        u1_ref[...] = u1

    full = lambda t: pl.BlockSpec(t.shape, lambda i: (0, 0))
    u, u1 = pl.pallas_call(
        body, grid=(S // ts,),
        in_specs=_halo_specs2(ts, C, S) + _halo_specs2(ts, C, S) + [full(conv_w), full(conv_b), full(ln_g), full(ln_b)],
        out_specs=[pl.BlockSpec((ts, C), lambda i: (i, 0))] * 2, out_shape=[SDS((S, C), BF), SDS((S, C), F32)],
        scratch_shapes=[pltpu.VMEM((ts + 2 * HALO, C), F32)], name=tag + "_conf",
        compiler_params=_params(("arbitrary",)))(a, a, a, gate, gate, gate, conv_w, conv_b, ln_g, ln_b)
    return u, q_mem, dict(a=a, gate=gate, u1=u1)


def conformer_bwd(tag, du, sv, conv_w, conv_b, ln_g, ln_b, ts=256):
    S = du.shape[0]
    ts = min(ts, S)
    C = CONV_CH

    def ln_b_fn(i, t, f):
        u1, d = t[1], t[0].astype(F32)
        mu = jnp.mean(u1, axis=-1, keepdims=True)
        xc = u1 - mu
        r = lax.rsqrt(jnp.mean(xc * xc, axis=-1, keepdims=True) + EPS)
        xh = xc * r
        dy = d * _dsilu(xh * f[0] + f[1])
        dxh = dy * f[0]
        du1 = r * (dxh - jnp.mean(dxh, axis=-1, keepdims=True) - xh * jnp.mean(dxh * xh, axis=-1, keepdims=True))
        return [du1], [jnp.sum(dy * xh, axis=0, keepdims=True), jnp.sum(dy, axis=0, keepdims=True),
                       jnp.sum(du1, axis=0, keepdims=True)]

    du1, dln_g, dln_b, dconv_b = rows_call(tag + "_confb1", ln_b_fn, S, 512, [du, sv['u1']], [ln_g, ln_b],
                                           [(C, F32)], [(1, C), (1, C), (1, C)])
    wpad = CONV_WIDTH + 1

    def body(dp, dc, dn, ap, ac, an, gp, gc, gn, w_ref, da_ref, dg_ref, dw_ref, ubuf, dbuf):
        i = pl.program_id(0)
        av, ok = _cat_rows(i, ts, S, (ap, ac, an))
        gv, _ = _cat_rows(i, ts, S, (gp, gc, gn))
        dv, _ = _cat_rows(i, ts, S, (dp, dc, dn))
        sg = jax.nn.sigmoid(gv)
        ubuf[...] = jnp.where(ok, av * sg, 0.0)
        dbuf[...] = jnp.where(ok, dv, 0.0)
        du0 = _conv_taps(dbuf, w_ref[...], ts, CONV_WIDTH, True)
        a_c, s_c = av[HALO:HALO + ts], sg[HALO:HALO + ts]
        da_ref[...] = (du0 * s_c).astype(da_ref.dtype)
        dg_ref[...] = (du0 * a_c * s_c * (1.0 - s_c)).astype(dg_ref.dtype)
        d_c = dbuf[pl.ds(HALO, ts), :]

        @pl.when(i == 0)
        def _():
            dw_ref[...] = jnp.zeros(dw_ref.shape, F32)

        for k in range(CONV_WIDTH):
            off = HALO + k - CONV_WIDTH // 2
            dw_ref[k:k + 1, :] += jnp.sum(d_c * ubuf[pl.ds(off, ts), :], axis=0, keepdims=True)

    full = lambda t: pl.BlockSpec(t.shape, lambda i: (0, 0))
    da, dgate, dw = pl.pallas_call(
        body, grid=(S // ts,),
        in_specs=_halo_specs2(ts, C, S) * 3 + [full(conv_w)],
        out_specs=[pl.BlockSpec((ts, C), lambda i: (i, 0))] * 2 + [pl.BlockSpec((wpad, C), lambda i: (0, 0))],
        out_shape=[SDS((S, C), BF), SDS((S, C), BF), SDS((wpad, C), F32)],
        scratch_shapes=[pltpu.VMEM((ts + 2 * HALO, C), F32)] * 2, name=tag + "_confb2",
        compiler_params=_params(("arbitrary",)))(du1, du1, du1, sv['a'], sv['a'], sv['a'], sv['gate'], sv['gate'],
                                                 sv['gate'], conv_w)
    grads = dict(c_conv_w=dw[:CONV_WIDTH], c_conv_b=dconv_b, c_ln_g=dln_g, c_ln_b=dln_b)
    return [da, dgate], grads


def shortconv_fwd(tag, z, conv_w, ts=256):
    S = z.shape[0]
    ts = min(ts, S)
    C = SC_CH
    bg, cg, hx, q_mem = z[:, :C], z[:, C:2 * C], z[:, 2 * C:3 * C], z[:, 3 * C:]

    def body(b_ref, cp, cc, cn, hp, hc, hn, w_ref, o_ref, buf):
        i = pl.program_id(0)
        cv, ok = _cat_rows(i, ts, S, (cp, cc, cn))
        hv, _ = _cat_rows(i, ts, S, (hp, hc, hn))
        buf[...] = jnp.where(ok, cv * hv, 0.0)
        o_ref[...] = (b_ref[...].astype(F32) * _conv_taps(buf, w_ref[...], ts, SC_WIDTH, False)).astype(o_ref.dtype)

    o = pl.pallas_call(
        body, grid=(S // ts,),
        in_specs=[pl.BlockSpec((ts, C), lambda i: (i, 0))] + _halo_specs2(ts, C, S) * 2
        + [pl.BlockSpec(conv_w.shape, lambda i: (0, 0))],
        out_specs=pl.BlockSpec((ts, C), lambda i: (i, 0)), out_shape=SDS((S, C), BF),
        scratch_shapes=[pltpu.VMEM((ts + 2 * HALO, C), F32)], name=tag + "_sconv",
        compiler_params=_params(("arbitrary",)))(bg, cg, cg, cg, hx, hx, hx, conv_w)
    return o, q_mem, dict(bg=bg, cg=cg, hx=hx)


def shortconv_bwd(tag, do, sv, conv_w, ts=256):
    S = do.shape[0]
    ts = min(ts, S)
    C = SC_CH
    wpad = 8

    def body(dp, dc, dn, bp, bc, bn, cp, cc, cn, hp, hc, hn, w_ref, db_ref, dcg_ref, dhx_ref, dw_ref, mbuf, dbuf):
        i = pl.program_id(0)
        dov, ok = _cat_rows(i, ts, S, (dp, dc, dn))
        bv, _ = _cat_rows(i, ts, S, (bp, bc, bn))
        cv, _ = _cat_rows(i, ts, S, (cp, cc, cn))
        hv, _ = _cat_rows(i, ts, S, (hp, hc, hn))
        mbuf[...] = jnp.where(ok, cv * hv, 0.0)
        dbuf[...] = jnp.where(ok, dov * bv, 0.0)
        conv = _conv_taps(mbuf, w_ref[...], ts, SC_WIDTH, False)
        db_ref[...] = (dov[HALO:HALO + ts] * conv).astype(db_ref.dtype)
        dm = _conv_taps(dbuf, w_ref[...], ts, SC_WIDTH, True)
        dcg_ref[...] = (dm * hv[HALO:HALO + ts]).astype(dcg_ref.dtype)
        dhx_ref[...] = (dm * cv[HALO:HALO + ts]).astype(dhx_ref.dtype)
        d_c = dbuf[pl.ds(HALO, ts), :]

        @pl.when(i == 0)
        def _():
            dw_ref[...] = jnp.zeros(dw_ref.shape, F32)

        for k in range(SC_WIDTH):
            off = HALO + k - SC_WIDTH // 2
            dw_ref[k:k + 1, :] += jnp.sum(d_c * mbuf[pl.ds(off, ts), :], axis=0, keepdims=True)

    dbg, dcg, dhx, dw = pl.pallas_call(
        body, grid=(S // ts,),
        in_specs=_halo_specs2(ts, C, S) * 4 + [pl.BlockSpec(conv_w.shape, lambda i: (0, 0))],
        out_specs=[pl.BlockSpec((ts, C), lambda i: (i, 0))] * 3 + [pl.BlockSpec((wpad, C), lambda i: (0, 0))],
        out_shape=[SDS((S, C), BF)] * 3 + [SDS((wpad, C), F32)],
        scratch_shapes=[pltpu.VMEM((ts + 2 * HALO, C), F32)] * 2, name=tag + "_sconvb",
        compiler_params=_params(("arbitrary",)))(do, do, do, sv['bg'], sv['bg'], sv['bg'], sv['cg'], sv['cg'],
                                                 sv['cg'], sv['hx'], sv['hx'], sv['hx'], conv_w)
    return [dbg, dcg, dhx], dict(d_conv_w=dw[:SC_WIDTH])


MIXERS = ('a', 'b', 'c', 'd')


def _mixer_weights(W, m):
    return W[m + '_w_in'][0], W[m + '_w_out'][0]


def layer_fwd(i, x, mem, rope, W):
    tag = f"l{i}"
    g = W['norm_g'][i]
    gr = lambda k: g[k:k + 1]
    m = MIXERS[i % len(MIXERS)]
    sv = {}
    x1, sv['ffn0'] = ffn_fwd(tag + "_f0", x, gr(0), gr(1), W['ffn_w_gate'][i, 0], W['ffn_w_up'][i, 0],
                             W['ffn_w_down'][i, 0])
    w_in, w_out = _mixer_weights(W, m)
    S = x.shape[0]
    h2 = rms_rows(tag + "_rms2", x1, gr(2))
    n_in = w_in.shape[1]
    z, = mm_call(tag + "_in", [[(h2, w_in, False, False)]], S, n_in, D_MODEL, tm=1024,
                 tn=(256 if n_in % 256 == 0 else n_in), tk=1024, **_plain(BF))
    if m == 'a':
        o, q_mem, sv['mix'] = mla_fwd(tag, z, rope, W['a_q_norm'], W['a_kv_norm'], W['a_w_uq'][0], W['a_w_ukv'][0])
    elif m == 'b':
        o, q_mem, sv['mix'] = dilated_fwd(tag, z)
    elif m == 'c':
        o, q_mem, sv['mix'] = conformer_fwd(tag, z, W['c_conv_w'][0], W['c_conv_b'], W['c_ln_g'], W['c_ln_b'])
    else:
        o, q_mem, sv['mix'] = shortconv_fwd(tag, z, W['d_conv_w'][0])
    kv = mem_kv_fwd(tag, mem, gr(6), W['mem_w_kv'][i])
    kbd, vbd = _block_diag_kv(kv)
    mo = mem_attn_fwd(tag, q_mem, kbd, vbd)
    cat = jnp.concatenate([o, mo], axis=1)
    n_out = cat.shape[1]

    def ep_out(a, t, c, f):
        return [a[0], t[0] + _rms(a[0], c[0])], []

    y2, x2 = mm_call(tag + "_out", [[(cat, w_out, False, False)]], S, D_MODEL, n_out, tm=512, tn=1024, tk=n_out,
                     epilogue=ep_out, outs=[BF, F32], tiles=[x1], cols=[gr(3)])
    sv.update(x1=x1, h2=h2, q_mem=q_mem, kbd=kbd, vbd=vbd, cat=cat, y2=y2)
    x3, sv['ffn1'] = ffn_fwd(tag + "_f1", x2, gr(4), gr(5), W['ffn_w_gate'][i, 1], W['ffn_w_up'][i, 1],
                             W['ffn_w_down'][i, 1])
    return x3, sv


def layer_bwd(i, dx3, sv, mem, rope, W):
    tag = f"l{i}"
    g = W['norm_g'][i]
    gr = lambda k: g[k:k + 1]
    m = MIXERS[i % len(MIXERS)]
    S = dx3.shape[0]
    dx2, f1 = ffn_bwd(tag + "_f1", dx3, sv['ffn1'], gr(4), gr(5), W['ffn_w_gate'][i, 1], W['ffn_w_up'][i, 1],
                      W['ffn_w_down'][i, 1])
    w_in, w_out = _mixer_weights(W, m)
    n_in, n_out = w_in.shape[1], w_out.shape[0]

    def post(ii, t, f):
        dx, dg = _rms_bwd(t[1].astype(F32), f[0], t[0])
        return [dx], [dg]

    dy2, dg3 = rows_call(tag + "_bpost", post, S, 512, [dx2, sv['y2']], [gr(3)], [(D_MODEL, BF)], [(1, D_MODEL)])
    dcat, = mm_call(tag + "_bout", [[(dy2, w_out, False, True)]], S, n_out, D_MODEL, tm=1024,
                    tn=(256 if n_out % 256 == 0 else n_out), tk=1024, **_plain(BF))
    dw_out, = mm_call(tag + "_dwout", [[(sv['cat'], dy2, True, False)]], n_out, D_MODEL, S, tm=n_out, tn=1024, tk=512,
                      **_plain(F32))
    n_o = n_out - MEM_WIDTH
    do, dmo = dcat[:, :n_o], dcat[:, n_o:]
    dq_mem, dkv = mem_attn_bwd(tag, sv['q_mem'], dmo, sv['kbd'], sv['vbd'])
    dw_kv, dg6 = mem_kv_bwd(tag, mem, gr(6), W['mem_w_kv'][i], dkv)
    if m == 'a':
        dzs, mg = mla_bwd(tag, do, sv['mix'], rope, W['a_q_norm'], W['a_kv_norm'], W['a_w_uq'][0], W['a_w_ukv'][0])
    elif m == 'b':
        dzs, mg = dilated_bwd(tag, do, sv['mix']), {}
    elif m == 'c':
        dzs, mg = conformer_bwd(tag, do, sv['mix'], W['c_conv_w'][0], W['c_conv_b'], W['c_ln_g'], W['c_ln_b'])
    else:
        dzs, mg = shortconv_bwd(tag, do, sv['mix'], W['d_conv_w'][0])
    dz = jnp.concatenate(dzs + [dq_mem], axis=1)
    dw_in, = mm_call(tag + "_dwin", [[(sv['h2'], dz, True, False)]], D_MODEL, n_in, S, tm=1024,
                     tn=(256 if n_in % 256 == 0 else n_in), tk=512, **_plain(F32))

    def ep_in(a, t, c, f):
        dx, dg = _rms_bwd(t[0], c[0], a[0])
        return [t[1] + dx], [dg]

    dx1, dg2 = mm_call(tag + "_bin", [[(dz, w_in, False, True)]], S, D_MODEL, n_in, tm=512, tn=1024,
                       tk=(n_in // 2 if n_in % 256 == 0 else n_in), epilogue=ep_in, outs=[F32],
                       tiles=[sv['x1'], dx2], cols=[gr(2)], reds=[1])
    dx0, f0 = ffn_bwd(tag + "_f0", dx1, sv['ffn0'], gr(0), gr(1), W['ffn_w_gate'][i, 0], W['ffn_w_up'][i, 0],
                      W['ffn_w_down'][i, 0])
    dg = jnp.concatenate([f0['g_in'], f0['g_out'], dg2, dg3, f1['g_in'], f1['g_out'], dg6], axis=0)
    grads = dict(norm_g=dg, ffn_w_gate=jnp.stack([f0['wg'], f1['wg']]), ffn_w_up=jnp.stack([f0['wu'], f1['wu']]),
                 ffn_w_down=jnp.stack([f0['wd'], f1['wd']]), mem_w_kv=dw_kv)
    grads[m + '_w_in'] = dw_in[None]
    grads[m + '_w_out'] = dw_out[None]
    for k, v in mg.items():
        grads[k] = v if k in ('a_q_norm', 'a_kv_norm', 'c_conv_b', 'c_ln_g', 'c_ln_b') else v[None]
    return dx0, grads


def local_step(x, mem, positions, target, W):
    S = x.shape[0]
    rope = _rope_tables(positions)
    saved = []
    for i in range(DEPTH):
        x, sv = layer_fwd(i, x, mem, rope, W)
        saved.append(sv)

    def loss_fn(i, t, f):
        e = t[0] - t[1]
        part = jnp.sum(0.5 * jnp.mean(e * e, axis=-1, keepdims=True), axis=0, keepdims=True)
        return [e * (1.0 / D_MODEL)], [jnp.broadcast_to(part, (1, 128))]

    dx, loss = rows_call("loss", loss_fn, S, 512, [x, target], [], [(D_MODEL, F32)], [(1, 128)])
    per_layer = []
    for i in reversed(range(DEPTH)):
        dx, gr = layer_bwd(i, dx, saved[i], mem, rope, W)
        per_layer.append(gr)
    per_layer = per_layer[::-1]
    grads = {}
    for n in ('norm_g', 'ffn_w_gate', 'ffn_w_up', 'ffn_w_down', 'mem_w_kv'):
        grads[n] = jnp.stack([p[n] for p in per_layer])
    for p in per_layer:
        for n, v in p.items():
            if n not in grads:
                grads[n] = v
    return loss[0, 0], dx, grads


def _local_shape(full_shape, axis):
    s = list(full_shape)
    if axis is not None:
        s[axis] //= N_DEV
    return tuple(s)


def _pad_to(n, m):
    return -(-n // m) * m


def pack_local(arrs, dtype, row_mult):
    flat = jnp.concatenate([a.reshape(-1).astype(dtype) for a in arrs])
    n = _pad_to(flat.shape[0], PACK_W * row_mult)
    return jnp.pad(flat, (0, n - flat.shape[0])).reshape(n // PACK_W, PACK_W)


def unpack_local(buf, names, shapes):
    flat = buf.reshape(-1)
    out, off = {}, 0
    for n in names:
        sz = math.prod(shapes[n])
        out[n] = flat[off:off + sz].reshape(shapes[n])
        off += sz
    return out


def unpack_gathered(buf, names, shapes):
    flat = buf.reshape(N_DEV, -1)
    out, off = {}, 0
    for n in names:
        ax = SHARD_AXIS[n]
        sz = math.prod(shapes[n])
        t = flat[:, off:off + sz].reshape((N_DEV,) + tuple(shapes[n]))
        t = jnp.moveaxis(t, 0, ax)
        full = list(shapes[n])
        full[ax] *= N_DEV
        out[n] = t.reshape(full)
        off += sz
    return out


def pack_grads(grads, names, row_mult):
    parts = []
    for n in names:
        g = grads[n].astype(F32)
        ax = SHARD_AXIS[n]
        if ax is None:
            parts.append(jnp.broadcast_to(g.reshape(1, -1), (N_DEV, g.size)))
        else:
            shp = list(g.shape)
            t = g.reshape(shp[:ax] + [N_DEV, shp[ax] // N_DEV] + shp[ax + 1:])
            parts.append(jnp.moveaxis(t, ax, 0).reshape(N_DEV, -1))
    flat = jnp.concatenate(parts, axis=1)
    n = _pad_to(flat.shape[1], PACK_W * row_mult)
    return jnp.pad(flat, ((0, 0), (0, n - flat.shape[1]))).reshape(N_DEV, n // PACK_W, PACK_W)


COL_SHARDED = ['a_w_uq', 'a_w_ukv', 'b_w_in', 'b_w_out', 'c_w_in', 'd_w_in']
ROW_SHARDED_1024 = ['a_w_out', 'c_w_out', 'd_w_out']
FLAT = SMALL + ['a_q_norm', 'a_kv_norm']


def wire_weights(a):
    n_ff = DEPTH * 2 * D_MODEL
    pad = lambda t: jnp.pad(t.reshape(n_ff, FF_SHARD), ((0, 0), (0, FF_SLOT - FF_SHARD)))
    w384 = jnp.concatenate([pad(a['ffn_w_gate']), pad(a['ffn_w_up'])], axis=0)
    w1024 = jnp.concatenate([a['ffn_w_down'].reshape(DEPTH * 2 * FF_SHARD, D_MODEL)]
                            + [a[n][0] for n in ROW_SHARDED_1024], axis=0)
    rest = [a['mem_w_kv'].reshape(-1, 2 * MEM_WIDTH), a['a_w_in'][0]] + [a[n][0] for n in COL_SHARDED]
    return [t.astype(BF) for t in [w384, w1024] + rest]


def unwire_weights(g):
    n_ff = DEPTH * 2 * D_MODEL
    W = {}
    t = g[0].transpose(1, 0, 2).reshape(2 * n_ff, D_FFP)
    W['ffn_w_gate'] = t[:n_ff].reshape(DEPTH, 2, D_MODEL, D_FFP)
    W['ffn_w_up'] = t[n_ff:].reshape(DEPTH, 2, D_MODEL, D_FFP)
    n_dn = DEPTH * 2 * FF_SHARD
    dn = g[1][:, :n_dn].reshape(N_DEV, DEPTH, 2, FF_SHARD, D_MODEL)
    dn = jnp.pad(dn, ((0, 0), (0, 0), (0, 0), (0, FF_SLOT - FF_SHARD), (0, 0)))
    W['ffn_w_down'] = dn.transpose(1, 2, 0, 3, 4).reshape(DEPTH, 2, D_FFP, D_MODEL)
    rows = D_MODEL // N_DEV
    for k, n in enumerate(ROW_SHARDED_1024):
        W[n] = g[1][:, n_dn + k * rows:n_dn + (k + 1) * rows].reshape(1, D_MODEL, D_MODEL)
    W['mem_w_kv'] = g[2].reshape(N_DEV, DEPTH, rows, 2 * MEM_WIDTH).transpose(1, 0, 2, 3).reshape(DEPTH, D_MODEL, -1)
    W['a_w_in'] = g[3].reshape(1, D_MODEL, -1)
    for k, n in enumerate(COL_SHARDED):
        t = g[4 + k]
        W[n] = t.transpose(1, 0, 2).reshape(1, t.shape[1], N_DEV * t.shape[2])
    return W


def wire_grads(grads):
    n_ff = DEPTH * 2 * D_MODEL
    slot = lambda t: t.reshape(n_ff, N_DEV, FF_SLOT).transpose(1, 0, 2)
    p384 = jnp.concatenate([slot(grads['ffn_w_gate']), slot(grads['ffn_w_up'])], axis=1)
    dn = grads['ffn_w_down'].reshape(DEPTH, 2, N_DEV, FF_SLOT, D_MODEL)[:, :, :, :FF_SHARD]
    dn = dn.transpose(2, 0, 1, 3, 4).reshape(N_DEV, DEPTH * 2 * FF_SHARD, D_MODEL)
    rows = D_MODEL // N_DEV
    p1024 = jnp.concatenate([dn] + [grads[n].reshape(N_DEV, rows, D_MODEL) for n in ROW_SHARDED_1024], axis=1)
    kv = grads['mem_w_kv'].reshape(DEPTH, N_DEV, rows, 2 * MEM_WIDTH).transpose(1, 0, 2, 3)
    rest = [kv.reshape(N_DEV, DEPTH * rows, -1), grads['a_w_in'].reshape(N_DEV, rows, -1)]
    for n in COL_SHARDED:
        t = grads[n][0]
        rest.append(t.reshape(t.shape[0], N_DEV, -1).transpose(1, 0, 2))
    return [t.astype(BF) for t in [p384, p1024] + rest]


def _peers():
    x, y, c = lax.axis_index("x"), lax.axis_index("y"), lax.axis_index("c")
    me = 4 * x + 2 * y + c
    peers = []
    for k in range(1, N_DEV):
        px = 1 - x if k & 4 else x
        py = 1 - y if k & 2 else y
        pc = 1 - c if k & 1 else c
        peers.append(((px, py, pc), 4 * px + 2 * py + pc))
    return me, peers


def _exchange(name, arrays, scatter):
    n = len(arrays)

    def body(*refs):
        srcs, outs = refs[:n], refs[n:2 * n]
        send_sems, recv_sems, local_sems = refs[2 * n:]
        me, peers = _peers()
        pending = []
        for a, (src, out) in enumerate(zip(srcs, outs)):
            mine = pltpu.make_async_copy(src.at[me] if scatter else src, out.at[me], local_sems.at[a])
            mine.start()
            pending.append(mine)
        sends = []
        for a, (src, out) in enumerate(zip(srcs, outs)):
            for k, (dev, idx) in enumerate(peers):
                cp = pltpu.make_async_remote_copy(src_ref=src.at[idx] if scatter else src, dst_ref=out.at[me],
                                                  send_sem=send_sems.at[a, k], recv_sem=recv_sems.at[a, k],
                                                  device_id=dev, device_id_type=pl.DeviceIdType.MESH)
                cp.start()
                sends.append(cp)
        for a, (src, out) in enumerate(zip(srcs, outs)):
            for k, (dev, idx) in enumerate(peers):
                pltpu.make_async_remote_copy(src_ref=src.at[idx] if scatter else src, dst_ref=out.at[idx],
                                             send_sem=send_sems.at[a, k], recv_sem=recv_sems.at[a, k],
                                             device_id=dev, device_id_type=pl.DeviceIdType.MESH).wait_recv()
        for cp in sends:
            cp.wait_send()
        for mine in pending:
            mine.wait()

    out_shape = [SDS(t.shape if scatter else (N_DEV,) + t.shape, t.dtype) for t in arrays]
    hbm = pl.BlockSpec(memory_space=pltpu.HBM)
    return pl.pallas_call(
        body, out_shape=out_shape, in_specs=[hbm] * n, out_specs=[hbm] * n,
        scratch_shapes=[pltpu.SemaphoreType.DMA((n, N_DEV - 1)), pltpu.SemaphoreType.DMA((n, N_DEV - 1)),
                        pltpu.SemaphoreType.DMA((n,))], name=name)(*arrays)


ADAM_ROWS = 128


def sum_adamw(name, parts, row_off, w, m, v, tr):
    rows, wl = w.shape
    wp = parts.shape[2]
    tr = min(tr, rows)
    assert rows % tr == 0 and row_off % tr == 0, (name, rows, row_off, tr)
    off = row_off // tr

    def body(p_ref, w_ref, m_ref, v_ref, g_out, d_out, m_out, v_out):
        g = p_ref[0].astype(F32)
        for q in range(1, N_DEV):
            g = g + p_ref[q].astype(F32)
        g = g[:, :wl]
        m_new = ADAM_B1 * m_ref[...] + (1.0 - ADAM_B1) * g
        v_new = ADAM_B2 * v_ref[...] + (1.0 - ADAM_B2) * jnp.square(g)
        m_hat = m_new / (1.0 - ADAM_B1 ** ADAM_STEP)
        v_hat = v_new / (1.0 - ADAM_B2 ** ADAM_STEP)
        g_out[...] = g
        d_out[...] = -ADAM_LR * (m_hat / (jnp.sqrt(v_hat) + ADAM_EPS) + ADAM_WD * w_ref[...])
        m_out[...] = m_new
        v_out[...] = v_new

    blk = pl.BlockSpec((tr, wl), lambda i: (i, 0))
    return pl.pallas_call(
        body, grid=(rows // tr,),
        in_specs=[pl.BlockSpec((N_DEV, tr, wp), lambda i: (0, i + off, 0)), blk, blk, blk],
        out_specs=[blk] * 4, out_shape=[SDS((rows, wl), F32)] * 4, name=name,
        compiler_params=_params(("arbitrary",)))(parts, w, m, v)


def kernel(x, mem, positions, norm_g, ffn_w_gate, ffn_w_up, ffn_w_down, mem_w_kv, a_w_in, a_q_norm, a_kv_norm, a_w_uq, a_w_ukv, a_w_out, b_w_in, b_w_out, c_w_in, c_conv_w, c_conv_b, c_ln_g, c_ln_b, c_w_out, d_w_in, d_conv_w, d_w_out, loss_target, m_norm_g, m_ffn_w_gate, m_ffn_w_up, m_ffn_w_down, m_mem_w_kv, m_a_w_in, m_a_q_norm, m_a_kv_norm, m_a_w_uq, m_a_w_ukv, m_a_w_out, m_b_w_in, m_b_w_out, m_c_w_in, m_c_conv_w, m_c_conv_b, m_c_ln_g, m_c_ln_b, m_c_w_out, m_d_w_in, m_d_conv_w, m_d_w_out, v_norm_g, v_ffn_w_gate, v_ffn_w_up, v_ffn_w_down, v_mem_w_kv, v_a_w_in, v_a_q_norm, v_a_kv_norm, v_a_w_uq, v_a_w_ukv, v_a_w_out, v_b_w_in, v_b_w_out, v_c_w_in, v_c_conv_w, v_c_conv_b, v_c_ln_g, v_c_ln_b, v_c_w_out, v_d_w_in, v_d_conv_w, v_d_w_out):
    a = dict(zip(IN_NAMES, (x, mem, positions, norm_g, ffn_w_gate, ffn_w_up, ffn_w_down, mem_w_kv, a_w_in, a_q_norm, a_kv_norm, a_w_uq, a_w_ukv, a_w_out, b_w_in, b_w_out, c_w_in, c_conv_w, c_conv_b, c_ln_g, c_ln_b, c_w_out, d_w_in, d_conv_w, d_w_out, loss_target, m_norm_g, m_ffn_w_gate, m_ffn_w_up, m_ffn_w_down, m_mem_w_kv, m_a_w_in, m_a_q_norm, m_a_kv_norm, m_a_w_uq, m_a_w_ukv, m_a_w_out, m_b_w_in, m_b_w_out, m_c_w_in, m_c_conv_w, m_c_conv_b, m_c_ln_g, m_c_ln_b, m_c_w_out, m_d_w_in, m_d_conv_w, m_d_w_out, v_norm_g, v_ffn_w_gate, v_ffn_w_up, v_ffn_w_down, v_mem_w_kv, v_a_w_in, v_a_q_norm, v_a_kv_norm, v_a_w_uq, v_a_w_ukv, v_a_w_out, v_b_w_in, v_b_w_out, v_c_w_in, v_c_conv_w, v_c_conv_b, v_c_ln_g, v_c_ln_b, v_c_w_out, v_d_w_in, v_d_conv_w, v_d_w_out)))
    shapes = {n: tuple(a[n].shape) for n in WEIGHTS}
    gathered = _exchange("gather_weights", wire_weights(a) + [pack_local([a[n] for n in SMALL], F32, 8)], False)
    W = unwire_weights(gathered[:-1])
    W.update(unpack_gathered(gathered[-1], SMALL, shapes))
    W['a_q_norm'], W['a_kv_norm'] = a['a_q_norm'], a['a_kv_norm']
    loss, grad_x, grads = local_step(a['x'][0], a['mem'][0], a['positions'][0], a['loss_target'][0], W)
    parts = _exchange("scatter_grads", wire_grads(grads) + [pack_grads(grads, FLAT, ADAM_ROWS)], True)
    res = {}

    def adam(n, p, row_off, rows, tr):
        view = [a[pre + n].reshape(rows, -1) for pre in ('', 'm_', 'v_')]
        res[n] = [t.reshape(shapes[n]) for t in sum_adamw("adamw_" + n, p, row_off, *view, tr)]

    n_ff = DEPTH * 2 * D_MODEL
    adam('ffn_w_gate', parts[0], 0, n_ff, 256)
    adam('ffn_w_up', parts[0], n_ff, n_ff, 256)
    n_dn = DEPTH * 2 * FF_SHARD
    adam('ffn_w_down', parts[1], 0, n_dn, 128)
    rows = D_MODEL // N_DEV
    for k, n in enumerate(ROW_SHARDED_1024):
        adam(n, parts[1], n_dn + k * rows, rows, 128)
    adam('mem_w_kv', parts[2], 0, DEPTH * rows, 128)
    adam('a_w_in', parts[3], 0, rows, 128)
    for k, n in enumerate(COL_SHARDED):
        adam(n, parts[4 + k], 0, shapes[n][1], 128)
    packed = [pack_local([a[pre + n] for n in FLAT], F32, ADAM_ROWS) for pre in ('', 'm_', 'v_')]
    flat = [unpack_local(r, FLAT, shapes) for r in sum_adamw("adamw_flat", parts[-1], 0, *packed, ADAM_ROWS)]
    for n in FLAT:
        res[n] = [f[n] for f in flat]
    loss = lax.psum(loss, MESH_AXES)
    return (loss, grad_x[None], *[res[n][k] for k in range(4) for n in WEIGHTS])
```

```python
import functools
import math

import jax
import jax.numpy as jnp
from jax import lax
from jax.experimental import pallas as pl
from jax.experimental.pallas import tpu as pltpu

BF, F32 = jnp.bfloat16, jnp.float32
SDS = jax.ShapeDtypeStruct

D_MODEL, DEPTH, D_FF = 1024, 4, 2816
HEAD_DIM, MEM_HEADS, MEM_WIDTH, N_MEM = 64, 4, 256, 256
MLA_HEADS, MLA_Q_LORA, MLA_KV_LORA, MLA_NOPE, MLA_ROPE, MLA_V = 12, 384, 256, 64, 32, 64
MLA_QK = MLA_NOPE + MLA_ROPE
ROPE_THETA = 10000.0
DIL_GROUPS = ((128, 1), (512, 4), (2048, 16))
DIL_HEADS, DIL_HALF = 8, 64
DIL_QKV = len(DIL_GROUPS) * 3 * DIL_HEADS * HEAD_DIM
ALIBI_MAX = 8.0
CONV_CH, CONV_WIDTH, SC_CH, SC_WIDTH = 768, 31, 768, 3
EPS, NEG = 1e-6, -1e30
ADAM_LR, ADAM_B1, ADAM_B2, ADAM_EPS, ADAM_WD, ADAM_STEP = 0.001, 0.9, 0.999, 1e-08, 0.01, 10

N_DEV = 8
MESH_AXES = ("x", "y", "c")
FF_SHARD = D_FF // N_DEV
FF_SLOT = 384
D_FFP = N_DEV * FF_SLOT
LOG2E = 1.4426950408889634
V7X_VMEM_LIMIT = 48 * 1024 * 1024
HALO = 16
PACK_W = 1024

WEIGHTS = ['norm_g', 'ffn_w_gate', 'ffn_w_up', 'ffn_w_down', 'mem_w_kv', 'a_w_in', 'a_q_norm', 'a_kv_norm',
           'a_w_uq', 'a_w_ukv', 'a_w_out', 'b_w_in', 'b_w_out', 'c_w_in', 'c_conv_w', 'c_conv_b', 'c_ln_g',
           'c_ln_b', 'c_w_out', 'd_w_in', 'd_conv_w', 'd_w_out']
SHARD_AXIS = dict(norm_g=2, ffn_w_gate=3, ffn_w_up=3, ffn_w_down=2, mem_w_kv=1, a_w_in=1, a_q_norm=None,
                  a_kv_norm=None, a_w_uq=2, a_w_ukv=2, a_w_out=1, b_w_in=2, b_w_out=2, c_w_in=2, c_conv_w=2,
                  c_conv_b=1, c_ln_g=1, c_ln_b=1, c_w_out=1, d_w_in=2, d_conv_w=2, d_w_out=1)
SMALL = ['norm_g', 'c_conv_w', 'c_conv_b', 'c_ln_g', 'c_ln_b', 'd_conv_w']
BIG = [n for n in WEIGHTS if n not in SMALL and SHARD_AXIS[n] is not None]
IN_NAMES = (['x', 'mem', 'positions'] + WEIGHTS + ['loss_target'] + ['m_' + n for n in WEIGHTS]
            + ['v_' + n for n in WEIGHTS])


def _params(sem):
    return pltpu.CompilerParams(dimension_semantics=sem, vmem_limit_bytes=V7X_VMEM_LIMIT)


def _dot(a, b, ca=1, cb=0):
    return lax.dot_general(a, b, (((ca,), (cb,)), ((), ())), preferred_element_type=F32)


def mm_call(name, accs, M, N, K, *, tm, tn, tk, epilogue, outs, tiles=(), cols=(), fulls=(), reds=()):
    tm, tn, tk = min(tm, M), min(tn, N), min(tk, K)
    assert M % tm == 0 and N % tn == 0 and K % tk == 0, (name, M, N, K, tm, tn, tk)
    gj, gi, gk = N // tn, M // tm, K // tk
    flat = [p for acc in accs for p in acc]
    in_specs, args = [], []
    for (a, b, ta, tb) in flat:
        in_specs.append(pl.BlockSpec((tk, tm), lambda j, i, k: (k, i)) if ta else pl.BlockSpec((tm, tk), lambda j, i, k: (i, k)))
        in_specs.append(pl.BlockSpec((tn, tk), lambda j, i, k: (j, k)) if tb else pl.BlockSpec((tk, tn), lambda j, i, k: (k, j)))
        args += [a, b]
    for t in tiles:
        in_specs.append(pl.BlockSpec((tm, tn), lambda j, i, k: (i, j)))
        args.append(t)
    for c in cols:
        in_specs.append(pl.BlockSpec((c.shape[0], tn), lambda j, i, k: (0, j)))
        args.append(c)
    for f in fulls:
        in_specs.append(pl.BlockSpec(f.shape, lambda j, i, k: (0,) * f.ndim))
        args.append(f)
    out_shape = [SDS((M, N), dt) for dt in outs] + [SDS((r, N), F32) for r in reds]
    out_specs = ([pl.BlockSpec((tm, tn), lambda j, i, k: (i, j)) for _ in outs]
                 + [pl.BlockSpec((r, tn), lambda j, i, k: (0, j)) for r in reds])
    n_in = len(args)
    n_acc = len(accs)

    def body(*refs):
        in_refs = refs[:n_in]
        out_refs = refs[n_in:n_in + len(outs)]
        red_refs = refs[n_in + len(outs):n_in + len(outs) + len(reds)]
        acc_refs = refs[n_in + len(outs) + len(reds):]
        i, k = pl.program_id(1), pl.program_id(2)
        sums, idx = [], 0
        for acc in accs:
            tot = None
            for (_, _, ta, tb) in acc:
                av = in_refs[2 * idx][...].astype(BF)
                bv = in_refs[2 * idx + 1][...].astype(BF)
                idx += 1
                d = _dot(av, bv, 0 if ta else 1, 1 if tb else 0)
                tot = d if tot is None else tot + d
            sums.append(tot)

        def finish(vals):
            p = 2 * len(flat)
            tv = [r[...] for r in in_refs[p:p + len(tiles)]]
            cv = [r[...] for r in in_refs[p + len(tiles):p + len(tiles) + len(cols)]]
            fv = [r[...] for r in in_refs[p + len(tiles) + len(cols):]]
            ov, rv = epilogue(vals, tv, cv, fv)
            for r, v in zip(out_refs, ov):
                r[...] = v.astype(r.dtype)
            for r, v in zip(red_refs, rv):
                @pl.when(i == 0)
                def _(r=r, v=v):
                    r[...] = v

                @pl.when(i > 0)
                def _(r=r, v=v):
                    r[...] += v

        if gk == 1:
            finish(sums)
        else:
            @pl.when(k == 0)
            def _():
                for r, v in zip(acc_refs, sums):
                    r[...] = v

            @pl.when(k > 0)
            def _():
                for r, v in zip(acc_refs, sums):
                    r[...] += v

            @pl.when(k == gk - 1)
            def _():
                finish([r[...] for r in acc_refs])

    scratch = [] if gk == 1 else [pltpu.VMEM((tm, tn), F32) for _ in range(n_acc)]
    res = pl.pallas_call(body, grid=(gj, gi, gk), in_specs=in_specs, out_specs=out_specs, out_shape=out_shape,
                         scratch_shapes=scratch, name=name,
                         compiler_params=_params(("arbitrary", "arbitrary", "arbitrary")))(*args)
    return res


def _plain(dt):
    return dict(epilogue=lambda a, t, c, f: ([a[0]], []), outs=[dt])


def rows_call(name, fn, S, ts, tiled, fulls, outs, reds=()):
    ts = min(ts, S)
    assert S % ts == 0, (name, S, ts)
    in_specs = ([pl.BlockSpec((ts, t.shape[1]), lambda i: (i, 0)) for t in tiled]
                + [pl.BlockSpec(f.shape, lambda i, n=f.ndim: (0,) * n) for f in fulls])
    out_shape = [SDS((S, w), dt) for (w, dt) in outs] + [SDS(rw, F32) for rw in reds]
    out_specs = ([pl.BlockSpec((ts, w), lambda i: (i, 0)) for (w, _) in outs]
                 + [pl.BlockSpec(rw, lambda i: (0, 0)) for rw in reds])
    nt, nf, no = len(tiled), len(fulls), len(outs)

    def body(*refs):
        i = pl.program_id(0)
        tv = [r[...] for r in refs[:nt]]
        fv = [r[...] for r in refs[nt:nt + nf]]
        ov, rv = fn(i, tv, fv)
        for r, v in zip(refs[nt + nf:nt + nf + no], ov):
            r[...] = v.astype(r.dtype)
        for r, v in zip(refs[nt + nf + no:], rv):
            @pl.when(i == 0)
            def _(r=r, v=v):
                r[...] = v

            @pl.when(i > 0)
            def _(r=r, v=v):
                r[...] += v

    return pl.pallas_call(body, grid=(S // ts,), in_specs=in_specs, out_specs=out_specs, out_shape=out_shape,
                          name=name, compiler_params=_params(("arbitrary",)))(*tiled, *fulls)


def hrows_call(name, fn, H, S, ts, tiled, outs):
    ts = min(ts, S)
    in_specs = [pl.BlockSpec((1, ts, t.shape[2]), lambda h, i: (h, i, 0)) for t in tiled]
    out_shape = [SDS((H, S, w), dt) for (w, dt) in outs]
    out_specs = [pl.BlockSpec((1, ts, w), lambda h, i: (h, i, 0)) for (w, _) in outs]
    nt = len(tiled)

    def body(*refs):
        ov = fn([r[0] for r in refs[:nt]])
        for r, v in zip(refs[nt:], ov):
            r[0] = v.astype(r.dtype)

    return pl.pallas_call(body, grid=(H, S // ts), in_specs=in_specs, out_specs=out_specs, out_shape=out_shape,
                          name=name, compiler_params=_params(("arbitrary", "arbitrary")))(*tiled)


def _rms(x, g):
    r = lax.rsqrt(jnp.mean(x * x, axis=-1, keepdims=True) + EPS)
    return x * r * g


def _rms_bwd(x, g, dy):
    r = lax.rsqrt(jnp.mean(x * x, axis=-1, keepdims=True) + EPS)
    xh = x * r
    dg = jnp.sum(dy * xh, axis=0, keepdims=True)
    dxh = dy * g
    dx = r * (dxh - xh * jnp.mean(dxh * xh, axis=-1, keepdims=True))
    return dx, dg


def _silu(x):
    return x * jax.nn.sigmoid(x)


def _dsilu(x):
    s = jax.nn.sigmoid(x)
    return s * (1.0 + x * (1.0 - s))


def rms_rows(name, x, g):
    S, w = x.shape
    return rows_call(name, lambda i, t, f: ([_rms(t[0].astype(F32), f[0])], []), S, 512, [x], [g], [(w, BF)])[0]


def ffn_fwd(tag, x0, g_in, g_out, wg, wu, wd):
    S = x0.shape[0]
    h = rms_rows(tag + "_rms", x0, g_in)

    def ep_up(a, t, c, f):
        return [a[0], a[1], _silu(a[0]) * a[1]], []

    gate, up, act = mm_call(tag + "_up", [[(h, wg, False, False)], [(h, wu, False, False)]], S, D_FFP, D_MODEL,
                            tm=1024, tn=512, tk=1024, epilogue=ep_up, outs=[BF, BF, BF])

    def ep_down(a, t, c, f):
        return [a[0], t[0] + 0.5 * _rms(a[0], c[0])], []

    y, x1 = mm_call(tag + "_down", [[(act, wd, False, False)]], S, D_MODEL, D_FFP, tm=512, tn=1024, tk=1024,
                    epilogue=ep_down, outs=[BF, F32], tiles=[x0], cols=[g_out])
    return x1, dict(x0=x0, h=h, gate=gate, up=up, act=act, y=y)


def ffn_bwd(tag, dx1, sv, g_in, g_out, wg, wu, wd):
    S = dx1.shape[0]

    def post(i, t, f):
        dx, dg = _rms_bwd(t[1].astype(F32), f[0], 0.5 * t[0])
        return [dx], [dg]

    da, dg_out = rows_call(tag + "_bpost", post, S, 512, [dx1, sv['y']], [g_out], [(D_MODEL, BF)], [(1, D_MODEL)])

    def ep_act(a, t, c, f):
        g, u = t[0].astype(F32), t[1].astype(F32)
        return [a[0] * u * _dsilu(g), a[0] * _silu(g)], []

    dgate, dup = mm_call(tag + "_bact", [[(da, wd, False, True)]], S, D_FFP, D_MODEL, tm=1024, tn=512, tk=1024,
                         epilogue=ep_act, outs=[BF, BF], tiles=[sv['gate'], sv['up']])
    dwd, = mm_call(tag + "_dwd", [[(sv['act'], da, True, False)]], D_FFP, D_MODEL, S, tm=1024, tn=1024, tk=512,
                   **_plain(F32))
    dwg, dwu = mm_call(tag + "_dwgu", [[(sv['h'], dgate, True, False)], [(sv['h'], dup, True, False)]],
                       D_MODEL, D_FFP, S, tm=1024, tn=512, tk=1024,
                       epilogue=lambda a, t, c, f: ([a[0], a[1]], []), outs=[F32, F32])

    def ep_in(a, t, c, f):
        dx, dg = _rms_bwd(t[0], c[0], a[0])
        return [t[1] + dx], [dg]

    (dx0, dg_in) = mm_call(tag + "_bin", [[(dgate, wg, False, True), (dup, wu, False, True)]], S, D_MODEL, D_FFP,
                           tm=512, tn=1024, tk=1024, epilogue=ep_in, outs=[F32], tiles=[sv['x0'], dx1], cols=[g_in],
                           reds=[1])
    return dx0, dict(g_in=dg_in, g_out=dg_out, wg=dwg, wu=dwu, wd=dwd)


def mem_kv_fwd(tag, mem, g6, w_kv):
    def body(mem_ref, g_ref, w_ref, kv_ref):
        kv_ref[...] = _dot(_rms(mem_ref[...], g_ref[...]).astype(BF), w_ref[...])

    return pl.pallas_call(body, out_shape=SDS((N_MEM, 2 * MEM_WIDTH), F32), name=tag + "_memkv",
                          compiler_params=pltpu.CompilerParams(vmem_limit_bytes=V7X_VMEM_LIMIT))(mem, g6, w_kv)


def mem_kv_bwd(tag, mem, g6, w_kv, dkv):
    def body(mem_ref, g_ref, w_ref, dkv_ref, dw_ref, dg_ref):
        x = mem_ref[...]
        r = lax.rsqrt(jnp.mean(x * x, axis=-1, keepdims=True) + EPS)
        xh = x * r
        d = dkv_ref[...].astype(BF)
        dw_ref[...] = _dot((xh * g_ref[...]).astype(BF), d, 0, 0)
        dmem_n = _dot(d, w_ref[...], 1, 1)
        dg_ref[...] = jnp.sum(dmem_n * xh, axis=0, keepdims=True)

    return pl.pallas_call(body, out_shape=(SDS((D_MODEL, 2 * MEM_WIDTH), F32), SDS((1, D_MODEL), F32)),
                          name=tag + "_memkvb",
                          compiler_params=pltpu.CompilerParams(vmem_limit_bytes=V7X_VMEM_LIMIT))(mem, g6, w_kv, dkv)


def _block_diag_kv(kv):
    k = kv[:, :MEM_WIDTH].reshape(N_MEM, MEM_HEADS, HEAD_DIM)
    v = kv[:, MEM_WIDTH:].reshape(N_MEM, MEM_HEADS, HEAD_DIM)
    eye = jnp.eye(MEM_HEADS, dtype=kv.dtype)
    kbd = jnp.einsum('nhd,hg->hdgn', k, eye).reshape(MEM_WIDTH, MEM_HEADS * N_MEM)
    vbd = jnp.einsum('nhd,hg->hngd', v, eye).reshape(MEM_HEADS * N_MEM, MEM_WIDTH)
    return kbd.astype(BF), vbd.astype(BF)


def _mem_probs(q, kbd):
    s = _dot(q, kbd) * (HEAD_DIM ** -0.5)
    ps = []
    for h in range(MEM_HEADS):
        sh = s[:, h * N_MEM:(h + 1) * N_MEM]
        e = jnp.exp(sh - jnp.max(sh, axis=-1, keepdims=True))
        ps.append(e / jnp.sum(e, axis=-1, keepdims=True))
    return ps


def mem_attn_fwd(tag, q_mem, kbd, vbd):
    S = q_mem.shape[0]

    def fn(i, t, f):
        p = jnp.concatenate(_mem_probs(t[0], f[0]), axis=1).astype(BF)
        return [_dot(p, f[1])], []

    return rows_call(tag + "_mem", fn, S, 512, [q_mem], [kbd, vbd], [(MEM_WIDTH, BF)])[0]


def mem_attn_bwd(tag, q_mem, dmo, kbd, vbd):
    S = q_mem.shape[0]

    def fn(i, t, f):
        q, do = t
        ps = _mem_probs(q, f[0])
        dp = _dot(do, f[1], 1, 1)
        dss = []
        for h in range(MEM_HEADS):
            dph = dp[:, h * N_MEM:(h + 1) * N_MEM]
            dss.append(ps[h] * (dph - jnp.sum(dph * ps[h], axis=-1, keepdims=True)) * (HEAD_DIM ** -0.5))
        ds = jnp.concatenate(dss, axis=1).astype(BF)
        p = jnp.concatenate(ps, axis=1).astype(BF)
        return [_dot(ds, f[0], 1, 1)], [_dot(q, ds, 0, 0), _dot(p, do, 0, 0)]

    dq, dkbd, dvbd = rows_call(tag + "_memb", fn, S, 512, [q_mem, dmo], [kbd, vbd], [(MEM_WIDTH, BF)],
                               [(MEM_WIDTH, MEM_HEADS * N_MEM), (MEM_HEADS * N_MEM, MEM_WIDTH)])
    dk = jnp.einsum('hdhn->nhd', dkbd.reshape(MEM_HEADS, HEAD_DIM, MEM_HEADS, N_MEM)).reshape(N_MEM, MEM_WIDTH)
    dv = jnp.einsum('hnhd->nhd', dvbd.reshape(MEM_HEADS, N_MEM, MEM_HEADS, HEAD_DIM)).reshape(N_MEM, MEM_WIDTH)
    return dq, jnp.concatenate([dk, dv], axis=1)


def _rope_tables(positions):
    half = MLA_ROPE // 2
    inv = ROPE_THETA ** (-jnp.arange(half, dtype=F32) / half)
    ang = positions.astype(F32)[:, None] * inv
    cos, sin = jnp.cos(ang), jnp.sin(ang)
    cos32, sin32 = jnp.concatenate([cos, cos], 1), jnp.concatenate([sin, sin], 1)
    idx = jnp.arange(MLA_ROPE)
    p32 = (jnp.where((idx[:, None] == idx[None, :] + half), -1.0, 0.0)
           + jnp.where((idx[:, None] + half == idx[None, :]), 1.0, 0.0)).astype(F32)
    S = positions.shape[0]
    ones = jnp.ones((S, MLA_NOPE), F32)
    cq = jnp.tile(jnp.concatenate([ones, cos32], 1), (1, MLA_HEADS))
    sq = jnp.tile(jnp.concatenate([0.0 * ones, sin32], 1), (1, MLA_HEADS))
    pq = jnp.zeros((MLA_QK, MLA_QK), F32).at[MLA_NOPE:, MLA_NOPE:].set(p32)
    pq = jnp.kron(jnp.eye(MLA_HEADS, dtype=F32), pq)
    return dict(cos32=cos32, sin32=sin32, p32=p32.astype(BF), p32t=p32.T.astype(BF), cq=cq, sq=sq,
                pq=pq.astype(BF), pqt=pq.T.astype(BF))


def _cargo_specs(cargo):
    if cargo is None:
        return 0, [], [], [], []
    arrays, scatter = cargo
    n = len(arrays)
    hbm = pl.BlockSpec(memory_space=pltpu.HBM)
    out_shape = [SDS(t.shape if scatter else (N_DEV,) + t.shape, t.dtype) for t in arrays]
    scratch = [pltpu.SemaphoreType.DMA((n, N_DEV - 1)), pltpu.SemaphoreType.DMA((n, N_DEV - 1)),
               pltpu.SemaphoreType.DMA((n,))]
    return n, [hbm] * n, [hbm] * n, out_shape, scratch


def flash_fwd(tag, qh, kh, vh1, dv, tq=1024, tk=2048, cargo=None):
    H, S, dq = qh.shape
    dv1 = vh1.shape[2]
    tq, tk = min(tq, S), min(tk, S)
    nk = S // tk
    nc, c_in, c_out, c_shape, c_scratch = _cargo_specs(cargo)

    def body(*refs):
        q_ref, k_ref, v_ref = refs[:3]
        o_ref, lse_ref = refs[3 + nc:5 + nc]
        m_s, acc_s = refs[5 + 2 * nc:7 + 2 * nc]
        h, i, j = pl.program_id(0), pl.program_id(1), pl.program_id(2)
        if nc:
            start, wait = _exchange_ops(refs[3:3 + nc], refs[5 + nc:5 + 2 * nc], *refs[7 + 2 * nc:], cargo[1])
            pl.when((h == 0) & (i == 0) & (j == 0))(start)

        @pl.when(j == 0)
        def _():
            m_s[...] = jnp.full(m_s.shape, NEG, F32)
            acc_s[...] = jnp.zeros(acc_s.shape, F32)

        s = _dot(q_ref[0], k_ref[0], 1, 1)
        m_old = m_s[...]
        m_new = jnp.maximum(m_old, jnp.max(s, axis=-1, keepdims=True))
        p = jnp.exp2((s - m_new).astype(BF))
        acc_s[...] = jnp.exp2(m_old - m_new) * acc_s[...] + _dot(p, v_ref[0])
        m_s[...] = m_new

        @pl.when(j == nk - 1)
        def _():
            acc = acc_s[...]
            l = acc[:, dv:dv + 1]
            o_ref[0] = (acc[:, :dv] / l).astype(o_ref.dtype)
            lse_ref[0] = m_s[...] + jnp.log2(l)

        if nc:
            pl.when((h == H - 1) & (i == S // tq - 1) & (j == nk - 1))(wait)

    res = pl.pallas_call(
        body, grid=(H, S // tq, nk),
        in_specs=[pl.BlockSpec((1, tq, dq), lambda h, i, j: (h, i, 0)),
                  pl.BlockSpec((1, tk, dq), lambda h, i, j: (h, j, 0)),
                  pl.BlockSpec((1, tk, dv1), lambda h, i, j: (h, j, 0))] + c_in,
        out_specs=[pl.BlockSpec((1, tq, dv), lambda h, i, j: (h, i, 0)),
                   pl.BlockSpec((1, tq, 1), lambda h, i, j: (h, i, 0))] + c_out,
        out_shape=[SDS((H, S, dv), BF), SDS((H, S, 1), F32)] + c_shape,
        scratch_shapes=[pltpu.VMEM((tq, 1), F32), pltpu.VMEM((tq, dv1), F32)] + c_scratch,
        name=tag + "_flash", compiler_params=_params(("arbitrary", "arbitrary", "arbitrary")))(
            qh, kh, vh1, *(cargo[0] if nc else []))
    return res[0], res[1], list(res[2:])


def flash_bwd(tag, qh, kh, vh, doh, lse_row, delta_row, tq=1024, tk=1024, cargo=None):
    H, S, dq = qh.shape
    dv = vh.shape[2]
    tq, tk = min(tq, S), min(tk, S)
    nc, c_in, c_out, c_shape, c_scratch = _cargo_specs(cargo)

    def body(*refs):
        k_ref, v_ref, q_ref, do_ref, lse_ref, dl_ref = refs[:6]
        dk_ref, dv_ref, dq_ref = refs[6 + nc:9 + nc]
        h, j, i = pl.program_id(0), pl.program_id(1), pl.program_id(2)
        if nc:
            start, wait = _exchange_ops(refs[6:6 + nc], refs[9 + nc:9 + 2 * nc], *refs[9 + 2 * nc:], cargo[1])
            pl.when((h == 0) & (j == 0) & (i == 0))(start)
        k, v, q, do = k_ref[0], v_ref[0], q_ref[0], do_ref[0]
        pt = jnp.exp2((_dot(k, q, 1, 1) - lse_ref[0]).astype(BF))
        dst = pt * (_dot(v, do, 1, 1) - dl_ref[0]).astype(BF)
        dvc = _dot(pt, do)
        dkc = _dot(dst, q) * (1.0 / LOG2E)
        dqc = _dot(dst, k, 0, 0)

        @pl.when(i == 0)
        def _():
            dk_ref[0] = dkc
            dv_ref[0] = dvc

        @pl.when(i > 0)
        def _():
            dk_ref[0] += dkc
            dv_ref[0] += dvc

        rows = pl.ds(pl.multiple_of(i * tq, tq), tq)

        @pl.when(j == 0)
        def _():
            dq_ref[0, rows, :] = dqc

        @pl.when(j > 0)
        def _():
            dq_ref[0, rows, :] += dqc

        if nc:
            pl.when((h == H - 1) & (j == S // tk - 1) & (i == S // tq - 1))(wait)

    res = pl.pallas_call(
        body, grid=(H, S // tk, S // tq),
        in_specs=[pl.BlockSpec((1, tk, dq), lambda h, j, i: (h, j, 0)),
                  pl.BlockSpec((1, tk, dv), lambda h, j, i: (h, j, 0)),
                  pl.BlockSpec((1, tq, dq), lambda h, j, i: (h, i, 0)),
                  pl.BlockSpec((1, tq, dv), lambda h, j, i: (h, i, 0)),
                  pl.BlockSpec((1, 1, tq), lambda h, j, i: (h, 0, i)),
                  pl.BlockSpec((1, 1, tq), lambda h, j, i: (h, 0, i))] + c_in,
        out_specs=[pl.BlockSpec((1, tk, dq), lambda h, j, i: (h, j, 0)),
                   pl.BlockSpec((1, tk, dv), lambda h, j, i: (h, j, 0)),
                   pl.BlockSpec((1, S, dq), lambda h, j, i: (h, 0, 0))] + c_out,
        out_shape=[SDS((H, S, dq), F32), SDS((H, S, dv), F32), SDS((H, S, dq), F32)] + c_shape,
        scratch_shapes=c_scratch, name=tag + "_flashb",
        compiler_params=_params(("arbitrary", "arbitrary", "arbitrary")))(
            kh, vh, qh, doh, lse_row, delta_row, *(cargo[0] if nc else []))
    return res[0], res[1], res[2], list(res[3:])


def _to_heads(t, H):
    return t.reshape(t.shape[0], H, -1).transpose(1, 0, 2)


def _from_heads(t):
    return t.transpose(1, 0, 2).reshape(t.shape[1], -1)


def mla_fwd(tag, z, rope, q_g, kv_g, w_uq, w_ukv, cargo=None):
    S = z.shape[0]
    o1, o2, o3 = MLA_Q_LORA, MLA_Q_LORA + MLA_KV_LORA, MLA_Q_LORA + MLA_KV_LORA + MLA_ROPE
    c_q, c_kv, k_r, q_mem = z[:, :o1], z[:, o1:o2], z[:, o2:o3], z[:, o3:]

    def pre(i, t, f):
        kr = t[2].astype(F32)
        krot = kr * t[3] + _dot(t[2], f[2]) * t[4]
        return [_rms(t[0].astype(F32), f[0]), _rms(t[1].astype(F32), f[1]), krot], []

    qn, kvn, krot = rows_call(tag + "_pre", pre, S, 512, [c_q, c_kv, k_r, rope['cos32'], rope['sin32']],
                              [q_g, kv_g, rope['p32']], [(MLA_Q_LORA, BF), (MLA_KV_LORA, BF), (MLA_ROPE, BF)])

    qscale = MLA_QK ** -0.5 * LOG2E

    def ep_q(a, t, c, f):
        return [(a[0] * t[0] + _dot(a[0].astype(BF), f[0]) * t[1]) * qscale], []

    nq = MLA_HEADS * MLA_QK
    q, = mm_call(tag + "_uq", [[(qn, w_uq, False, False)]], S, nq, MLA_Q_LORA, tm=512, tn=nq, tk=MLA_Q_LORA,
                 epilogue=ep_q, outs=[BF], tiles=[rope['cq'], rope['sq']], fulls=[rope['pq']])
    nkv = MLA_HEADS * (MLA_NOPE + MLA_V)
    kv, = mm_call(tag + "_ukv", [[(kvn, w_ukv, False, False)]], S, nkv, MLA_KV_LORA, tm=1024, tn=nkv, tk=MLA_KV_LORA,
                  **_plain(BF))
    qh = _to_heads(q, MLA_HEADS)
    kv3 = kv.reshape(S, MLA_HEADS, MLA_NOPE + MLA_V)
    kh = jnp.concatenate([kv3[:, :, :MLA_NOPE].transpose(1, 0, 2),
                          jnp.broadcast_to(krot[None], (MLA_HEADS, S, MLA_ROPE))], axis=2)
    vh = kv3[:, :, MLA_NOPE:].transpose(1, 0, 2)
    ones = jnp.ones((MLA_HEADS, S, 1), BF)
    vh1 = jnp.concatenate([vh, ones, jnp.zeros((MLA_HEADS, S, 128 - MLA_V - 1), BF)], axis=2)
    oh, lse, got = flash_fwd(tag, qh, kh, vh1, MLA_V, cargo=cargo)
    sv = dict(c_q=c_q, c_kv=c_kv, k_r=k_r, qn=qn, kvn=kvn, qh=qh, kh=kh, vh=vh, oh=oh, lse=lse)
    return _from_heads(oh), q_mem, sv, got


def mla_bwd(tag, do, sv, rope, q_g, kv_g, w_uq, w_ukv, cargo=None):
    S = do.shape[0]
    H = MLA_HEADS
    doh = _to_heads(do, H)
    delta, = hrows_call(tag + "_delta",
                        lambda t: [jnp.sum(t[0].astype(F32) * t[1].astype(F32), axis=-1, keepdims=True)],
                        H, S, 1024, [doh, sv['oh']], [(1, F32)])
    dkh, dvh, dqh, got = flash_bwd(tag, sv['qh'], sv['kh'], sv['vh'], doh, sv['lse'].reshape(H, 1, S),
                                   delta.reshape(H, 1, S), cargo=cargo)
    nq = H * MLA_QK
    dq_rot = _from_heads(dqh)
    scale = MLA_QK ** -0.5

    def unrope(i, t, f):
        d = t[0] * scale
        return [d * t[1] + _dot((d * t[2]).astype(BF), f[0])], []

    dq, = rows_call(tag + "_unrope", unrope, S, 512, [dq_rot, rope['cq'], rope['sq']], [rope['pqt']], [(nq, BF)])

    def ep_norm(a, t, c, f):
        dx, dg = _rms_bwd(t[0].astype(F32), c[0], a[0])
        return [dx], [dg]

    dc_q, dq_g = mm_call(tag + "_bqn", [[(dq, w_uq, False, True)]], S, MLA_Q_LORA, nq, tm=1024, tn=MLA_Q_LORA, tk=nq,
                         epilogue=ep_norm, outs=[BF], tiles=[sv['c_q']], cols=[q_g], reds=[1])
    dw_uq, = mm_call(tag + "_dwuq", [[(sv['qn'], dq, True, False)]], MLA_Q_LORA, nq, S, tm=MLA_Q_LORA, tn=nq, tk=1024,
                     **_plain(F32))
    nkv = H * (MLA_NOPE + MLA_V)
    dkv = jnp.concatenate([dkh[:, :, :MLA_NOPE], dvh], axis=2).transpose(1, 0, 2).reshape(S, nkv).astype(BF)
    dc_kv, dkv_g = mm_call(tag + "_bkvn", [[(dkv, w_ukv, False, True)]], S, MLA_KV_LORA, nkv, tm=1024, tn=MLA_KV_LORA,
                           tk=nkv, epilogue=ep_norm, outs=[BF], tiles=[sv['c_kv']], cols=[kv_g], reds=[1])
    dw_ukv, = mm_call(tag + "_dwukv", [[(sv['kvn'], dkv, True, False)]], MLA_KV_LORA, nkv, S, tm=MLA_KV_LORA, tn=nkv,
                      tk=1024, **_plain(F32))
    dkr_heads = dkh[:, :, MLA_NOPE:].transpose(1, 0, 2).reshape(S, H * MLA_ROPE)
    hsum = jnp.tile(jnp.eye(MLA_ROPE, dtype=F32), (H, 1))

    def krope(i, t, f):
        d = jnp.dot(t[0], f[0], preferred_element_type=F32, precision=lax.Precision.HIGHEST)
        return [d * t[1] + _dot((d * t[2]).astype(BF), f[1])], []

    dk_r, = rows_call(tag + "_bkr", krope, S, 512, [dkr_heads, rope['cos32'], rope['sin32']], [hsum, rope['p32t']],
                      [(MLA_ROPE, BF)])
    grads = dict(a_q_norm=dq_g, a_kv_norm=dkv_g, a_w_uq=dw_uq, a_w_ukv=dw_ukv)
    return [dc_q, dc_kv, dk_r], grads, got


def _dil_perm(t, dil):
    H, S, d = t.shape
    return t.reshape(H, S // dil, dil, d).transpose(0, 2, 1, 3).reshape(H, S, d)


def _dil_unperm(t, dil):
    H, S, d = t.shape
    return t.reshape(H, dil, S // dil, d).transpose(0, 2, 1, 3).reshape(H, S, d)


def _band(rpos, cpos, S, seg_shift):
    dist = jnp.abs(rpos - cpos)
    valid = ((dist <= DIL_HALF) & (cpos >= 0) & (cpos < S) & (rpos >= 0) & (rpos < S)
             & (jnp.right_shift(rpos, seg_shift) == jnp.right_shift(cpos, seg_shift)))
    return dist, valid


def _halo_specs(blk, halo, n_halo_blocks, w, nh):
    r = blk // halo
    return [pl.BlockSpec((nh, halo, w), lambda h, i: (h, jnp.maximum(i * r - 1, 0), 0)),
            pl.BlockSpec((nh, blk, w), lambda h, i: (h, i, 0)),
            pl.BlockSpec((nh, halo, w), lambda h, i: (h, jnp.minimum((i + 1) * r, n_halo_blocks - 1), 0))]


def dil_fwd(tag, q, k, v, slopes, dil, tq=256):
    H, S, dh = q.shape
    nh = H
    tq = min(tq, S)
    seg_shift = int(math.log2(S // dil))
    hb = DIL_HALF
    scale = dh ** -0.5

    def body(sl_ref, q_ref, kp, kc, kn, vp, vc, vn, o_ref, lse_ref):
        i = pl.program_id(1)
        shp = (tq, tq + 2 * hb)
        qpos = i * tq + lax.broadcasted_iota(jnp.int32, shp, 0)
        kpos = i * tq - hb + lax.broadcasted_iota(jnp.int32, shp, 1)
        dist, valid = _band(qpos, kpos, S, seg_shift)
        tokens = (dist * dil).astype(F32)
        for h in range(nh):
            kcat = jnp.concatenate([kp[h], kc[h], kn[h]], axis=0)
            vcat = jnp.concatenate([vp[h], vc[h], vn[h]], axis=0)
            s = _dot(q_ref[h], kcat, 1, 1) * scale - sl_ref[h][:, :1] * tokens
            s = jnp.where(valid, s, NEG)
            m = jnp.max(s, axis=-1, keepdims=True)
            lse = m + jnp.log(jnp.sum(jnp.exp(s - m), axis=-1, keepdims=True))
            o_ref[h] = _dot(jnp.exp(s - lse).astype(BF), vcat)
            lse_ref[h] = lse

    return pl.pallas_call(
        body, grid=(H // nh, S // tq),
        in_specs=([pl.BlockSpec((nh, 1, 128), lambda h, i: (h, 0, 0)), pl.BlockSpec((nh, tq, dh), lambda h, i: (h, i, 0))]
                  + _halo_specs(tq, hb, S // hb, dh, nh) * 2),
        out_specs=[pl.BlockSpec((nh, tq, dh), lambda h, i: (h, i, 0)), pl.BlockSpec((nh, tq, 1), lambda h, i: (h, i, 0))],
        out_shape=[SDS((H, S, dh), F32), SDS((H, S, 1), F32)],
        name=tag + "_dil", compiler_params=_params(("arbitrary", "arbitrary")))(slopes, q, k, k, k, v, v, v)


def dil_bwd_q(tag, q, k, v, do, lse, deff, slopes, dil, tq=256):
    H, S, dh = q.shape
    nh = H
    tq = min(tq, S)
    seg_shift = int(math.log2(S // dil))
    hb = DIL_HALF
    scale = dh ** -0.5

    def body(sl_ref, q_ref, do_ref, lse_ref, de_ref, kp, kc, kn, vp, vc, vn, dq_ref):
        i = pl.program_id(1)
        shp = (tq, tq + 2 * hb)
        qpos = i * tq + lax.broadcasted_iota(jnp.int32, shp, 0)
        kpos = i * tq - hb + lax.broadcasted_iota(jnp.int32, shp, 1)
        dist, valid = _band(qpos, kpos, S, seg_shift)
        tokens = (dist * dil).astype(F32)
        for h in range(nh):
            kcat = jnp.concatenate([kp[h], kc[h], kn[h]], axis=0)
            vcat = jnp.concatenate([vp[h], vc[h], vn[h]], axis=0)
            s = _dot(q_ref[h], kcat, 1, 1) * scale - sl_ref[h][:, :1] * tokens
            p = jnp.where(valid, jnp.exp(jnp.where(valid, s, NEG) - lse_ref[h]), 0.0)
            dp = _dot(do_ref[h], vcat, 1, 1)
            ds = (p * (dp - de_ref[h]) * scale).astype(BF)
            dq_ref[h] = _dot(ds, kcat)

    row = lambda w: pl.BlockSpec((nh, tq, w), lambda h, i: (h, i, 0))
    return pl.pallas_call(
        body, grid=(H // nh, S // tq),
        in_specs=([pl.BlockSpec((nh, 1, 128), lambda h, i: (h, 0, 0)), row(dh), row(dh), row(1), row(1)]
                  + _halo_specs(tq, hb, S // hb, dh, nh) * 2),
        out_specs=row(dh), out_shape=SDS((H, S, dh), F32),
        name=tag + "_dilbq",
        compiler_params=_params(("arbitrary", "arbitrary")))(slopes, q, do, lse, deff, k, k, k, v, v, v)


def dil_bwd_kv(tag, q, k, v, do, lse_row, deff_row, slopes, dil, tk=256):
    H, S, dh = q.shape
    nh = H
    tk = min(tk, S)
    seg_shift = int(math.log2(S // dil))
    hq = 128
    scale = dh ** -0.5
    r = tk // hq
    nhb = S // hq
    stat_specs = [pl.BlockSpec((nh, 1, hq), lambda h, j: (h, 0, jnp.maximum(j * r - 1, 0))),
                  pl.BlockSpec((nh, 1, tk), lambda h, j: (h, 0, j)),
                  pl.BlockSpec((nh, 1, hq), lambda h, j: (h, 0, jnp.minimum((j + 1) * r, nhb - 1)))]

    def body(sl_ref, k_ref, v_ref, qp, qc, qn, dop, doc, don, lp, lc, ln, ep, ec, en, dk_ref, dv_ref):
        j = pl.program_id(1)
        shp = (tk, tk + 2 * hq)
        kpos = j * tk + lax.broadcasted_iota(jnp.int32, shp, 0)
        qpos = j * tk - hq + lax.broadcasted_iota(jnp.int32, shp, 1)
        dist, valid = _band(kpos, qpos, S, seg_shift)
        tokens = (dist * dil).astype(F32)
        for h in range(nh):
            qcat = jnp.concatenate([qp[h], qc[h], qn[h]], axis=0)
            docat = jnp.concatenate([dop[h], doc[h], don[h]], axis=0)
            lse = jnp.concatenate([lp[h], lc[h], ln[h]], axis=1)
            deff = jnp.concatenate([ep[h], ec[h], en[h]], axis=1)
            st = _dot(k_ref[h], qcat, 1, 1) * scale - sl_ref[h][:, :1] * tokens
            pt = jnp.where(valid, jnp.exp(jnp.where(valid, st, NEG) - jnp.where(valid, lse, 0.0)), 0.0)
            dpt = _dot(v_ref[h], docat, 1, 1)
            dst = (pt * (dpt - deff) * scale).astype(BF)
            dv_ref[h] = _dot(pt.astype(BF), docat)
            dk_ref[h] = _dot(dst, qcat)

    cur = lambda w: pl.BlockSpec((nh, tk, w), lambda h, j: (h, j, 0))
    return pl.pallas_call(
        body, grid=(H // nh, S // tk),
        in_specs=([pl.BlockSpec((nh, 1, 128), lambda h, j: (h, 0, 0)), cur(dh), cur(dh)]
                  + _halo_specs(tk, hq, nhb, dh, nh) * 2 + stat_specs * 2),
        out_specs=[cur(dh), cur(dh)], out_shape=[SDS((H, S, dh), F32), SDS((H, S, dh), F32)],
        name=tag + "_dilbkv", compiler_params=_params(("arbitrary", "arbitrary")))(
            slopes, k, v, q, q, q, do, do, do, lse_row, lse_row, lse_row, deff_row, deff_row, deff_row)


def _alibi_slopes():
    n = len(DIL_GROUPS) * DIL_HEADS
    s = 2.0 ** (-ALIBI_MAX * (jnp.arange(n, dtype=F32) + 1.0) / n)
    return jnp.broadcast_to(s.reshape(len(DIL_GROUPS), DIL_HEADS, 1, 1), (len(DIL_GROUPS), DIL_HEADS, 1, 128))


def _merge_weights(lses):
    m = jnp.maximum(jnp.maximum(lses[0], lses[1]), lses[2])
    es = [jnp.exp(l - m) for l in lses]
    tot = es[0] + es[1] + es[2]
    return [e / tot for e in es]


def dilated_fwd(tag, z):
    S = z.shape[0]
    H = DIL_HEADS
    ng = len(DIL_GROUPS)
    zd = z[:, :DIL_QKV].reshape(S, ng, 3, H, HEAD_DIM)
    q_mem = z[:, DIL_QKV:]
    slopes = _alibi_slopes()
    qkv, outs, lses = [], [], []
    for g, (_, dil) in enumerate(DIL_GROUPS):
        t = [_dil_perm(zd[:, g, c].transpose(1, 0, 2), dil) for c in range(3)]
        o, lse = dil_fwd(f"{tag}_g{g}", t[0], t[1], t[2], slopes[g], dil)
        qkv.append(t)
        outs.append(_dil_unperm(o, dil))
        lses.append(_dil_unperm(lse, dil))

    def merge(t):
        w = _merge_weights(t[3:6])
        return [w[0] * t[0] + w[1] * t[1] + w[2] * t[2]]

    oh, = hrows_call(tag + "_merge", merge, H, S, 1024, outs + lses, [(HEAD_DIM, BF)])
    return _from_heads(oh), q_mem, dict(qkv=qkv, outs=outs, lses=lses)


def dilated_bwd(tag, do, sv):
    S = do.shape[0]
    H = DIL_HEADS
    slopes = _alibi_slopes()
    doh = _to_heads(do, H)

    def merge_b(t):
        d = t[0].astype(F32)
        w = _merge_weights(t[4:7])
        dws = [jnp.sum(d * t[1 + g], axis=-1, keepdims=True) for g in range(3)]
        c = w[0] * dws[0] + w[1] * dws[1] + w[2] * dws[2]
        return [w[g] * d for g in range(3)] + [w[g] * c for g in range(3)]

    res = hrows_call(tag + "_mergeb", merge_b, H, S, 1024, [doh] + sv['outs'] + sv['lses'],
                     [(HEAD_DIM, BF)] * 3 + [(1, F32)] * 3)
    parts = []
    for g, (_, dil) in enumerate(DIL_GROUPS):
        q, k, v = sv['qkv'][g]
        do_g = _dil_perm(res[g], dil)
        deff = _dil_perm(res[3 + g], dil)
        lse = _dil_perm(sv['lses'][g], dil)
        dq = dil_bwd_q(f"{tag}_g{g}", q, k, v, do_g, lse, deff, slopes[g], dil)
        dk, dv = dil_bwd_kv(f"{tag}_g{g}", q, k, v, do_g, lse.reshape(H, 1, S), deff.reshape(H, 1, S), slopes[g], dil)
        parts.append(jnp.stack([_dil_unperm(t, dil).transpose(1, 0, 2) for t in (dq, dk, dv)], axis=1))
    dzd = jnp.stack(parts, axis=1)
    return [dzd.reshape(S, DIL_QKV).astype(BF)]


def _halo_specs2(ts, w, S):
    r = ts // HALO
    return [pl.BlockSpec((HALO, w), lambda i: (jnp.maximum(i * r - 1, 0), 0)),
            pl.BlockSpec((ts, w), lambda i: (i, 0)),
            pl.BlockSpec((HALO, w), lambda i: (jnp.minimum((i + 1) * r, S // HALO - 1), 0))]


def _cat_rows(i, ts, S, refs):
    v = jnp.concatenate([r[...].astype(F32) for r in refs], axis=0)
    pos = i * ts - HALO + lax.broadcasted_iota(jnp.int32, (ts + 2 * HALO, 1), 0)
    return v, (pos >= 0) & (pos < S)


def _conv_taps(buf_ref, w, ts, width, flip):
    acc = None
    for k in range(width):
        off = HALO + ((width // 2 - k) if flip else (k - width // 2))
        term = buf_ref[pl.ds(off, ts), :] * w[k:k + 1, :]
        acc = term if acc is None else acc + term
    return acc


def conformer_fwd(tag, z, conv_w, conv_b, ln_g, ln_b, ts=256):
    S = z.shape[0]
    ts = min(ts, S)
    C = CONV_CH
    a, gate, q_mem = z[:, :C], z[:, C:2 * C], z[:, 2 * C:]

    def body(ap, ac, an, gp, gc, gn, w_ref, b_ref, lg_ref, lb_ref, u_ref, u1_ref, buf):
        i = pl.program_id(0)
        av, ok = _cat_rows(i, ts, S, (ap, ac, an))
        gv, _ = _cat_rows(i, ts, S, (gp, gc, gn))
        buf[...] = jnp.where(ok, av * jax.nn.sigmoid(gv), 0.0)
        u1 = _conv_taps(buf, w_ref[...], ts, CONV_WIDTH, False) + b_ref[...]
        mu = jnp.mean(u1, axis=-1, keepdims=True)
        xc = u1 - mu
        y = xc * lax.rsqrt(jnp.mean(xc * xc, axis=-1, keepdims=True) + EPS) * lg_ref[...] + lb_ref[...]
        u_ref[...] = _silu(y).astype(u_ref.dtype)
        u1_ref[...] = u1

    full = lambda t: pl.BlockSpec(t.shape, lambda i: (0, 0))
    u, u1 = pl.pallas_call(
        body, grid=(S // ts,),
        in_specs=_halo_specs2(ts, C, S) + _halo_specs2(ts, C, S) + [full(conv_w), full(conv_b), full(ln_g), full(ln_b)],
        out_specs=[pl.BlockSpec((ts, C), lambda i: (i, 0))] * 2, out_shape=[SDS((S, C), BF), SDS((S, C), F32)],
        scratch_shapes=[pltpu.VMEM((ts + 2 * HALO, C), F32)], name=tag + "_conf",
        compiler_params=_params(("arbitrary",)))(a, a, a, gate, gate, gate, conv_w, conv_b, ln_g, ln_b)
    return u, q_mem, dict(a=a, gate=gate, u1=u1)


def conformer_bwd(tag, du, sv, conv_w, conv_b, ln_g, ln_b, ts=256):
    S = du.shape[0]
    ts = min(ts, S)
    C = CONV_CH

    def ln_b_fn(i, t, f):
        u1, d = t[1], t[0].astype(F32)
        mu = jnp.mean(u1, axis=-1, keepdims=True)
        xc = u1 - mu
        r = lax.rsqrt(jnp.mean(xc * xc, axis=-1, keepdims=True) + EPS)
        xh = xc * r
        dy = d * _dsilu(xh * f[0] + f[1])
        dxh = dy * f[0]
        du1 = r * (dxh - jnp.mean(dxh, axis=-1, keepdims=True) - xh * jnp.mean(dxh * xh, axis=-1, keepdims=True))
        return [du1], [jnp.sum(dy * xh, axis=0, keepdims=True), jnp.sum(dy, axis=0, keepdims=True),
                       jnp.sum(du1, axis=0, keepdims=True)]

    du1, dln_g, dln_b, dconv_b = rows_call(tag + "_confb1", ln_b_fn, S, 512, [du, sv['u1']], [ln_g, ln_b],
                                           [(C, F32)], [(1, C), (1, C), (1, C)])
    wpad = CONV_WIDTH + 1

    def body(dp, dc, dn, ap, ac, an, gp, gc, gn, w_ref, da_ref, dg_ref, dw_ref, ubuf, dbuf):
        i = pl.program_id(0)
        av, ok = _cat_rows(i, ts, S, (ap, ac, an))
        gv, _ = _cat_rows(i, ts, S, (gp, gc, gn))
        dv, _ = _cat_rows(i, ts, S, (dp, dc, dn))
        sg = jax.nn.sigmoid(gv)
        ubuf[...] = jnp.where(ok, av * sg, 0.0)
        dbuf[...] = jnp.where(ok, dv, 0.0)
        du0 = _conv_taps(dbuf, w_ref[...], ts, CONV_WIDTH, True)
        a_c, s_c = av[HALO:HALO + ts], sg[HALO:HALO + ts]
        da_ref[...] = (du0 * s_c).astype(da_ref.dtype)
        dg_ref[...] = (du0 * a_c * s_c * (1.0 - s_c)).astype(dg_ref.dtype)
        d_c = dbuf[pl.ds(HALO, ts), :]

        @pl.when(i == 0)
        def _():
            dw_ref[...] = jnp.zeros(dw_ref.shape, F32)

        for k in range(CONV_WIDTH):
            off = HALO + k - CONV_WIDTH // 2
            dw_ref[k:k + 1, :] += jnp.sum(d_c * ubuf[pl.ds(off, ts), :], axis=0, keepdims=True)

    full = lambda t: pl.BlockSpec(t.shape, lambda i: (0, 0))
    da, dgate, dw = pl.pallas_call(
        body, grid=(S // ts,),
        in_specs=_halo_specs2(ts, C, S) * 3 + [full(conv_w)],
        out_specs=[pl.BlockSpec((ts, C), lambda i: (i, 0))] * 2 + [pl.BlockSpec((wpad, C), lambda i: (0, 0))],
        out_shape=[SDS((S, C), BF), SDS((S, C), BF), SDS((wpad, C), F32)],
        scratch_shapes=[pltpu.VMEM((ts + 2 * HALO, C), F32)] * 2, name=tag + "_confb2",
        compiler_params=_params(("arbitrary",)))(du1, du1, du1, sv['a'], sv['a'], sv['a'], sv['gate'], sv['gate'],
                                                 sv['gate'], conv_w)
    grads = dict(c_conv_w=dw[:CONV_WIDTH], c_conv_b=dconv_b, c_ln_g=dln_g, c_ln_b=dln_b)
    return [da, dgate], grads


def shortconv_fwd(tag, z, conv_w, ts=256):
    S = z.shape[0]
    ts = min(ts, S)
    C = SC_CH
    bg, cg, hx, q_mem = z[:, :C], z[:, C:2 * C], z[:, 2 * C:3 * C], z[:, 3 * C:]

    def body(b_ref, cp, cc, cn, hp, hc, hn, w_ref, o_ref, buf):
        i = pl.program_id(0)
        cv, ok = _cat_rows(i, ts, S, (cp, cc, cn))
        hv, _ = _cat_rows(i, ts, S, (hp, hc, hn))
        buf[...] = jnp.where(ok, cv * hv, 0.0)
        o_ref[...] = (b_ref[...].astype(F32) * _conv_taps(buf, w_ref[...], ts, SC_WIDTH, False)).astype(o_ref.dtype)

    o = pl.pallas_call(
        body, grid=(S // ts,),
        in_specs=[pl.BlockSpec((ts, C), lambda i: (i, 0))] + _halo_specs2(ts, C, S) * 2
        + [pl.BlockSpec(conv_w.shape, lambda i: (0, 0))],
        out_specs=pl.BlockSpec((ts, C), lambda i: (i, 0)), out_shape=SDS((S, C), BF),
        scratch_shapes=[pltpu.VMEM((ts + 2 * HALO, C), F32)], name=tag + "_sconv",
        compiler_params=_params(("arbitrary",)))(bg, cg, cg, cg, hx, hx, hx, conv_w)
    return o, q_mem, dict(bg=bg, cg=cg, hx=hx)


def shortconv_bwd(tag, do, sv, conv_w, ts=256):
    S = do.shape[0]
    ts = min(ts, S)
    C = SC_CH
    wpad = 8

    def body(dp, dc, dn, bp, bc, bn, cp, cc, cn, hp, hc, hn, w_ref, db_ref, dcg_ref, dhx_ref, dw_ref, mbuf, dbuf):
        i = pl.program_id(0)
        dov, ok = _cat_rows(i, ts, S, (dp, dc, dn))
        bv, _ = _cat_rows(i, ts, S, (bp, bc, bn))
        cv, _ = _cat_rows(i, ts, S, (cp, cc, cn))
        hv, _ = _cat_rows(i, ts, S, (hp, hc, hn))
        mbuf[...] = jnp.where(ok, cv * hv, 0.0)
        dbuf[...] = jnp.where(ok, dov * bv, 0.0)
        conv = _conv_taps(mbuf, w_ref[...], ts, SC_WIDTH, False)
        db_ref[...] = (dov[HALO:HALO + ts] * conv).astype(db_ref.dtype)
        dm = _conv_taps(dbuf, w_ref[...], ts, SC_WIDTH, True)
        dcg_ref[...] = (dm * hv[HALO:HALO + ts]).astype(dcg_ref.dtype)
        dhx_ref[...] = (dm * cv[HALO:HALO + ts]).astype(dhx_ref.dtype)
        d_c = dbuf[pl.ds(HALO, ts), :]

        @pl.when(i == 0)
        def _():
            dw_ref[...] = jnp.zeros(dw_ref.shape, F32)

        for k in range(SC_WIDTH):
            off = HALO + k - SC_WIDTH // 2
            dw_ref[k:k + 1, :] += jnp.sum(d_c * mbuf[pl.ds(off, ts), :], axis=0, keepdims=True)

    dbg, dcg, dhx, dw = pl.pallas_call(
        body, grid=(S // ts,),
        in_specs=_halo_specs2(ts, C, S) * 4 + [pl.BlockSpec(conv_w.shape, lambda i: (0, 0))],
        out_specs=[pl.BlockSpec((ts, C), lambda i: (i, 0))] * 3 + [pl.BlockSpec((wpad, C), lambda i: (0, 0))],
        out_shape=[SDS((S, C), BF)] * 3 + [SDS((wpad, C), F32)],
        scratch_shapes=[pltpu.VMEM((ts + 2 * HALO, C), F32)] * 2, name=tag + "_sconvb",
        compiler_params=_params(("arbitrary",)))(do, do, do, sv['bg'], sv['bg'], sv['bg'], sv['cg'], sv['cg'],
                                                 sv['cg'], sv['hx'], sv['hx'], sv['hx'], conv_w)
    return [dbg, dcg, dhx], dict(d_conv_w=dw[:SC_WIDTH])


MIXERS = ('a', 'b', 'c', 'd')


def _mixer_weights(W, m):
    return W[m + '_w_in'][0], W[m + '_w_out'][0]


def layer_fwd(i, x, mem, rope, W, gather=None):
    tag = f"l{i}"
    g = W['norm_g'][i]
    gr = lambda k: g[k:k + 1]
    m = MIXERS[i % len(MIXERS)]
    sv = {}
    x1, sv['ffn0'] = ffn_fwd(tag + "_f0", x, gr(0), gr(1), W['ffn_w_gate'][i, 0], W['ffn_w_up'][i, 0],
                             W['ffn_w_down'][i, 0])
    w_in = W[m + '_w_in'][0]
    S = x.shape[0]
    h2 = rms_rows(tag + "_rms2", x1, gr(2))
    n_in = w_in.shape[1]
    z, = mm_call(tag + "_in", [[(h2, w_in, False, False)]], S, n_in, D_MODEL, tm=1024,
                 tn=(256 if n_in % 256 == 0 else n_in), tk=1024, **_plain(BF))
    if m == 'a':
        o, q_mem, sv['mix'], got = mla_fwd(tag, z, rope, W['a_q_norm'], W['a_kv_norm'], W['a_w_uq'][0],
                                           W['a_w_ukv'][0], cargo=(gather[0], False) if gather else None)
        if gather:
            W = gather[1](got)
    elif m == 'b':
        o, q_mem, sv['mix'] = dilated_fwd(tag, z)
    elif m == 'c':
        o, q_mem, sv['mix'] = conformer_fwd(tag, z, W['c_conv_w'][0], W['c_conv_b'], W['c_ln_g'], W['c_ln_b'])
    else:
        o, q_mem, sv['mix'] = shortconv_fwd(tag, z, W['d_conv_w'][0])
    w_out = W[m + '_w_out'][0]
    kv = mem_kv_fwd(tag, mem, gr(6), W['mem_w_kv'][i])
    kbd, vbd = _block_diag_kv(kv)
    mo = mem_attn_fwd(tag, q_mem, kbd, vbd)
    cat = jnp.concatenate([o, mo], axis=1)
    n_out = cat.shape[1]

    def ep_out(a, t, c, f):
        return [a[0], t[0] + _rms(a[0], c[0])], []

    y2, x2 = mm_call(tag + "_out", [[(cat, w_out, False, False)]], S, D_MODEL, n_out, tm=512, tn=1024, tk=n_out,
                     epilogue=ep_out, outs=[BF, F32], tiles=[x1], cols=[gr(3)])
    sv.update(x1=x1, h2=h2, q_mem=q_mem, kbd=kbd, vbd=vbd, cat=cat, y2=y2)
    x3, sv['ffn1'] = ffn_fwd(tag + "_f1", x2, gr(4), gr(5), W['ffn_w_gate'][i, 1], W['ffn_w_up'][i, 1],
                             W['ffn_w_down'][i, 1])
    return x3, sv, W


def layer_bwd(i, dx3, sv, mem, rope, W, scatter=None):
    tag = f"l{i}"
    g = W['norm_g'][i]
    gr = lambda k: g[k:k + 1]
    m = MIXERS[i % len(MIXERS)]
    S = dx3.shape[0]
    dx2, f1 = ffn_bwd(tag + "_f1", dx3, sv['ffn1'], gr(4), gr(5), W['ffn_w_gate'][i, 1], W['ffn_w_up'][i, 1],
                      W['ffn_w_down'][i, 1])
    w_in, w_out = W[m + '_w_in'][0], W[m + '_w_out'][0]
    n_in, n_out = w_in.shape[1], w_out.shape[0]
    got = None

    def post(ii, t, f):
        dx, dg = _rms_bwd(t[1].astype(F32), f[0], t[0])
        return [dx], [dg]

    dy2, dg3 = rows_call(tag + "_bpost", post, S, 512, [dx2, sv['y2']], [gr(3)], [(D_MODEL, BF)], [(1, D_MODEL)])
    dcat, = mm_call(tag + "_bout", [[(dy2, w_out, False, True)]], S, n_out, D_MODEL, tm=1024,
                    tn=(256 if n_out % 256 == 0 else n_out), tk=1024, **_plain(BF))
    dw_out, = mm_call(tag + "_dwout", [[(sv['cat'], dy2, True, False)]], n_out, D_MODEL, S, tm=n_out, tn=1024, tk=512,
                      **_plain(F32))
    n_o = n_out - MEM_WIDTH
    do, dmo = dcat[:, :n_o], dcat[:, n_o:]
    dq_mem, dkv = mem_attn_bwd(tag, sv['q_mem'], dmo, sv['kbd'], sv['vbd'])
    dw_kv, dg6 = mem_kv_bwd(tag, mem, gr(6), W['mem_w_kv'][i], dkv)
    if m == 'a':
        cargo = (scatter(dict(ffn1=f1, a_w_out=dw_out[None], mem_w_kv=dw_kv)), True) if scatter else None
        dzs, mg, got = mla_bwd(tag, do, sv['mix'], rope, W['a_q_norm'], W['a_kv_norm'], W['a_w_uq'][0],
                               W['a_w_ukv'][0], cargo=cargo)
    elif m == 'b':
        dzs, mg = dilated_bwd(tag, do, sv['mix']), {}
    elif m == 'c':
        dzs, mg = conformer_bwd(tag, do, sv['mix'], W['c_conv_w'][0], W['c_conv_b'], W['c_ln_g'], W['c_ln_b'])
    else:
        dzs, mg = shortconv_bwd(tag, do, sv['mix'], W['d_conv_w'][0])
    dz = jnp.concatenate(dzs + [dq_mem], axis=1)
    dw_in, = mm_call(tag + "_dwin", [[(sv['h2'], dz, True, False)]], D_MODEL, n_in, S, tm=1024,
                     tn=(256 if n_in % 256 == 0 else n_in), tk=512, **_plain(F32))

    def ep_in(a, t, c, f):
        dx, dg = _rms_bwd(t[0], c[0], a[0])
        return [t[1] + dx], [dg]

    dx1, dg2 = mm_call(tag + "_bin", [[(dz, w_in, False, True)]], S, D_MODEL, n_in, tm=512, tn=1024,
                       tk=(n_in // 2 if n_in % 256 == 0 else n_in), epilogue=ep_in, outs=[F32],
                       tiles=[sv['x1'], dx2], cols=[gr(2)], reds=[1])
    dx0, f0 = ffn_bwd(tag + "_f0", dx1, sv['ffn0'], gr(0), gr(1), W['ffn_w_gate'][i, 0], W['ffn_w_up'][i, 0],
                      W['ffn_w_down'][i, 0])
    dg = jnp.concatenate([f0['g_in'], f0['g_out'], dg2, dg3, f1['g_in'], f1['g_out'], dg6], axis=0)
    grads = dict(norm_g=dg, ffn=[f0, f1], mem_w_kv=dw_kv)
    grads[m + '_w_in'] = dw_in[None]
    grads[m + '_w_out'] = dw_out[None]
    for k, v in mg.items():
        grads[k] = v if k in ('a_q_norm', 'a_kv_norm', 'c_conv_b', 'c_ln_g', 'c_ln_b') else v[None]
    return dx0, grads, got


def local_step(x, mem, positions, target, W, gather=None, scatter=None):
    S = x.shape[0]
    rope = _rope_tables(positions)
    saved = []
    for i in range(DEPTH):
        x, sv, W = layer_fwd(i, x, mem, rope, W, gather if i == 0 else None)
        saved.append(sv)

    def loss_fn(i, t, f):
        e = t[0] - t[1]
        part = jnp.sum(0.5 * jnp.mean(e * e, axis=-1, keepdims=True), axis=0, keepdims=True)
        return [e * (1.0 / D_MODEL)], [jnp.broadcast_to(part, (1, 128))]

    dx, loss = rows_call("loss", loss_fn, S, 512, [x, target], [], [(D_MODEL, F32)], [(1, 128)])
    upper, got = [], None
    for i in reversed(range(DEPTH)):
        hook = (lambda early: scatter(upper, early)) if (scatter and i == 0) else None
        dx, gr, got = layer_bwd(i, dx, saved[i], mem, rope, W, hook)
        upper.insert(0, gr)
    return loss[0, 0], dx, upper, got


def stack_grads(per_layer, names=None):
    grads = {}
    want = lambda n: names is None or n in names
    if want('norm_g'):
        grads['norm_g'] = jnp.stack([p['norm_g'] for p in per_layer])
    if want('mem_w_kv'):
        grads['mem_w_kv'] = jnp.stack([p['mem_w_kv'] for p in per_layer])
    for n, k in (('ffn_w_gate', 'wg'), ('ffn_w_up', 'wu'), ('ffn_w_down', 'wd')):
        if want(n):
            grads[n] = jnp.stack([jnp.stack([f[k] for f in p['ffn']]) for p in per_layer])
    for p in per_layer:
        for n, v in p.items():
            if n not in ('norm_g', 'mem_w_kv', 'ffn') and want(n):
                grads[n] = v
    return grads


def _local_shape(full_shape, axis):
    s = list(full_shape)
    if axis is not None:
        s[axis] //= N_DEV
    return tuple(s)


def _pad_to(n, m):
    return -(-n // m) * m


def pack_local(arrs, dtype, row_mult):
    flat = jnp.concatenate([a.reshape(-1).astype(dtype) for a in arrs])
    n = _pad_to(flat.shape[0], PACK_W * row_mult)
    return jnp.pad(flat, (0, n - flat.shape[0])).reshape(n // PACK_W, PACK_W)


def unpack_local(buf, names, shapes):
    flat = buf.reshape(-1)
    out, off = {}, 0
    for n in names:
        sz = math.prod(shapes[n])
        out[n] = flat[off:off + sz].reshape(shapes[n])
        off += sz
    return out


def unpack_gathered(buf, names, shapes):
    flat = buf.reshape(N_DEV, -1)
    out, off = {}, 0
    for n in names:
        ax = SHARD_AXIS[n]
        sz = math.prod(shapes[n])
        t = flat[:, off:off + sz].reshape((N_DEV,) + tuple(shapes[n]))
        t = jnp.moveaxis(t, 0, ax)
        full = list(shapes[n])
        full[ax] *= N_DEV
        out[n] = t.reshape(full)
        off += sz
    return out


def pack_grads(grads, names, row_mult):
    parts = []
    for n in names:
        g = grads[n].astype(F32)
        ax = SHARD_AXIS[n]
        if ax is None:
            parts.append(jnp.broadcast_to(g.reshape(1, -1), (N_DEV, g.size)))
        else:
            shp = list(g.shape)
            t = g.reshape(shp[:ax] + [N_DEV, shp[ax] // N_DEV] + shp[ax + 1:])
            parts.append(jnp.moveaxis(t, ax, 0).reshape(N_DEV, -1))
    flat = jnp.concatenate(parts, axis=1)
    n = _pad_to(flat.shape[1], PACK_W * row_mult)
    return jnp.pad(flat, ((0, 0), (0, n - flat.shape[1]))).reshape(N_DEV, n // PACK_W, PACK_W)


COL_SHARDED = ['a_w_uq', 'a_w_ukv', 'b_w_in', 'b_w_out', 'c_w_in', 'd_w_in']
ROW_SHARDED_1024 = ['a_w_out', 'c_w_out', 'd_w_out']
FLAT = SMALL + ['a_q_norm', 'a_kv_norm']


EDGE_COLS = ['a_w_uq', 'a_w_ukv']
BULK_COLS = ['b_w_in', 'b_w_out', 'c_w_in', 'd_w_in']
N_FFN = DEPTH * 2


def _pad_cols(t):
    return jnp.pad(t, ((0, 0), (0, FF_SLOT - FF_SHARD)))


def wire_weights_edge(a):
    ws = [_pad_cols(a['ffn_w_gate'][0, 0]), _pad_cols(a['ffn_w_up'][0, 0]), a['ffn_w_down'][0, 0], a['a_w_in'][0]]
    return [t.astype(BF) for t in ws + [a[n][0] for n in EDGE_COLS]]


def wire_weights_bulk(a):
    rest = lambda t, r: t.reshape(N_FFN * r, t.shape[-1])[r:]
    ws = [_pad_cols(rest(a['ffn_w_gate'], D_MODEL)), _pad_cols(rest(a['ffn_w_up'], D_MODEL)),
          jnp.concatenate([rest(a['ffn_w_down'], FF_SHARD)] + [a[n][0] for n in ROW_SHARDED_1024], axis=0),
          a['mem_w_kv'].reshape(-1, 2 * MEM_WIDTH)]
    return [t.astype(BF) for t in ws + [a[n][0] for n in BULK_COLS]]


def _unslot_cols(t):
    return t.transpose(1, 0, 2).reshape(t.shape[1], N_DEV * t.shape[2])


def _unslot_down(t, n):
    t = t.reshape(N_DEV, n, FF_SHARD, D_MODEL)
    t = jnp.pad(t, ((0, 0), (0, 0), (0, FF_SLOT - FF_SHARD), (0, 0)))
    return t.transpose(1, 0, 2, 3).reshape(n, D_FFP, D_MODEL)


def unwire_edge(g):
    W = dict(ffn_w_gate=_unslot_cols(g[0]).reshape(1, 1, D_MODEL, D_FFP),
             ffn_w_up=_unslot_cols(g[1]).reshape(1, 1, D_MODEL, D_FFP),
             ffn_w_down=_unslot_down(g[2], 1).reshape(1, 1, D_FFP, D_MODEL), a_w_in=g[3].reshape(1, D_MODEL, -1))
    for k, n in enumerate(EDGE_COLS):
        W[n] = _unslot_cols(g[4 + k])[None]
    return W


def unwire_bulk(g, edge):
    W = dict(edge)
    for k, n in enumerate(('ffn_w_gate', 'ffn_w_up')):
        t = _unslot_cols(g[k]).reshape(N_FFN - 1, D_MODEL, D_FFP)
        W[n] = jnp.concatenate([edge[n][0], t], axis=0).reshape(DEPTH, 2, D_MODEL, D_FFP)
    n_dn = (N_FFN - 1) * FF_SHARD
    dn = _unslot_down(g[2][:, :n_dn], N_FFN - 1)
    W['ffn_w_down'] = jnp.concatenate([edge['ffn_w_down'][0], dn], axis=0).reshape(DEPTH, 2, D_FFP, D_MODEL)
    rows = D_MODEL // N_DEV
    for k, n in enumerate(ROW_SHARDED_1024):
        W[n] = g[2][:, n_dn + k * rows:n_dn + (k + 1) * rows].reshape(1, D_MODEL, D_MODEL)
    W['mem_w_kv'] = g[3].reshape(N_DEV, DEPTH, rows, 2 * MEM_WIDTH).transpose(1, 0, 2, 3).reshape(DEPTH, D_MODEL, -1)
    for k, n in enumerate(BULK_COLS):
        W[n] = _unslot_cols(g[4 + k])[None]
    return W


def _slot_cols(t):
    return t.reshape(t.shape[0], N_DEV, -1).transpose(1, 0, 2)


def _slot_down(t):
    return t.reshape(N_DEV, FF_SLOT, D_MODEL)[:, :FF_SHARD]


def wire_grads_edge(ffn0, g):
    ws = [_slot_cols(ffn0['wg']), _slot_cols(ffn0['wu']), _slot_down(ffn0['wd']),
          g['a_w_in'][0].reshape(N_DEV, D_MODEL // N_DEV, -1)] + [_slot_cols(g[n][0]) for n in EDGE_COLS]
    return [t.astype(BF) for t in ws]


def wire_grads_bulk(ffns, g, mem_w_kv):
    rows = D_MODEL // N_DEV
    kv = mem_w_kv.reshape(DEPTH, N_DEV, rows, 2 * MEM_WIDTH).transpose(1, 0, 2, 3).reshape(N_DEV, DEPTH * rows, -1)
    ws = [jnp.concatenate([_slot_cols(f['wg']) for f in ffns], axis=1),
          jnp.concatenate([_slot_cols(f['wu']) for f in ffns], axis=1),
          jnp.concatenate([_slot_down(f['wd']) for f in ffns]
                          + [g[n][0].reshape(N_DEV, rows, D_MODEL) for n in ROW_SHARDED_1024], axis=1), kv]
    return [t.astype(BF) for t in ws + [_slot_cols(g[n][0]) for n in BULK_COLS]]


def _peers():
    x, y, c = lax.axis_index("x"), lax.axis_index("y"), lax.axis_index("c")
    me = 4 * x + 2 * y + c
    peers = []
    for k in range(1, N_DEV):
        px = 1 - x if k & 4 else x
        py = 1 - y if k & 2 else y
        pc = 1 - c if k & 1 else c
        peers.append(((px, py, pc), 4 * px + 2 * py + pc))
    return me, peers


def _exchange(name, arrays, scatter):
    n, c_in, c_out, c_shape, c_scratch = _cargo_specs((arrays, scatter))

    def body(*refs):
        start, wait = _exchange_ops(refs[:n], refs[n:2 * n], *refs[2 * n:], scatter)
        start()
        wait()

    return pl.pallas_call(body, out_shape=c_shape, in_specs=c_in, out_specs=c_out, scratch_shapes=c_scratch,
                          name=name)(*arrays)


def _exchange_ops(srcs, outs, send_sems, recv_sems, local_sems, scatter):
    me, peers = _peers()

    def local(a):
        return pltpu.make_async_copy(srcs[a].at[me] if scatter else srcs[a], outs[a].at[me], local_sems.at[a])

    def remote(a, k, landing):
        dev, idx = peers[k]
        return pltpu.make_async_remote_copy(src_ref=srcs[a].at[idx] if scatter else srcs[a],
                                            dst_ref=outs[a].at[idx if landing else me],
                                            send_sem=send_sems.at[a, k], recv_sem=recv_sems.at[a, k],
                                            device_id=dev, device_id_type=pl.DeviceIdType.MESH)

    pairs = [(a, k) for a in range(len(srcs)) for k in range(N_DEV - 1)]

    def start():
        for a in range(len(srcs)):
            local(a).start()
        for a, k in pairs:
            remote(a, k, False).start()

    def wait():
        for a, k in pairs:
            remote(a, k, True).wait_recv()
        for a, k in pairs:
            remote(a, k, False).wait_send()
        for a in range(len(srcs)):
            local(a).wait()

    return start, wait


ADAM_ROWS = 128


def sum_adamw(name, parts, row_off, w, m, v, tr):
    rows, wl = w.shape
    wp = parts.shape[2]
    tr = min(tr, rows)
    assert rows % tr == 0 and row_off % tr == 0, (name, rows, row_off, tr)
    off = row_off // tr

    def body(p_ref, w_ref, m_ref, v_ref, g_out, d_out, m_out, v_out):
        g = p_ref[0].astype(F32)
        for q in range(1, N_DEV):
            g = g + p_ref[q].astype(F32)
        g = g[:, :wl]
        m_new = ADAM_B1 * m_ref[...] + (1.0 - ADAM_B1) * g
        v_new = ADAM_B2 * v_ref[...] + (1.0 - ADAM_B2) * jnp.square(g)
        m_hat = m_new / (1.0 - ADAM_B1 ** ADAM_STEP)
        v_hat = v_new / (1.0 - ADAM_B2 ** ADAM_STEP)
        g_out[...] = g
        d_out[...] = -ADAM_LR * (m_hat / (jnp.sqrt(v_hat) + ADAM_EPS) + ADAM_WD * w_ref[...])
        m_out[...] = m_new
        v_out[...] = v_new

    blk = pl.BlockSpec((tr, wl), lambda i: (i, 0))
    return pl.pallas_call(
        body, grid=(rows // tr,),
        in_specs=[pl.BlockSpec((N_DEV, tr, wp), lambda i: (0, i + off, 0)), blk, blk, blk],
        out_specs=[blk] * 4, out_shape=[SDS((rows, wl), F32)] * 4, name=name,
        compiler_params=_params(("arbitrary",)))(parts, w, m, v)


def kernel(x, mem, positions, norm_g, ffn_w_gate, ffn_w_up, ffn_w_down, mem_w_kv, a_w_in, a_q_norm, a_kv_norm, a_w_uq, a_w_ukv, a_w_out, b_w_in, b_w_out, c_w_in, c_conv_w, c_conv_b, c_ln_g, c_ln_b, c_w_out, d_w_in, d_conv_w, d_w_out, loss_target, m_norm_g, m_ffn_w_gate, m_ffn_w_up, m_ffn_w_down, m_mem_w_kv, m_a_w_in, m_a_q_norm, m_a_kv_norm, m_a_w_uq, m_a_w_ukv, m_a_w_out, m_b_w_in, m_b_w_out, m_c_w_in, m_c_conv_w, m_c_conv_b, m_c_ln_g, m_c_ln_b, m_c_w_out, m_d_w_in, m_d_conv_w, m_d_w_out, v_norm_g, v_ffn_w_gate, v_ffn_w_up, v_ffn_w_down, v_mem_w_kv, v_a_w_in, v_a_q_norm, v_a_kv_norm, v_a_w_uq, v_a_w_ukv, v_a_w_out, v_b_w_in, v_b_w_out, v_c_w_in, v_c_conv_w, v_c_conv_b, v_c_ln_g, v_c_ln_b, v_c_w_out, v_d_w_in, v_d_conv_w, v_d_w_out):
    a = dict(zip(IN_NAMES, (x, mem, positions, norm_g, ffn_w_gate, ffn_w_up, ffn_w_down, mem_w_kv, a_w_in, a_q_norm, a_kv_norm, a_w_uq, a_w_ukv, a_w_out, b_w_in, b_w_out, c_w_in, c_conv_w, c_conv_b, c_ln_g, c_ln_b, c_w_out, d_w_in, d_conv_w, d_w_out, loss_target, m_norm_g, m_ffn_w_gate, m_ffn_w_up, m_ffn_w_down, m_mem_w_kv, m_a_w_in, m_a_q_norm, m_a_kv_norm, m_a_w_uq, m_a_w_ukv, m_a_w_out, m_b_w_in, m_b_w_out, m_c_w_in, m_c_conv_w, m_c_conv_b, m_c_ln_g, m_c_ln_b, m_c_w_out, m_d_w_in, m_d_conv_w, m_d_w_out, v_norm_g, v_ffn_w_gate, v_ffn_w_up, v_ffn_w_down, v_mem_w_kv, v_a_w_in, v_a_q_norm, v_a_kv_norm, v_a_w_uq, v_a_w_ukv, v_a_w_out, v_b_w_in, v_b_w_out, v_c_w_in, v_c_conv_w, v_c_conv_b, v_c_ln_g, v_c_ln_b, v_c_w_out, v_d_w_in, v_d_conv_w, v_d_w_out)))
    shapes = {n: tuple(a[n].shape) for n in WEIGHTS}
    got_edge = _exchange("gather_edge", wire_weights_edge(a) + [pack_local([a[n] for n in SMALL], F32, 8)], False)
    common = unpack_gathered(got_edge[-1], SMALL, shapes)
    common['a_q_norm'], common['a_kv_norm'] = a['a_q_norm'], a['a_kv_norm']
    edge = unwire_edge(got_edge[:-1])

    def bulk_grads(upper, early):
        ffns = [early['ffn1']] + [f for p in upper for f in p['ffn']]
        g = dict(a_w_out=early['a_w_out'])
        for p in upper:
            g.update({n: v for n, v in p.items() if n.endswith(('_w_in', '_w_out'))})
        return wire_grads_bulk(ffns, g, jnp.stack([early['mem_w_kv']] + [p['mem_w_kv'] for p in upper]))

    loss, grad_x, per_layer, got_bulk = local_step(
        a['x'][0], a['mem'][0], a['positions'][0], a['loss_target'][0], {**edge, **common},
        gather=(wire_weights_bulk(a), lambda got: {**unwire_bulk(got, edge), **common}), scatter=bulk_grads)
    flat_grads = stack_grads(per_layer, FLAT)
    got_late = _exchange("scatter_edge", wire_grads_edge(per_layer[0]['ffn'][0], per_layer[0])
                         + [pack_grads(flat_grads, FLAT, ADAM_ROWS)], True)
    res = {}

    def adam(n, p, row_off, rows, tr):
        view = [a[pre + n].reshape(rows, -1) for pre in ('', 'm_', 'v_')]
        res[n] = [t.reshape(shapes[n]) for t in sum_adamw("adamw_" + n, p, row_off, *view, tr)]

    both = lambda k: jnp.concatenate([got_late[k], got_bulk[k]], axis=1)
    n_ff = N_FFN * D_MODEL
    adam('ffn_w_gate', both(0), 0, n_ff, 256)
    adam('ffn_w_up', both(1), 0, n_ff, 256)
    n_dn = N_FFN * FF_SHARD
    rows_1024 = both(2)
    adam('ffn_w_down', rows_1024, 0, n_dn, 128)
    rows = D_MODEL // N_DEV
    for k, n in enumerate(ROW_SHARDED_1024):
        adam(n, rows_1024, n_dn + k * rows, rows, 128)
    adam('mem_w_kv', got_bulk[3], 0, DEPTH * rows, 128)
    adam('a_w_in', got_late[3], 0, rows, 128)
    for k, n in enumerate(EDGE_COLS):
        adam(n, got_late[4 + k], 0, shapes[n][1], 128)
    for k, n in enumerate(BULK_COLS):
        adam(n, got_bulk[4 + k], 0, shapes[n][1], 128)
    packed = [pack_local([a[pre + n] for n in FLAT], F32, ADAM_ROWS) for pre in ('', 'm_', 'v_')]
    flat = [unpack_local(r, FLAT, shapes) for r in sum_adamw("adamw_flat", got_late[-1], 0, *packed, ADAM_ROWS)]
    for n in FLAT:
        res[n] = [f[n] for f in flat]
    loss = lax.psum(loss, MESH_AXES)
    return (loss, grad_x[None], *[res[n][k] for k in range(4) for n in WEIGHTS])
```

```python
import functools
import math

import jax
import jax.numpy as jnp
from jax import lax
from jax.experimental import pallas as pl
from jax.experimental.pallas import tpu as pltpu

BF, F32 = jnp.bfloat16, jnp.float32
SDS = jax.ShapeDtypeStruct

D_MODEL, DEPTH, D_FF = 1024, 4, 2816
HEAD_DIM, MEM_HEADS, MEM_WIDTH, N_MEM = 64, 4, 256, 256
MLA_HEADS, MLA_Q_LORA, MLA_KV_LORA, MLA_NOPE, MLA_ROPE, MLA_V = 12, 384, 256, 64, 32, 64
MLA_QK = MLA_NOPE + MLA_ROPE
ROPE_THETA = 10000.0
DIL_GROUPS = ((128, 1), (512, 4), (2048, 16))
DIL_HEADS, DIL_HALF = 8, 64
DIL_QKV = len(DIL_GROUPS) * 3 * DIL_HEADS * HEAD_DIM
ALIBI_MAX = 8.0
CONV_CH, CONV_WIDTH, SC_CH, SC_WIDTH = 768, 31, 768, 3
EPS, NEG = 1e-6, -1e30
ADAM_LR, ADAM_B1, ADAM_B2, ADAM_EPS, ADAM_WD, ADAM_STEP = 0.001, 0.9, 0.999, 1e-08, 0.01, 10

N_DEV = 8
MESH_AXES = ("x", "y", "c")
FF_SHARD = D_FF // N_DEV
FF_SLOT = 384
D_FFP = N_DEV * FF_SLOT
LOG2E = 1.4426950408889634
V7X_VMEM_LIMIT = 48 * 1024 * 1024
HALO = 16
PACK_W = 1024

WEIGHTS = ['norm_g', 'ffn_w_gate', 'ffn_w_up', 'ffn_w_down', 'mem_w_kv', 'a_w_in', 'a_q_norm', 'a_kv_norm',
           'a_w_uq', 'a_w_ukv', 'a_w_out', 'b_w_in', 'b_w_out', 'c_w_in', 'c_conv_w', 'c_conv_b', 'c_ln_g',
           'c_ln_b', 'c_w_out', 'd_w_in', 'd_conv_w', 'd_w_out']
SHARD_AXIS = dict(norm_g=2, ffn_w_gate=3, ffn_w_up=3, ffn_w_down=2, mem_w_kv=1, a_w_in=1, a_q_norm=None,
                  a_kv_norm=None, a_w_uq=2, a_w_ukv=2, a_w_out=1, b_w_in=2, b_w_out=2, c_w_in=2, c_conv_w=2,
                  c_conv_b=1, c_ln_g=1, c_ln_b=1, c_w_out=1, d_w_in=2, d_conv_w=2, d_w_out=1)
SMALL = ['norm_g', 'c_conv_w', 'c_conv_b', 'c_ln_g', 'c_ln_b', 'd_conv_w']
BIG = [n for n in WEIGHTS if n not in SMALL and SHARD_AXIS[n] is not None]
IN_NAMES = (['x', 'mem', 'positions'] + WEIGHTS + ['loss_target'] + ['m_' + n for n in WEIGHTS]
            + ['v_' + n for n in WEIGHTS])


def _params(sem):
    return pltpu.CompilerParams(dimension_semantics=sem, vmem_limit_bytes=V7X_VMEM_LIMIT)


def _dot(a, b, ca=1, cb=0):
    return lax.dot_general(a, b, (((ca,), (cb,)), ((), ())), preferred_element_type=F32)


def mm_call(name, accs, M, N, K, *, tm, tn, tk, epilogue, outs, tiles=(), cols=(), fulls=(), reds=()):
    tm, tn, tk = min(tm, M), min(tn, N), min(tk, K)
    assert M % tm == 0 and N % tn == 0 and K % tk == 0, (name, M, N, K, tm, tn, tk)
    gj, gi, gk = N // tn, M // tm, K // tk
    flat = [p for acc in accs for p in acc]
    in_specs, args = [], []
    for (a, b, ta, tb) in flat:
        in_specs.append(pl.BlockSpec((tk, tm), lambda j, i, k: (k, i)) if ta else pl.BlockSpec((tm, tk), lambda j, i, k: (i, k)))
        in_specs.append(pl.BlockSpec((tn, tk), lambda j, i, k: (j, k)) if tb else pl.BlockSpec((tk, tn), lambda j, i, k: (k, j)))
        args += [a, b]
    for t in tiles:
        in_specs.append(pl.BlockSpec((tm, tn), lambda j, i, k: (i, j)))
        args.append(t)
    for c in cols:
        in_specs.append(pl.BlockSpec((c.shape[0], tn), lambda j, i, k: (0, j)))
        args.append(c)
    for f in fulls:
        in_specs.append(pl.BlockSpec(f.shape, lambda j, i, k: (0,) * f.ndim))
        args.append(f)
    out_shape = [SDS((M, N), dt) for dt in outs] + [SDS((r, N), F32) for r in reds]
    out_specs = ([pl.BlockSpec((tm, tn), lambda j, i, k: (i, j)) for _ in outs]
                 + [pl.BlockSpec((r, tn), lambda j, i, k: (0, j)) for r in reds])
    n_in = len(args)
    n_acc = len(accs)

    def body(*refs):
        in_refs = refs[:n_in]
        out_refs = refs[n_in:n_in + len(outs)]
        red_refs = refs[n_in + len(outs):n_in + len(outs) + len(reds)]
        acc_refs = refs[n_in + len(outs) + len(reds):]
        i, k = pl.program_id(1), pl.program_id(2)

        def product(idx):
            _, _, ta, tb = flat[idx]
            av = in_refs[2 * idx][...].astype(BF)
            bv = in_refs[2 * idx + 1][...].astype(BF)
            return _dot(av, bv, 0 if ta else 1, 1 if tb else 0)

        def finish(vals):
            p = 2 * len(flat)
            tv = [r[...] for r in in_refs[p:p + len(tiles)]]
            cv = [r[...] for r in in_refs[p + len(tiles):p + len(tiles) + len(cols)]]
            fv = [r[...] for r in in_refs[p + len(tiles) + len(cols):]]
            ov, rv = epilogue(vals, tv, cv, fv)
            for r, v in zip(out_refs, ov):
                r[...] = v.astype(r.dtype)
            for r, v in zip(red_refs, rv):
                @pl.when(i == 0)
                def _(r=r, v=v):
                    r[...] = v

                @pl.when(i > 0)
                def _(r=r, v=v):
                    r[...] += v

        starts = [sum(len(acc) for acc in accs[:n]) for n in range(n_acc)]
        if gk == 1:
            vals = []
            for s, acc in zip(starts, accs):
                tot = product(s)
                for p in range(1, len(acc)):
                    tot = tot + product(s + p)
                vals.append(tot)
            finish(vals)
        else:
            @pl.when(k == 0)
            def _():
                for r in acc_refs:
                    r[...] = jnp.zeros(r.shape, F32)

            for r, s, acc in zip(acc_refs, starts, accs):
                for p in range(len(acc)):
                    r[...] += product(s + p)

            @pl.when(k == gk - 1)
            def _():
                finish([r[...] for r in acc_refs])

    scratch = [] if gk == 1 else [pltpu.VMEM((tm, tn), F32) for _ in range(n_acc)]
    res = pl.pallas_call(body, grid=(gj, gi, gk), in_specs=in_specs, out_specs=out_specs, out_shape=out_shape,
                         scratch_shapes=scratch, name=name,
                         compiler_params=_params(("arbitrary", "arbitrary", "arbitrary")))(*args)
    return res


def _plain(dt):
    return dict(epilogue=lambda a, t, c, f: ([a[0]], []), outs=[dt])


def rows_call(name, fn, S, ts, tiled, fulls, outs, reds=()):
    ts = min(ts, S)
    assert S % ts == 0, (name, S, ts)
    in_specs = ([pl.BlockSpec((ts, t.shape[1]), lambda i: (i, 0)) for t in tiled]
                + [pl.BlockSpec(f.shape, lambda i, n=f.ndim: (0,) * n) for f in fulls])
    out_shape = [SDS((S, w), dt) for (w, dt) in outs] + [SDS(rw, F32) for rw in reds]
    out_specs = ([pl.BlockSpec((ts, w), lambda i: (i, 0)) for (w, _) in outs]
                 + [pl.BlockSpec(rw, lambda i: (0, 0)) for rw in reds])
    nt, nf, no = len(tiled), len(fulls), len(outs)

    def body(*refs):
        i = pl.program_id(0)
        tv = [r[...] for r in refs[:nt]]
        fv = [r[...] for r in refs[nt:nt + nf]]
        ov, rv = fn(i, tv, fv)
        for r, v in zip(refs[nt + nf:nt + nf + no], ov):
            r[...] = v.astype(r.dtype)
        for r, v in zip(refs[nt + nf + no:], rv):
            @pl.when(i == 0)
            def _(r=r, v=v):
                r[...] = v

            @pl.when(i > 0)
            def _(r=r, v=v):
                r[...] += v

    return pl.pallas_call(body, grid=(S // ts,), in_specs=in_specs, out_specs=out_specs, out_shape=out_shape,
                          name=name, compiler_params=_params(("arbitrary",)))(*tiled, *fulls)


def hrows_call(name, fn, H, S, ts, tiled, outs):
    ts = min(ts, S)
    in_specs = [pl.BlockSpec((1, ts, t.shape[2]), lambda h, i: (h, i, 0)) for t in tiled]
    out_shape = [SDS((H, S, w), dt) for (w, dt) in outs]
    out_specs = [pl.BlockSpec((1, ts, w), lambda h, i: (h, i, 0)) for (w, _) in outs]
    nt = len(tiled)

    def body(*refs):
        ov = fn([r[0] for r in refs[:nt]])
        for r, v in zip(refs[nt:], ov):
            r[0] = v.astype(r.dtype)

    return pl.pallas_call(body, grid=(H, S // ts), in_specs=in_specs, out_specs=out_specs, out_shape=out_shape,
                          name=name, compiler_params=_params(("arbitrary", "arbitrary")))(*tiled)


def _rms(x, g):
    r = lax.rsqrt(jnp.mean(x * x, axis=-1, keepdims=True) + EPS)
    return x * r * g


def _rms_bwd(x, g, dy):
    r = lax.rsqrt(jnp.mean(x * x, axis=-1, keepdims=True) + EPS)
    xh = x * r
    dg = jnp.sum(dy * xh, axis=0, keepdims=True)
    dxh = dy * g
    dx = r * (dxh - xh * jnp.mean(dxh * xh, axis=-1, keepdims=True))
    return dx, dg


def _silu(x):
    return x * jax.nn.sigmoid(x)


def _dsilu(x):
    s = jax.nn.sigmoid(x)
    return s * (1.0 + x * (1.0 - s))


def rms_rows(name, x, g):
    S, w = x.shape
    return rows_call(name, lambda i, t, f: ([_rms(t[0].astype(F32), f[0])], []), S, 512, [x], [g], [(w, BF)])[0]


def ffn_fwd(tag, x0, g_in, g_out, wg, wu, wd):
    S = x0.shape[0]
    h = rms_rows(tag + "_rms", x0, g_in)

    def ep_up(a, t, c, f):
        return [a[0], a[1], _silu(a[0]) * a[1]], []

    gate, up, act = mm_call(tag + "_up", [[(h, wg, False, False)], [(h, wu, False, False)]], S, D_FFP, D_MODEL,
                            tm=1024, tn=512, tk=1024, epilogue=ep_up, outs=[BF, BF, BF])

    def ep_down(a, t, c, f):
        return [a[0], t[0] + 0.5 * _rms(a[0], c[0])], []

    y, x1 = mm_call(tag + "_down", [[(act, wd, False, False)]], S, D_MODEL, D_FFP, tm=512, tn=1024, tk=1024,
                    epilogue=ep_down, outs=[BF, F32], tiles=[x0], cols=[g_out])
    return x1, dict(x0=x0, h=h, gate=gate, up=up, act=act, y=y)


def ffn_bwd(tag, dx1, sv, g_in, g_out, wg, wu, wd):
    S = dx1.shape[0]

    def post(i, t, f):
        dx, dg = _rms_bwd(t[1].astype(F32), f[0], 0.5 * t[0])
        return [dx], [dg]

    da, dg_out = rows_call(tag + "_bpost", post, S, 512, [dx1, sv['y']], [g_out], [(D_MODEL, BF)], [(1, D_MODEL)])

    def ep_act(a, t, c, f):
        g, u = t[0].astype(F32), t[1].astype(F32)
        sg = jax.nn.sigmoid(g)
        return [a[0] * u * (sg * (1.0 + g * (1.0 - sg))), a[0] * (g * sg)], []

    dgate, dup = mm_call(tag + "_bact", [[(da, wd, False, True)]], S, D_FFP, D_MODEL, tm=1024, tn=512, tk=1024,
                         epilogue=ep_act, outs=[BF, BF], tiles=[sv['gate'], sv['up']])
    dwd, = mm_call(tag + "_dwd", [[(sv['act'], da, True, False)]], D_FFP, D_MODEL, S, tm=1024, tn=1024, tk=1024,
                   **_plain(F32))
    dwg, dwu = mm_call(tag + "_dwgu", [[(sv['h'], dgate, True, False)], [(sv['h'], dup, True, False)]],
                       D_MODEL, D_FFP, S, tm=1024, tn=512, tk=1024,
                       epilogue=lambda a, t, c, f: ([a[0], a[1]], []), outs=[F32, F32])

    def ep_in(a, t, c, f):
        dx, dg = _rms_bwd(t[0], c[0], a[0])
        return [t[1] + dx], [dg]

    (dx0, dg_in) = mm_call(tag + "_bin", [[(dgate, wg, False, True), (dup, wu, False, True)]], S, D_MODEL, D_FFP,
                           tm=512, tn=1024, tk=1024, epilogue=ep_in, outs=[F32], tiles=[sv['x0'], dx1], cols=[g_in],
                           reds=[1])
    return dx0, dict(g_in=dg_in, g_out=dg_out, wg=dwg, wu=dwu, wd=dwd)


def mem_kv_fwd(tag, mem, g6, w_kv):
    def body(mem_ref, g_ref, w_ref, kv_ref):
        kv_ref[...] = _dot(_rms(mem_ref[...], g_ref[...]).astype(BF), w_ref[...])

    return pl.pallas_call(body, out_shape=SDS((N_MEM, 2 * MEM_WIDTH), F32), name=tag + "_memkv",
                          compiler_params=pltpu.CompilerParams(vmem_limit_bytes=V7X_VMEM_LIMIT))(mem, g6, w_kv)


def mem_kv_bwd(tag, mem, g6, w_kv, dkv):
    def body(mem_ref, g_ref, w_ref, dkv_ref, dw_ref, dg_ref):
        x = mem_ref[...]
        r = lax.rsqrt(jnp.mean(x * x, axis=-1, keepdims=True) + EPS)
        xh = x * r
        d = dkv_ref[...].astype(BF)
        dw_ref[...] = _dot((xh * g_ref[...]).astype(BF), d, 0, 0)
        dmem_n = _dot(d, w_ref[...], 1, 1)
        dg_ref[...] = jnp.sum(dmem_n * xh, axis=0, keepdims=True)

    return pl.pallas_call(body, out_shape=(SDS((D_MODEL, 2 * MEM_WIDTH), F32), SDS((1, D_MODEL), F32)),
                          name=tag + "_memkvb",
                          compiler_params=pltpu.CompilerParams(vmem_limit_bytes=V7X_VMEM_LIMIT))(mem, g6, w_kv, dkv)


def _block_diag_kv(kv):
    k = kv[:, :MEM_WIDTH].reshape(N_MEM, MEM_HEADS, HEAD_DIM)
    v = kv[:, MEM_WIDTH:].reshape(N_MEM, MEM_HEADS, HEAD_DIM)
    eye = jnp.eye(MEM_HEADS, dtype=kv.dtype)
    kbd = jnp.einsum('nhd,hg->hdgn', k, eye).reshape(MEM_WIDTH, MEM_HEADS * N_MEM)
    vbd = jnp.einsum('nhd,hg->hngd', v, eye).reshape(MEM_HEADS * N_MEM, MEM_WIDTH)
    return kbd.astype(BF), vbd.astype(BF)


def _mem_probs(q, kbd):
    s = _dot(q, kbd) * (HEAD_DIM ** -0.5)
    ps = []
    for h in range(MEM_HEADS):
        sh = s[:, h * N_MEM:(h + 1) * N_MEM]
        e = jnp.exp(sh - jnp.max(sh, axis=-1, keepdims=True))
        ps.append(e / jnp.sum(e, axis=-1, keepdims=True))
    return ps


def mem_attn_fwd(tag, q_mem, kbd, vbd):
    S = q_mem.shape[0]

    def fn(i, t, f):
        p = jnp.concatenate(_mem_probs(t[0], f[0]), axis=1).astype(BF)
        return [_dot(p, f[1])], []

    return rows_call(tag + "_mem", fn, S, 512, [q_mem], [kbd, vbd], [(MEM_WIDTH, BF)])[0]


def mem_attn_bwd(tag, q_mem, dmo, kbd, vbd):
    S = q_mem.shape[0]

    def fn(i, t, f):
        q, do = t
        ps = _mem_probs(q, f[0])
        dp = _dot(do, f[1], 1, 1)
        dss = []
        for h in range(MEM_HEADS):
            dph = dp[:, h * N_MEM:(h + 1) * N_MEM]
            dss.append(ps[h] * (dph - jnp.sum(dph * ps[h], axis=-1, keepdims=True)) * (HEAD_DIM ** -0.5))
        ds = jnp.concatenate(dss, axis=1).astype(BF)
        p = jnp.concatenate(ps, axis=1).astype(BF)
        return [_dot(ds, f[0], 1, 1)], [_dot(q, ds, 0, 0), _dot(p, do, 0, 0)]

    dq, dkbd, dvbd = rows_call(tag + "_memb", fn, S, 512, [q_mem, dmo], [kbd, vbd], [(MEM_WIDTH, BF)],
                               [(MEM_WIDTH, MEM_HEADS * N_MEM), (MEM_HEADS * N_MEM, MEM_WIDTH)])
    dk = jnp.einsum('hdhn->nhd', dkbd.reshape(MEM_HEADS, HEAD_DIM, MEM_HEADS, N_MEM)).reshape(N_MEM, MEM_WIDTH)
    dv = jnp.einsum('hnhd->nhd', dvbd.reshape(MEM_HEADS, N_MEM, MEM_HEADS, HEAD_DIM)).reshape(N_MEM, MEM_WIDTH)
    return dq, jnp.concatenate([dk, dv], axis=1)


def _rope_tables(positions):
    half = MLA_ROPE // 2
    inv = ROPE_THETA ** (-jnp.arange(half, dtype=F32) / half)
    ang = positions.astype(F32)[:, None] * inv
    cos, sin = jnp.cos(ang), jnp.sin(ang)
    cos32, sin32 = jnp.concatenate([cos, cos], 1), jnp.concatenate([sin, sin], 1)
    idx = jnp.arange(MLA_ROPE)
    p32 = (jnp.where((idx[:, None] == idx[None, :] + half), -1.0, 0.0)
           + jnp.where((idx[:, None] + half == idx[None, :]), 1.0, 0.0)).astype(F32)
    S = positions.shape[0]
    ones = jnp.ones((S, MLA_NOPE), F32)
    cq = jnp.tile(jnp.concatenate([ones, cos32], 1), (1, MLA_HEADS))
    sq = jnp.tile(jnp.concatenate([0.0 * ones, sin32], 1), (1, MLA_HEADS))
    pq = jnp.zeros((MLA_QK, MLA_QK), F32).at[MLA_NOPE:, MLA_NOPE:].set(p32)
    pq = jnp.kron(jnp.eye(MLA_HEADS, dtype=F32), pq)
    return dict(cos32=cos32, sin32=sin32, p32=p32.astype(BF), p32t=p32.T.astype(BF), cq=cq, sq=sq,
                pq=pq.astype(BF), pqt=pq.T.astype(BF))


def _cargo_specs(cargo):
    if cargo is None:
        return 0, [], [], [], []
    arrays, scatter = cargo
    n = len(arrays)
    hbm = pl.BlockSpec(memory_space=pltpu.HBM)
    out_shape = [SDS(t.shape if scatter else (N_DEV,) + t.shape, t.dtype) for t in arrays]
    scratch = [pltpu.SemaphoreType.DMA((n, N_DEV - 1)), pltpu.SemaphoreType.DMA((n, N_DEV - 1)),
               pltpu.SemaphoreType.DMA((n,))]
    return n, [hbm] * n, [hbm] * n, out_shape, scratch


def flash_fwd(tag, qh, kh, vh1, dv, tq=1024, tk=2048, cargo=None):
    H, S, dq = qh.shape
    dv1 = vh1.shape[2]
    tq, tk = min(tq, S), min(tk, S)
    nk = S // tk
    nc, c_in, c_out, c_shape, c_scratch = _cargo_specs(cargo)

    def body(*refs):
        q_ref, k_ref, v_ref = refs[:3]
        o_ref, lse_ref = refs[3 + nc:5 + nc]
        m_s, acc_s = refs[5 + 2 * nc:7 + 2 * nc]
        h, i, j = pl.program_id(0), pl.program_id(1), pl.program_id(2)
        if nc:
            start, wait = _exchange_ops(refs[3:3 + nc], refs[5 + nc:5 + 2 * nc], *refs[7 + 2 * nc:], cargo[1])
            pl.when((h == 0) & (i == 0) & (j == 0))(start)

        @pl.when(j == 0)
        def _():
            m_s[...] = jnp.full(m_s.shape, NEG, F32)
            acc_s[...] = jnp.zeros(acc_s.shape, F32)

        s = _dot(q_ref[0], k_ref[0], 1, 1)
        m_old = m_s[...]
        m_new = jnp.maximum(m_old, jnp.max(s, axis=-1, keepdims=True))
        p = jnp.exp2((s - m_new).astype(BF))
        acc_s[...] = jnp.exp2(m_old - m_new) * acc_s[...] + _dot(p, v_ref[0])
        m_s[...] = m_new

        @pl.when(j == nk - 1)
        def _():
            acc = acc_s[...]
            l = acc[:, dv:dv + 1]
            o_ref[0] = (acc[:, :dv] / l).astype(o_ref.dtype)
            lse_ref[0] = m_s[...] + jnp.log2(l)

        if nc:
            pl.when((h == H - 1) & (i == S // tq - 1) & (j == nk - 1))(wait)

    res = pl.pallas_call(
        body, grid=(H, S // tq, nk),
        in_specs=[pl.BlockSpec((1, tq, dq), lambda h, i, j: (h, i, 0)),
                  pl.BlockSpec((1, tk, dq), lambda h, i, j: (h, j, 0)),
                  pl.BlockSpec((1, tk, dv1), lambda h, i, j: (h, j, 0))] + c_in,
        out_specs=[pl.BlockSpec((1, tq, dv), lambda h, i, j: (h, i, 0)),
                   pl.BlockSpec((1, tq, 1), lambda h, i, j: (h, i, 0))] + c_out,
        out_shape=[SDS((H, S, dv), BF), SDS((H, S, 1), F32)] + c_shape,
        scratch_shapes=[pltpu.VMEM((tq, 1), F32), pltpu.VMEM((tq, dv1), F32)] + c_scratch,
        name=tag + "_flash", compiler_params=_params(("arbitrary", "arbitrary", "arbitrary")))(
            qh, kh, vh1, *(cargo[0] if nc else []))
    return res[0], res[1], list(res[2:])


def flash_bwd(tag, qh, kh, vh, doh, lse_row, delta_row, tq=1024, tk=1024, cargo=None):
    H, S, dq = qh.shape
    dv = vh.shape[2]
    tq, tk = min(tq, S), min(tk, S)
    nc, c_in, c_out, c_shape, c_scratch = _cargo_specs(cargo)

    def body(*refs):
        k_ref, v_ref, q_ref, do_ref, lse_ref, dl_ref = refs[:6]
        dk_ref, dv_ref, dq_ref = refs[6 + nc:9 + nc]
        h, j, i = pl.program_id(0), pl.program_id(1), pl.program_id(2)
        if nc:
            start, wait = _exchange_ops(refs[6:6 + nc], refs[9 + nc:9 + 2 * nc], *refs[9 + 2 * nc:], cargo[1])
            pl.when((h == 0) & (j == 0) & (i == 0))(start)
        k, v, q, do = k_ref[0], v_ref[0], q_ref[0], do_ref[0]
        pt = jnp.exp2((_dot(k, q, 1, 1) - lse_ref[0]).astype(BF))
        dst = pt * (_dot(v, do, 1, 1) - dl_ref[0]).astype(BF)
        dvc = _dot(pt, do)
        dkc = _dot(dst, q) * (1.0 / LOG2E)
        dqc = _dot(dst, k, 0, 0)

        @pl.when(i == 0)
        def _():
            dk_ref[0] = dkc
            dv_ref[0] = dvc

        @pl.when(i > 0)
        def _():
            dk_ref[0] += dkc
            dv_ref[0] += dvc

        rows = pl.ds(pl.multiple_of(i * tq, tq), tq)

        @pl.when(j == 0)
        def _():
            dq_ref[0, rows, :] = dqc

        @pl.when(j > 0)
        def _():
            dq_ref[0, rows, :] += dqc

        if nc:
            pl.when((h == H - 1) & (j == S // tk - 1) & (i == S // tq - 1))(wait)

    res = pl.pallas_call(
        body, grid=(H, S // tk, S // tq),
        in_specs=[pl.BlockSpec((1, tk, dq), lambda h, j, i: (h, j, 0)),
                  pl.BlockSpec((1, tk, dv), lambda h, j, i: (h, j, 0)),
                  pl.BlockSpec((1, tq, dq), lambda h, j, i: (h, i, 0)),
                  pl.BlockSpec((1, tq, dv), lambda h, j, i: (h, i, 0)),
                  pl.BlockSpec((1, 1, tq), lambda h, j, i: (h, 0, i)),
                  pl.BlockSpec((1, 1, tq), lambda h, j, i: (h, 0, i))] + c_in,
        out_specs=[pl.BlockSpec((1, tk, dq), lambda h, j, i: (h, j, 0)),
                   pl.BlockSpec((1, tk, dv), lambda h, j, i: (h, j, 0)),
                   pl.BlockSpec((1, S, dq), lambda h, j, i: (h, 0, 0))] + c_out,
        out_shape=[SDS((H, S, dq), F32), SDS((H, S, dv), F32), SDS((H, S, dq), F32)] + c_shape,
        scratch_shapes=c_scratch, name=tag + "_flashb",
        compiler_params=_params(("arbitrary", "arbitrary", "arbitrary")))(
            kh, vh, qh, doh, lse_row, delta_row, *(cargo[0] if nc else []))
    return res[0], res[1], res[2], list(res[3:])


def _to_heads(t, H):
    return t.reshape(t.shape[0], H, -1).transpose(1, 0, 2)


def _from_heads(t):
    return t.transpose(1, 0, 2).reshape(t.shape[1], -1)


def mla_fwd(tag, z, rope, q_g, kv_g, w_uq, w_ukv, cargo=None):
    S = z.shape[0]
    o1, o2, o3 = MLA_Q_LORA, MLA_Q_LORA + MLA_KV_LORA, MLA_Q_LORA + MLA_KV_LORA + MLA_ROPE
    c_q, c_kv, k_r, q_mem = z[:, :o1], z[:, o1:o2], z[:, o2:o3], z[:, o3:]

    def pre(i, t, f):
        kr = t[2].astype(F32)
        krot = kr * t[3] + _dot(t[2], f[2]) * t[4]
        return [_rms(t[0].astype(F32), f[0]), _rms(t[1].astype(F32), f[1]), krot], []

    qn, kvn, krot = rows_call(tag + "_pre", pre, S, 512, [c_q, c_kv, k_r, rope['cos32'], rope['sin32']],
                              [q_g, kv_g, rope['p32']], [(MLA_Q_LORA, BF), (MLA_KV_LORA, BF), (MLA_ROPE, BF)])

    qscale = MLA_QK ** -0.5 * LOG2E

    def ep_q(a, t, c, f):
        return [(a[0] * t[0] + _dot(a[0].astype(BF), f[0]) * t[1]) * qscale], []

    nq = MLA_HEADS * MLA_QK
    q, = mm_call(tag + "_uq", [[(qn, w_uq, False, False)]], S, nq, MLA_Q_LORA, tm=512, tn=nq, tk=MLA_Q_LORA,
                 epilogue=ep_q, outs=[BF], tiles=[rope['cq'], rope['sq']], fulls=[rope['pq']])
    nkv = MLA_HEADS * (MLA_NOPE + MLA_V)
    kv, = mm_call(tag + "_ukv", [[(kvn, w_ukv, False, False)]], S, nkv, MLA_KV_LORA, tm=1024, tn=nkv, tk=MLA_KV_LORA,
                  **_plain(BF))
    qh = _to_heads(q, MLA_HEADS)
    kv3 = kv.reshape(S, MLA_HEADS, MLA_NOPE + MLA_V)
    kh = jnp.concatenate([kv3[:, :, :MLA_NOPE].transpose(1, 0, 2),
                          jnp.broadcast_to(krot[None], (MLA_HEADS, S, MLA_ROPE))], axis=2)
    vh = kv3[:, :, MLA_NOPE:].transpose(1, 0, 2)
    ones = jnp.ones((MLA_HEADS, S, 1), BF)
    vh1 = jnp.concatenate([vh, ones, jnp.zeros((MLA_HEADS, S, 128 - MLA_V - 1), BF)], axis=2)
    oh, lse, got = flash_fwd(tag, qh, kh, vh1, MLA_V, cargo=cargo)
    sv = dict(c_q=c_q, c_kv=c_kv, k_r=k_r, qn=qn, kvn=kvn, qh=qh, kh=kh, vh=vh, oh=oh, lse=lse)
    return _from_heads(oh), q_mem, sv, got


def mla_bwd(tag, do, sv, rope, q_g, kv_g, w_uq, w_ukv, cargo=None):
    S = do.shape[0]
    H = MLA_HEADS
    doh = _to_heads(do, H)
    delta, = hrows_call(tag + "_delta",
                        lambda t: [jnp.sum(t[0].astype(F32) * t[1].astype(F32), axis=-1, keepdims=True)],
                        H, S, 1024, [doh, sv['oh']], [(1, F32)])
    dkh, dvh, dqh, got = flash_bwd(tag, sv['qh'], sv['kh'], sv['vh'], doh, sv['lse'].reshape(H, 1, S),
                                   delta.reshape(H, 1, S), cargo=cargo)
    nq = H * MLA_QK
    dq_rot = _from_heads(dqh)
    scale = MLA_QK ** -0.5

    def unrope(i, t, f):
        d = t[0] * scale
        return [d * t[1] + _dot((d * t[2]).astype(BF), f[0])], []

    dq, = rows_call(tag + "_unrope", unrope, S, 512, [dq_rot, rope['cq'], rope['sq']], [rope['pqt']], [(nq, BF)])

    def ep_norm(a, t, c, f):
        dx, dg = _rms_bwd(t[0].astype(F32), c[0], a[0])
        return [dx], [dg]

    dc_q, dq_g = mm_call(tag + "_bqn", [[(dq, w_uq, False, True)]], S, MLA_Q_LORA, nq, tm=1024, tn=MLA_Q_LORA, tk=nq,
                         epilogue=ep_norm, outs=[BF], tiles=[sv['c_q']], cols=[q_g], reds=[1])
    dw_uq, = mm_call(tag + "_dwuq", [[(sv['qn'], dq, True, False)]], MLA_Q_LORA, nq, S, tm=MLA_Q_LORA, tn=nq, tk=1024,
                     **_plain(F32))
    nkv = H * (MLA_NOPE + MLA_V)
    dkv = jnp.concatenate([dkh[:, :, :MLA_NOPE], dvh], axis=2).transpose(1, 0, 2).reshape(S, nkv).astype(BF)
    dc_kv, dkv_g = mm_call(tag + "_bkvn", [[(dkv, w_ukv, False, True)]], S, MLA_KV_LORA, nkv, tm=1024, tn=MLA_KV_LORA,
                           tk=nkv, epilogue=ep_norm, outs=[BF], tiles=[sv['c_kv']], cols=[kv_g], reds=[1])
    dw_ukv, = mm_call(tag + "_dwukv", [[(sv['kvn'], dkv, True, False)]], MLA_KV_LORA, nkv, S, tm=MLA_KV_LORA, tn=nkv,
                      tk=1024, **_plain(F32))
    dkr_heads = dkh[:, :, MLA_NOPE:].transpose(1, 0, 2).reshape(S, H * MLA_ROPE)
    hsum = jnp.tile(jnp.eye(MLA_ROPE, dtype=F32), (H, 1))

    def krope(i, t, f):
        d = jnp.dot(t[0], f[0], preferred_element_type=F32, precision=lax.Precision.HIGHEST)
        return [d * t[1] + _dot((d * t[2]).astype(BF), f[1])], []

    dk_r, = rows_call(tag + "_bkr", krope, S, 512, [dkr_heads, rope['cos32'], rope['sin32']], [hsum, rope['p32t']],
                      [(MLA_ROPE, BF)])
    grads = dict(a_q_norm=dq_g, a_kv_norm=dkv_g, a_w_uq=dw_uq, a_w_ukv=dw_ukv)
    return [dc_q, dc_kv, dk_r], grads, got


def _dil_perm(t, dil):
    H, S, d = t.shape
    return t.reshape(H, S // dil, dil, d).transpose(0, 2, 1, 3).reshape(H, S, d)


def _dil_unperm(t, dil):
    H, S, d = t.shape
    return t.reshape(H, dil, S // dil, d).transpose(0, 2, 1, 3).reshape(H, S, d)


def _band(rpos, cpos, S, seg_shift):
    dist = jnp.abs(rpos - cpos)
    valid = ((dist <= DIL_HALF) & (cpos >= 0) & (cpos < S) & (rpos >= 0) & (rpos < S)
             & (jnp.right_shift(rpos, seg_shift) == jnp.right_shift(cpos, seg_shift)))
    return dist, valid


def _halo_specs(blk, halo, n_halo_blocks, w, nh):
    r = blk // halo
    return [pl.BlockSpec((nh, halo, w), lambda h, i: (h, jnp.maximum(i * r - 1, 0), 0)),
            pl.BlockSpec((nh, blk, w), lambda h, i: (h, i, 0)),
            pl.BlockSpec((nh, halo, w), lambda h, i: (h, jnp.minimum((i + 1) * r, n_halo_blocks - 1), 0))]


def dil_fwd(tag, q, k, v, slopes, dil, tq=256):
    H, S, dh = q.shape
    nh = H
    tq = min(tq, S)
    seg_shift = int(math.log2(S // dil))
    hb = DIL_HALF
    scale = dh ** -0.5

    def body(sl_ref, q_ref, kp, kc, kn, vp, vc, vn, o_ref, lse_ref):
        i = pl.program_id(1)
        shp = (tq, tq + 2 * hb)
        qpos = i * tq + lax.broadcasted_iota(jnp.int32, shp, 0)
        kpos = i * tq - hb + lax.broadcasted_iota(jnp.int32, shp, 1)
        dist, valid = _band(qpos, kpos, S, seg_shift)
        tokens = (dist * dil).astype(F32)
        for h in range(nh):
            kcat = jnp.concatenate([kp[h], kc[h], kn[h]], axis=0)
            vcat = jnp.concatenate([vp[h], vc[h], vn[h]], axis=0)
            s = _dot(q_ref[h], kcat, 1, 1) * scale - sl_ref[h][:, :1] * tokens
            s = jnp.where(valid, s, NEG)
            m = jnp.max(s, axis=-1, keepdims=True)
            lse = m + jnp.log(jnp.sum(jnp.exp(s - m), axis=-1, keepdims=True))
            o_ref[h] = _dot(jnp.exp(s - lse).astype(BF), vcat)
            lse_ref[h] = lse

    return pl.pallas_call(
        body, grid=(H // nh, S // tq),
        in_specs=([pl.BlockSpec((nh, 1, 128), lambda h, i: (h, 0, 0)), pl.BlockSpec((nh, tq, dh), lambda h, i: (h, i, 0))]
                  + _halo_specs(tq, hb, S // hb, dh, nh) * 2),
        out_specs=[pl.BlockSpec((nh, tq, dh), lambda h, i: (h, i, 0)), pl.BlockSpec((nh, tq, 1), lambda h, i: (h, i, 0))],
        out_shape=[SDS((H, S, dh), F32), SDS((H, S, 1), F32)],
        name=tag + "_dil", compiler_params=_params(("arbitrary", "arbitrary")))(slopes, q, k, k, k, v, v, v)


def dil_bwd_q(tag, q, k, v, do, lse, deff, slopes, dil, tq=256):
    H, S, dh = q.shape
    nh = H
    tq = min(tq, S)
    seg_shift = int(math.log2(S // dil))
    hb = DIL_HALF
    scale = dh ** -0.5

    def body(sl_ref, q_ref, do_ref, lse_ref, de_ref, kp, kc, kn, vp, vc, vn, dq_ref):
        i = pl.program_id(1)
        shp = (tq, tq + 2 * hb)
        qpos = i * tq + lax.broadcasted_iota(jnp.int32, shp, 0)
        kpos = i * tq - hb + lax.broadcasted_iota(jnp.int32, shp, 1)
        dist, valid = _band(qpos, kpos, S, seg_shift)
        tokens = (dist * dil).astype(F32)
        for h in range(nh):
            kcat = jnp.concatenate([kp[h], kc[h], kn[h]], axis=0)
            vcat = jnp.concatenate([vp[h], vc[h], vn[h]], axis=0)
            s = _dot(q_ref[h], kcat, 1, 1) * scale - sl_ref[h][:, :1] * tokens
            p = jnp.where(valid, jnp.exp(jnp.where(valid, s, NEG) - lse_ref[h]), 0.0)
            dp = _dot(do_ref[h], vcat, 1, 1)
            ds = (p * (dp - de_ref[h]) * scale).astype(BF)
            dq_ref[h] = _dot(ds, kcat)

    row = lambda w: pl.BlockSpec((nh, tq, w), lambda h, i: (h, i, 0))
    return pl.pallas_call(
        body, grid=(H // nh, S // tq),
        in_specs=([pl.BlockSpec((nh, 1, 128), lambda h, i: (h, 0, 0)), row(dh), row(dh), row(1), row(1)]
                  + _halo_specs(tq, hb, S // hb, dh, nh) * 2),
        out_specs=row(dh), out_shape=SDS((H, S, dh), F32),
        name=tag + "_dilbq",
        compiler_params=_params(("arbitrary", "arbitrary")))(slopes, q, do, lse, deff, k, k, k, v, v, v)


def dil_bwd_kv(tag, q, k, v, do, lse_row, deff_row, slopes, dil, tk=256):
    H, S, dh = q.shape
    nh = H
    tk = min(tk, S)
    seg_shift = int(math.log2(S // dil))
    hq = 128
    scale = dh ** -0.5
    r = tk // hq
    nhb = S // hq
    stat_specs = [pl.BlockSpec((nh, 1, hq), lambda h, j: (h, 0, jnp.maximum(j * r - 1, 0))),
                  pl.BlockSpec((nh, 1, tk), lambda h, j: (h, 0, j)),
                  pl.BlockSpec((nh, 1, hq), lambda h, j: (h, 0, jnp.minimum((j + 1) * r, nhb - 1)))]

    def body(sl_ref, k_ref, v_ref, qp, qc, qn, dop, doc, don, lp, lc, ln, ep, ec, en, dk_ref, dv_ref):
        j = pl.program_id(1)
        shp = (tk, tk + 2 * hq)
        kpos = j * tk + lax.broadcasted_iota(jnp.int32, shp, 0)
        qpos = j * tk - hq + lax.broadcasted_iota(jnp.int32, shp, 1)
        dist, valid = _band(kpos, qpos, S, seg_shift)
        tokens = (dist * dil).astype(F32)
        for h in range(nh):
            qcat = jnp.concatenate([qp[h], qc[h], qn[h]], axis=0)
            docat = jnp.concatenate([dop[h], doc[h], don[h]], axis=0)
            lse = jnp.concatenate([lp[h], lc[h], ln[h]], axis=1)
            deff = jnp.concatenate([ep[h], ec[h], en[h]], axis=1)
            st = _dot(k_ref[h], qcat, 1, 1) * scale - sl_ref[h][:, :1] * tokens
            pt = jnp.where(valid, jnp.exp(jnp.where(valid, st, NEG) - jnp.where(valid, lse, 0.0)), 0.0)
            dpt = _dot(v_ref[h], docat, 1, 1)
            dst = (pt * (dpt - deff) * scale).astype(BF)
            dv_ref[h] = _dot(pt.astype(BF), docat)
            dk_ref[h] = _dot(dst, qcat)

    cur = lambda w: pl.BlockSpec((nh, tk, w), lambda h, j: (h, j, 0))
    return pl.pallas_call(
        body, grid=(H // nh, S // tk),
        in_specs=([pl.BlockSpec((nh, 1, 128), lambda h, j: (h, 0, 0)), cur(dh), cur(dh)]
                  + _halo_specs(tk, hq, nhb, dh, nh) * 2 + stat_specs * 2),
        out_specs=[cur(dh), cur(dh)], out_shape=[SDS((H, S, dh), F32), SDS((H, S, dh), F32)],
        name=tag + "_dilbkv", compiler_params=_params(("arbitrary", "arbitrary")))(
            slopes, k, v, q, q, q, do, do, do, lse_row, lse_row, lse_row, deff_row, deff_row, deff_row)


def _alibi_slopes():
    n = len(DIL_GROUPS) * DIL_HEADS
    s = 2.0 ** (-ALIBI_MAX * (jnp.arange(n, dtype=F32) + 1.0) / n)
    return jnp.broadcast_to(s.reshape(len(DIL_GROUPS), DIL_HEADS, 1, 1), (len(DIL_GROUPS), DIL_HEADS, 1, 128))


def _merge_weights(lses):
    m = jnp.maximum(jnp.maximum(lses[0], lses[1]), lses[2])
    es = [jnp.exp(l - m) for l in lses]
    tot = es[0] + es[1] + es[2]
    return [e / tot for e in es]


def dilated_fwd(tag, z):
    S = z.shape[0]
    H = DIL_HEADS
    ng = len(DIL_GROUPS)
    zd = z[:, :DIL_QKV].reshape(S, ng, 3, H, HEAD_DIM)
    q_mem = z[:, DIL_QKV:]
    slopes = _alibi_slopes()
    qkv, outs, lses = [], [], []
    for g, (_, dil) in enumerate(DIL_GROUPS):
        t = [_dil_perm(zd[:, g, c].transpose(1, 0, 2), dil) for c in range(3)]
        o, lse = dil_fwd(f"{tag}_g{g}", t[0], t[1], t[2], slopes[g], dil)
        qkv.append(t)
        outs.append(_dil_unperm(o, dil))
        lses.append(_dil_unperm(lse, dil))

    def merge(t):
        w = _merge_weights(t[3:6])
        return [w[0] * t[0] + w[1] * t[1] + w[2] * t[2]]

    oh, = hrows_call(tag + "_merge", merge, H, S, 1024, outs + lses, [(HEAD_DIM, BF)])
    return _from_heads(oh), q_mem, dict(qkv=qkv, outs=outs, lses=lses)


def dilated_bwd(tag, do, sv):
    S = do.shape[0]
    H = DIL_HEADS
    slopes = _alibi_slopes()
    doh = _to_heads(do, H)

    def merge_b(t):
        d = t[0].astype(F32)
        w = _merge_weights(t[4:7])
        dws = [jnp.sum(d * t[1 + g], axis=-1, keepdims=True) for g in range(3)]
        c = w[0] * dws[0] + w[1] * dws[1] + w[2] * dws[2]
        return [w[g] * d for g in range(3)] + [w[g] * c for g in range(3)]

    res = hrows_call(tag + "_mergeb", merge_b, H, S, 1024, [doh] + sv['outs'] + sv['lses'],
                     [(HEAD_DIM, BF)] * 3 + [(1, F32)] * 3)
    parts = []
    for g, (_, dil) in enumerate(DIL_GROUPS):
        q, k, v = sv['qkv'][g]
        do_g = _dil_perm(res[g], dil)
        deff = _dil_perm(res[3 + g], dil)
        lse = _dil_perm(sv['lses'][g], dil)
        dq = dil_bwd_q(f"{tag}_g{g}", q, k, v, do_g, lse, deff, slopes[g], dil)
        dk, dv = dil_bwd_kv(f"{tag}_g{g}", q, k, v, do_g, lse.reshape(H, 1, S), deff.reshape(H, 1, S), slopes[g], dil)
        parts.append(jnp.stack([_dil_unperm(t, dil).transpose(1, 0, 2) for t in (dq, dk, dv)], axis=1))
    dzd = jnp.stack(parts, axis=1)
    return [dzd.reshape(S, DIL_QKV).astype(BF)]


def _halo_specs2(ts, w, S):
    r = ts // HALO
    return [pl.BlockSpec((HALO, w), lambda i: (jnp.maximum(i * r - 1, 0), 0)),
            pl.BlockSpec((ts, w), lambda i: (i, 0)),
            pl.BlockSpec((HALO, w), lambda i: (jnp.minimum((i + 1) * r, S // HALO - 1), 0))]


def _cat_rows(i, ts, S, refs):
    v = jnp.concatenate([r[...].astype(F32) for r in refs], axis=0)
    pos = i * ts - HALO + lax.broadcasted_iota(jnp.int32, (ts + 2 * HALO, 1), 0)
    return v, (pos >= 0) & (pos < S)


def _conv_taps(buf_ref, w, ts, width, flip):
    acc = None
    for k in range(width):
        off = HALO + ((width // 2 - k) if flip else (k - width // 2))
        term = buf_ref[pl.ds(off, ts), :] * w[k:k + 1, :]
        acc = term if acc is None else acc + term
    return acc


def conformer_fwd(tag, z, conv_w, conv_b, ln_g, ln_b, ts=256):
    S = z.shape[0]
    ts = min(ts, S)
    C = CONV_CH
    a, gate, q_mem = z[:, :C], z[:, C:2 * C], z[:, 2 * C:]

    def body(ap, ac, an, gp, gc, gn, w_ref, b_ref, lg_ref, lb_ref, u_ref, u1_ref, buf):
        i = pl.program_id(0)
        av, ok = _cat_rows(i, ts, S, (ap, ac, an))
        gv, _ = _cat_rows(i, ts, S, (gp, gc, gn))
        buf[...] = jnp.where(ok, av * jax.nn.sigmoid(gv), 0.0)
        u1 = _conv_taps(buf, w_ref[...], ts, CONV_WIDTH, False) + b_ref[...]
        mu = jnp.mean(u1, axis=-1, keepdims=True)
        xc = u1 - mu
        y = xc * lax.rsqrt(jnp.mean(xc * xc, axis=-1, keepdims=True) + EPS) * lg_ref[...] + lb_ref[...]
        u_ref[...] = _silu(y).astype(u_ref.dtype)
        u1_ref[...] = u1

    full = lambda t: pl.BlockSpec(t.shape, lambda i: (0, 0))
    u, u1 = pl.pallas_call(
        body, grid=(S // ts,),
        in_specs=_halo_specs2(ts, C, S) + _halo_specs2(ts, C, S) + [full(conv_w), full(conv_b), full(ln_g), full(ln_b)],
        out_specs=[pl.BlockSpec((ts, C), lambda i: (i, 0))] * 2, out_shape=[SDS((S, C), BF), SDS((S, C), F32)],
        scratch_shapes=[pltpu.VMEM((ts + 2 * HALO, C), F32)], name=tag + "_conf",
        compiler_params=_params(("arbitrary",)))(a, a, a, gate, gate, gate, conv_w, conv_b, ln_g, ln_b)
    return u, q_mem, dict(a=a, gate=gate, u1=u1)


def conformer_bwd(tag, du, sv, conv_w, conv_b, ln_g, ln_b, ts=256):
    S = du.shape[0]
    ts = min(ts, S)
    C = CONV_CH

    def ln_b_fn(i, t, f):
        u1, d = t[1], t[0].astype(F32)
        mu = jnp.mean(u1, axis=-1, keepdims=True)
        xc = u1 - mu
        r = lax.rsqrt(jnp.mean(xc * xc, axis=-1, keepdims=True) + EPS)
        xh = xc * r
        dy = d * _dsilu(xh * f[0] + f[1])
        dxh = dy * f[0]
        du1 = r * (dxh - jnp.mean(dxh, axis=-1, keepdims=True) - xh * jnp.mean(dxh * xh, axis=-1, keepdims=True))
        return [du1], [jnp.sum(dy * xh, axis=0, keepdims=True), jnp.sum(dy, axis=0, keepdims=True),
                       jnp.sum(du1, axis=0, keepdims=True)]

    du1, dln_g, dln_b, dconv_b = rows_call(tag + "_confb1", ln_b_fn, S, 512, [du, sv['u1']], [ln_g, ln_b],
                                           [(C, F32)], [(1, C), (1, C), (1, C)])
    wpad = CONV_WIDTH + 1

    def body(dp, dc, dn, ap, ac, an, gp, gc, gn, w_ref, da_ref, dg_ref, dw_ref, ubuf, dbuf):
        i = pl.program_id(0)
        av, ok = _cat_rows(i, ts, S, (ap, ac, an))
        gv, _ = _cat_rows(i, ts, S, (gp, gc, gn))
        dv, _ = _cat_rows(i, ts, S, (dp, dc, dn))
        sg = jax.nn.sigmoid(gv)
        ubuf[...] = jnp.where(ok, av * sg, 0.0)
        dbuf[...] = jnp.where(ok, dv, 0.0)
        du0 = _conv_taps(dbuf, w_ref[...], ts, CONV_WIDTH, True)
        a_c, s_c = av[HALO:HALO + ts], sg[HALO:HALO + ts]
        da_ref[...] = (du0 * s_c).astype(da_ref.dtype)
        dg_ref[...] = (du0 * a_c * s_c * (1.0 - s_c)).astype(dg_ref.dtype)
        d_c = dbuf[pl.ds(HALO, ts), :]

        @pl.when(i == 0)
        def _():
            dw_ref[...] = jnp.zeros(dw_ref.shape, F32)

        for k in range(CONV_WIDTH):
            off = HALO + k - CONV_WIDTH // 2
            dw_ref[k:k + 1, :] += jnp.sum(d_c * ubuf[pl.ds(off, ts), :], axis=0, keepdims=True)

    full = lambda t: pl.BlockSpec(t.shape, lambda i: (0, 0))
    da, dgate, dw = pl.pallas_call(
        body, grid=(S // ts,),
        in_specs=_halo_specs2(ts, C, S) * 3 + [full(conv_w)],
        out_specs=[pl.BlockSpec((ts, C), lambda i: (i, 0))] * 2 + [pl.BlockSpec((wpad, C), lambda i: (0, 0))],
        out_shape=[SDS((S, C), BF), SDS((S, C), BF), SDS((wpad, C), F32)],
        scratch_shapes=[pltpu.VMEM((ts + 2 * HALO, C), F32)] * 2, name=tag + "_confb2",
        compiler_params=_params(("arbitrary",)))(du1, du1, du1, sv['a'], sv['a'], sv['a'], sv['gate'], sv['gate'],
                                                 sv['gate'], conv_w)
    grads = dict(c_conv_w=dw[:CONV_WIDTH], c_conv_b=dconv_b, c_ln_g=dln_g, c_ln_b=dln_b)
    return [da, dgate], grads


def shortconv_fwd(tag, z, conv_w, ts=256):
    S = z.shape[0]
    ts = min(ts, S)
    C = SC_CH
    bg, cg, hx, q_mem = z[:, :C], z[:, C:2 * C], z[:, 2 * C:3 * C], z[:, 3 * C:]

    def body(b_ref, cp, cc, cn, hp, hc, hn, w_ref, o_ref, buf):
        i = pl.program_id(0)
        cv, ok = _cat_rows(i, ts, S, (cp, cc, cn))
        hv, _ = _cat_rows(i, ts, S, (hp, hc, hn))
        buf[...] = jnp.where(ok, cv * hv, 0.0)
        o_ref[...] = (b_ref[...].astype(F32) * _conv_taps(buf, w_ref[...], ts, SC_WIDTH, False)).astype(o_ref.dtype)

    o = pl.pallas_call(
        body, grid=(S // ts,),
        in_specs=[pl.BlockSpec((ts, C), lambda i: (i, 0))] + _halo_specs2(ts, C, S) * 2
        + [pl.BlockSpec(conv_w.shape, lambda i: (0, 0))],
        out_specs=pl.BlockSpec((ts, C), lambda i: (i, 0)), out_shape=SDS((S, C), BF),
        scratch_shapes=[pltpu.VMEM((ts + 2 * HALO, C), F32)], name=tag + "_sconv",
        compiler_params=_params(("arbitrary",)))(bg, cg, cg, cg, hx, hx, hx, conv_w)
    return o, q_mem, dict(bg=bg, cg=cg, hx=hx)


def shortconv_bwd(tag, do, sv, conv_w, ts=256):
    S = do.shape[0]
    ts = min(ts, S)
    C = SC_CH
    wpad = 8

    def body(dp, dc, dn, bp, bc, bn, cp, cc, cn, hp, hc, hn, w_ref, db_ref, dcg_ref, dhx_ref, dw_ref, mbuf, dbuf):
        i = pl.program_id(0)
        dov, ok = _cat_rows(i, ts, S, (dp, dc, dn))
        bv, _ = _cat_rows(i, ts, S, (bp, bc, bn))
        cv, _ = _cat_rows(i, ts, S, (cp, cc, cn))
        hv, _ = _cat_rows(i, ts, S, (hp, hc, hn))
        mbuf[...] = jnp.where(ok, cv * hv, 0.0)
        dbuf[...] = jnp.where(ok, dov * bv, 0.0)
        conv = _conv_taps(mbuf, w_ref[...], ts, SC_WIDTH, False)
        db_ref[...] = (dov[HALO:HALO + ts] * conv).astype(db_ref.dtype)
        dm = _conv_taps(dbuf, w_ref[...], ts, SC_WIDTH, True)
        dcg_ref[...] = (dm * hv[HALO:HALO + ts]).astype(dcg_ref.dtype)
        dhx_ref[...] = (dm * cv[HALO:HALO + ts]).astype(dhx_ref.dtype)
        d_c = dbuf[pl.ds(HALO, ts), :]

        @pl.when(i == 0)
        def _():
            dw_ref[...] = jnp.zeros(dw_ref.shape, F32)

        for k in range(SC_WIDTH):
            off = HALO + k - SC_WIDTH // 2
            dw_ref[k:k + 1, :] += jnp.sum(d_c * mbuf[pl.ds(off, ts), :], axis=0, keepdims=True)

    dbg, dcg, dhx, dw = pl.pallas_call(
        body, grid=(S // ts,),
        in_specs=_halo_specs2(ts, C, S) * 4 + [pl.BlockSpec(conv_w.shape, lambda i: (0, 0))],
        out_specs=[pl.BlockSpec((ts, C), lambda i: (i, 0))] * 3 + [pl.BlockSpec((wpad, C), lambda i: (0, 0))],
        out_shape=[SDS((S, C), BF)] * 3 + [SDS((wpad, C), F32)],
        scratch_shapes=[pltpu.VMEM((ts + 2 * HALO, C), F32)] * 2, name=tag + "_sconvb",
        compiler_params=_params(("arbitrary",)))(do, do, do, sv['bg'], sv['bg'], sv['bg'], sv['cg'], sv['cg'],
                                                 sv['cg'], sv['hx'], sv['hx'], sv['hx'], conv_w)
    return [dbg, dcg, dhx], dict(d_conv_w=dw[:SC_WIDTH])


MIXERS = ('a', 'b', 'c', 'd')


def _mixer_weights(W, m):
    return W[m + '_w_in'][0], W[m + '_w_out'][0]


def layer_fwd(i, x, mem, rope, W, gather=None):
    tag = f"l{i}"
    g = W['norm_g'][i]
    gr = lambda k: g[k:k + 1]
    m = MIXERS[i % len(MIXERS)]
    sv = {}
    x1, sv['ffn0'] = ffn_fwd(tag + "_f0", x, gr(0), gr(1), W['ffn_w_gate'][i, 0], W['ffn_w_up'][i, 0],
                             W['ffn_w_down'][i, 0])
    w_in = W[m + '_w_in'][0]
    S = x.shape[0]
    h2 = rms_rows(tag + "_rms2", x1, gr(2))
    n_in = w_in.shape[1]
    z, = mm_call(tag + "_in", [[(h2, w_in, False, False)]], S, n_in, D_MODEL, tm=1024,
                 tn=(256 if n_in % 256 == 0 else n_in), tk=1024, **_plain(BF))
    if m == 'a':
        o, q_mem, sv['mix'], got = mla_fwd(tag, z, rope, W['a_q_norm'], W['a_kv_norm'], W['a_w_uq'][0],
                                           W['a_w_ukv'][0], cargo=(gather[0], False) if gather else None)
        if gather:
            W = gather[1](got)
    elif m == 'b':
        o, q_mem, sv['mix'] = dilated_fwd(tag, z)
    elif m == 'c':
        o, q_mem, sv['mix'] = conformer_fwd(tag, z, W['c_conv_w'][0], W['c_conv_b'], W['c_ln_g'], W['c_ln_b'])
    else:
        o, q_mem, sv['mix'] = shortconv_fwd(tag, z, W['d_conv_w'][0])
    w_out = W[m + '_w_out'][0]
    kv = mem_kv_fwd(tag, mem, gr(6), W['mem_w_kv'][i])
    kbd, vbd = _block_diag_kv(kv)
    mo = mem_attn_fwd(tag, q_mem, kbd, vbd)
    cat = jnp.concatenate([o, mo], axis=1)
    n_out = cat.shape[1]

    def ep_out(a, t, c, f):
        return [a[0], t[0] + _rms(a[0], c[0])], []

    y2, x2 = mm_call(tag + "_out", [[(cat, w_out, False, False)]], S, D_MODEL, n_out, tm=1024, tn=1024, tk=n_out,
                     epilogue=ep_out, outs=[BF, F32], tiles=[x1], cols=[gr(3)])
    sv.update(x1=x1, h2=h2, q_mem=q_mem, kbd=kbd, vbd=vbd, cat=cat, y2=y2)
    x3, sv['ffn1'] = ffn_fwd(tag + "_f1", x2, gr(4), gr(5), W['ffn_w_gate'][i, 1], W['ffn_w_up'][i, 1],
                             W['ffn_w_down'][i, 1])
    return x3, sv, W


def layer_bwd(i, dx3, sv, mem, rope, W, scatter=None):
    tag = f"l{i}"
    g = W['norm_g'][i]
    gr = lambda k: g[k:k + 1]
    m = MIXERS[i % len(MIXERS)]
    S = dx3.shape[0]
    dx2, f1 = ffn_bwd(tag + "_f1", dx3, sv['ffn1'], gr(4), gr(5), W['ffn_w_gate'][i, 1], W['ffn_w_up'][i, 1],
                      W['ffn_w_down'][i, 1])
    w_in, w_out = W[m + '_w_in'][0], W[m + '_w_out'][0]
    n_in, n_out = w_in.shape[1], w_out.shape[0]
    got = None

    def post(ii, t, f):
        dx, dg = _rms_bwd(t[1].astype(F32), f[0], t[0])
        return [dx], [dg]

    dy2, dg3 = rows_call(tag + "_bpost", post, S, 512, [dx2, sv['y2']], [gr(3)], [(D_MODEL, BF)], [(1, D_MODEL)])
    dcat, = mm_call(tag + "_bout", [[(dy2, w_out, False, True)]], S, n_out, D_MODEL, tm=1024,
                    tn=(256 if n_out % 256 == 0 else n_out), tk=1024, **_plain(BF))
    dw_out, = mm_call(tag + "_dwout", [[(sv['cat'], dy2, True, False)]], n_out, D_MODEL, S, tm=n_out, tn=1024, tk=1024,
                      **_plain(F32))
    n_o = n_out - MEM_WIDTH
    do, dmo = dcat[:, :n_o], dcat[:, n_o:]
    dq_mem, dkv = mem_attn_bwd(tag, sv['q_mem'], dmo, sv['kbd'], sv['vbd'])
    dw_kv, dg6 = mem_kv_bwd(tag, mem, gr(6), W['mem_w_kv'][i], dkv)
    if m == 'a':
        cargo = (scatter(dict(ffn1=f1, a_w_out=dw_out[None], mem_w_kv=dw_kv)), True) if scatter else None
        dzs, mg, got = mla_bwd(tag, do, sv['mix'], rope, W['a_q_norm'], W['a_kv_norm'], W['a_w_uq'][0],
                               W['a_w_ukv'][0], cargo=cargo)
    elif m == 'b':
        dzs, mg = dilated_bwd(tag, do, sv['mix']), {}
    elif m == 'c':
        dzs, mg = conformer_bwd(tag, do, sv['mix'], W['c_conv_w'][0], W['c_conv_b'], W['c_ln_g'], W['c_ln_b'])
    else:
        dzs, mg = shortconv_bwd(tag, do, sv['mix'], W['d_conv_w'][0])
    dz = jnp.concatenate(dzs + [dq_mem], axis=1)
    dw_in, = mm_call(tag + "_dwin", [[(sv['h2'], dz, True, False)]], D_MODEL, n_in, S, tm=1024,
                     tn=(512 if n_in % 512 == 0 else (256 if n_in % 256 == 0 else n_in)), tk=1024, **_plain(F32))

    def ep_in(a, t, c, f):
        dx, dg = _rms_bwd(t[0], c[0], a[0])
        return [t[1] + dx], [dg]

    dx1, dg2 = mm_call(tag + "_bin", [[(dz, w_in, False, True)]], S, D_MODEL, n_in, tm=512, tn=1024,
                       tk=(n_in // 2 if n_in % 256 == 0 else n_in), epilogue=ep_in, outs=[F32],
                       tiles=[sv['x1'], dx2], cols=[gr(2)], reds=[1])
    dx0, f0 = ffn_bwd(tag + "_f0", dx1, sv['ffn0'], gr(0), gr(1), W['ffn_w_gate'][i, 0], W['ffn_w_up'][i, 0],
                      W['ffn_w_down'][i, 0])
    dg = jnp.concatenate([f0['g_in'], f0['g_out'], dg2, dg3, f1['g_in'], f1['g_out'], dg6], axis=0)
    grads = dict(norm_g=dg, ffn=[f0, f1], mem_w_kv=dw_kv)
    grads[m + '_w_in'] = dw_in[None]
    grads[m + '_w_out'] = dw_out[None]
    for k, v in mg.items():
        grads[k] = v if k in ('a_q_norm', 'a_kv_norm', 'c_conv_b', 'c_ln_g', 'c_ln_b') else v[None]
    return dx0, grads, got


def local_step(x, mem, positions, target, W, gather=None, scatter=None):
    S = x.shape[0]
    rope = _rope_tables(positions)
    saved = []
    for i in range(DEPTH):
        x, sv, W = layer_fwd(i, x, mem, rope, W, gather if i == 0 else None)
        saved.append(sv)

    def loss_fn(i, t, f):
        e = t[0] - t[1]
        part = jnp.sum(0.5 * jnp.mean(e * e, axis=-1, keepdims=True), axis=0, keepdims=True)
        return [e * (1.0 / D_MODEL)], [jnp.broadcast_to(part, (1, 128))]

    dx, loss = rows_call("loss", loss_fn, S, 512, [x, target], [], [(D_MODEL, F32)], [(1, 128)])
    upper, got = [], None
    for i in reversed(range(DEPTH)):
        hook = (lambda early: scatter(upper, early)) if (scatter and i == 0) else None
        dx, gr, got = layer_bwd(i, dx, saved[i], mem, rope, W, hook)
        upper.insert(0, gr)
    return loss[0, 0], dx, upper, got


def stack_grads(per_layer, names=None):
    grads = {}
    want = lambda n: names is None or n in names
    if want('norm_g'):
        grads['norm_g'] = jnp.stack([p['norm_g'] for p in per_layer])
    if want('mem_w_kv'):
        grads['mem_w_kv'] = jnp.stack([p['mem_w_kv'] for p in per_layer])
    for n, k in (('ffn_w_gate', 'wg'), ('ffn_w_up', 'wu'), ('ffn_w_down', 'wd')):
        if want(n):
            grads[n] = jnp.stack([jnp.stack([f[k] for f in p['ffn']]) for p in per_layer])
    for p in per_layer:
        for n, v in p.items():
            if n not in ('norm_g', 'mem_w_kv', 'ffn') and want(n):
                grads[n] = v
    return grads


def _local_shape(full_shape, axis):
    s = list(full_shape)
    if axis is not None:
        s[axis] //= N_DEV
    return tuple(s)


def _pad_to(n, m):
    return -(-n // m) * m


def pack_local(arrs, dtype, row_mult):
    flat = jnp.concatenate([a.reshape(-1).astype(dtype) for a in arrs])
    n = _pad_to(flat.shape[0], PACK_W * row_mult)
    return jnp.pad(flat, (0, n - flat.shape[0])).reshape(n // PACK_W, PACK_W)


def unpack_local(buf, names, shapes):
    flat = buf.reshape(-1)
    out, off = {}, 0
    for n in names:
        sz = math.prod(shapes[n])
        out[n] = flat[off:off + sz].reshape(shapes[n])
        off += sz
    return out


def unpack_gathered(buf, names, shapes):
    flat = buf.reshape(N_DEV, -1)
    out, off = {}, 0
    for n in names:
        ax = SHARD_AXIS[n]
        sz = math.prod(shapes[n])
        t = flat[:, off:off + sz].reshape((N_DEV,) + tuple(shapes[n]))
        t = jnp.moveaxis(t, 0, ax)
        full = list(shapes[n])
        full[ax] *= N_DEV
        out[n] = t.reshape(full)
        off += sz
    return out


def pack_grads(grads, names, row_mult):
    parts = []
    for n in names:
        g = grads[n].astype(F32)
        ax = SHARD_AXIS[n]
        if ax is None:
            parts.append(jnp.broadcast_to(g.reshape(1, -1), (N_DEV, g.size)))
        else:
            shp = list(g.shape)
            t = g.reshape(shp[:ax] + [N_DEV, shp[ax] // N_DEV] + shp[ax + 1:])
            parts.append(jnp.moveaxis(t, ax, 0).reshape(N_DEV, -1))
    flat = jnp.concatenate(parts, axis=1)
    n = _pad_to(flat.shape[1], PACK_W * row_mult)
    return jnp.pad(flat, ((0, 0), (0, n - flat.shape[1]))).reshape(N_DEV, n // PACK_W, PACK_W)


COL_SHARDED = ['a_w_uq', 'a_w_ukv', 'b_w_in', 'b_w_out', 'c_w_in', 'd_w_in']
ROW_SHARDED_1024 = ['a_w_out', 'c_w_out', 'd_w_out']
FLAT = SMALL + ['a_q_norm', 'a_kv_norm']


EDGE_COLS = ['a_w_uq', 'a_w_ukv']
BULK_COLS = ['b_w_in', 'b_w_out', 'c_w_in', 'd_w_in']
N_FFN = DEPTH * 2


def _pad_cols(t):
    return jnp.pad(t, ((0, 0), (0, FF_SLOT - FF_SHARD)))


def wire_weights_edge(a):
    ws = [_pad_cols(a['ffn_w_gate'][0, 0]), _pad_cols(a['ffn_w_up'][0, 0]), a['ffn_w_down'][0, 0], a['a_w_in'][0]]
    return [t.astype(BF) for t in ws + [a[n][0] for n in EDGE_COLS]]


def wire_weights_bulk(a):
    rest = lambda t, r: t.reshape(N_FFN * r, t.shape[-1])[r:]
    ws = [_pad_cols(rest(a['ffn_w_gate'], D_MODEL)), _pad_cols(rest(a['ffn_w_up'], D_MODEL)),
          jnp.concatenate([rest(a['ffn_w_down'], FF_SHARD)] + [a[n][0] for n in ROW_SHARDED_1024], axis=0),
          a['mem_w_kv'].reshape(-1, 2 * MEM_WIDTH)]
    return [t.astype(BF) for t in ws + [a[n][0] for n in BULK_COLS]]


def _unslot_cols(t):
    return t.transpose(1, 0, 2).reshape(t.shape[1], N_DEV * t.shape[2])


def _unslot_down(t, n):
    t = t.reshape(N_DEV, n, FF_SHARD, D_MODEL)
    t = jnp.pad(t, ((0, 0), (0, 0), (0, FF_SLOT - FF_SHARD), (0, 0)))
    return t.transpose(1, 0, 2, 3).reshape(n, D_FFP, D_MODEL)


def unwire_edge(g):
    W = dict(ffn_w_gate=_unslot_cols(g[0]).reshape(1, 1, D_MODEL, D_FFP),
             ffn_w_up=_unslot_cols(g[1]).reshape(1, 1, D_MODEL, D_FFP),
             ffn_w_down=_unslot_down(g[2], 1).reshape(1, 1, D_FFP, D_MODEL), a_w_in=g[3].reshape(1, D_MODEL, -1))
    for k, n in enumerate(EDGE_COLS):
        W[n] = _unslot_cols(g[4 + k])[None]
    return W


def unwire_bulk(g, edge):
    W = dict(edge)
    for k, n in enumerate(('ffn_w_gate', 'ffn_w_up')):
        t = _unslot_cols(g[k]).reshape(N_FFN - 1, D_MODEL, D_FFP)
        W[n] = jnp.concatenate([edge[n][0], t], axis=0).reshape(DEPTH, 2, D_MODEL, D_FFP)
    n_dn = (N_FFN - 1) * FF_SHARD
    dn = _unslot_down(g[2][:, :n_dn], N_FFN - 1)
    W['ffn_w_down'] = jnp.concatenate([edge['ffn_w_down'][0], dn], axis=0).reshape(DEPTH, 2, D_FFP, D_MODEL)
    rows = D_MODEL // N_DEV
    for k, n in enumerate(ROW_SHARDED_1024):
        W[n] = g[2][:, n_dn + k * rows:n_dn + (k + 1) * rows].reshape(1, D_MODEL, D_MODEL)
    W['mem_w_kv'] = g[3].reshape(N_DEV, DEPTH, rows, 2 * MEM_WIDTH).transpose(1, 0, 2, 3).reshape(DEPTH, D_MODEL, -1)
    for k, n in enumerate(BULK_COLS):
        W[n] = _unslot_cols(g[4 + k])[None]
    return W


def _slot_cols(t):
    return t.reshape(t.shape[0], N_DEV, -1).transpose(1, 0, 2)


def _slot_down(t):
    return t.reshape(N_DEV, FF_SLOT, D_MODEL)[:, :FF_SHARD]


def wire_grads_edge(ffn0, g):
    ws = [_slot_cols(ffn0['wg']), _slot_cols(ffn0['wu']), _slot_down(ffn0['wd']),
          g['a_w_in'][0].reshape(N_DEV, D_MODEL // N_DEV, -1)] + [_slot_cols(g[n][0]) for n in EDGE_COLS]
    return [t.astype(BF) for t in ws]


def wire_grads_bulk(ffns, g, mem_w_kv):
    rows = D_MODEL // N_DEV
    kv = mem_w_kv.reshape(DEPTH, N_DEV, rows, 2 * MEM_WIDTH).transpose(1, 0, 2, 3).reshape(N_DEV, DEPTH * rows, -1)
    ws = [jnp.concatenate([_slot_cols(f['wg']) for f in ffns], axis=1),
          jnp.concatenate([_slot_cols(f['wu']) for f in ffns], axis=1),
          jnp.concatenate([_slot_down(f['wd']) for f in ffns]
                          + [g[n][0].reshape(N_DEV, rows, D_MODEL) for n in ROW_SHARDED_1024], axis=1), kv]
    return [t.astype(BF) for t in ws + [_slot_cols(g[n][0]) for n in BULK_COLS]]


def _peers():
    x, y, c = lax.axis_index("x"), lax.axis_index("y"), lax.axis_index("c")
    me = 4 * x + 2 * y + c
    peers = []
    for k in range(1, N_DEV):
        px = 1 - x if k & 4 else x
        py = 1 - y if k & 2 else y
        pc = 1 - c if k & 1 else c
        peers.append(((px, py, pc), 4 * px + 2 * py + pc))
    return me, peers


def _exchange(name, arrays, scatter):
    n, c_in, c_out, c_shape, c_scratch = _cargo_specs((arrays, scatter))

    def body(*refs):
        start, wait = _exchange_ops(refs[:n], refs[n:2 * n], *refs[2 * n:], scatter)
        start()
        wait()

    return pl.pallas_call(body, out_shape=c_shape, in_specs=c_in, out_specs=c_out, scratch_shapes=c_scratch,
                          name=name)(*arrays)


def _exchange_ops(srcs, outs, send_sems, recv_sems, local_sems, scatter):
    me, peers = _peers()

    def local(a):
        return pltpu.make_async_copy(srcs[a].at[me] if scatter else srcs[a], outs[a].at[me], local_sems.at[a])

    def remote(a, k, landing):
        dev, idx = peers[k]
        return pltpu.make_async_remote_copy(src_ref=srcs[a].at[idx] if scatter else srcs[a],
                                            dst_ref=outs[a].at[idx if landing else me],
                                            send_sem=send_sems.at[a, k], recv_sem=recv_sems.at[a, k],
                                            device_id=dev, device_id_type=pl.DeviceIdType.MESH)

    pairs = [(a, k) for a in range(len(srcs)) for k in range(N_DEV - 1)]

    def start():
        for a in range(len(srcs)):
            local(a).start()
        for a, k in pairs:
            remote(a, k, False).start()

    def wait():
        for a, k in pairs:
            remote(a, k, True).wait_recv()
        for a, k in pairs:
            remote(a, k, False).wait_send()
        for a in range(len(srcs)):
            local(a).wait()

    return start, wait


ADAM_ROWS = 128


def sum_adamw(name, parts, row_off, w, m, v, tr):
    rows, wl = w.shape
    wp = parts.shape[2]
    tr = min(tr, rows)
    assert rows % tr == 0 and row_off % tr == 0, (name, rows, row_off, tr)
    off = row_off // tr

    def body(p_ref, w_ref, m_ref, v_ref, g_out, d_out, m_out, v_out):
        g = p_ref[0].astype(F32)
        for q in range(1, N_DEV):
            g = g + p_ref[q].astype(F32)
        g = g[:, :wl]
        m_new = ADAM_B1 * m_ref[...] + (1.0 - ADAM_B1) * g
        v_new = ADAM_B2 * v_ref[...] + (1.0 - ADAM_B2) * jnp.square(g)
        m_hat = m_new / (1.0 - ADAM_B1 ** ADAM_STEP)
        v_hat = v_new / (1.0 - ADAM_B2 ** ADAM_STEP)
        g_out[...] = g
        d_out[...] = -ADAM_LR * (m_hat / (jnp.sqrt(v_hat) + ADAM_EPS) + ADAM_WD * w_ref[...])
        m_out[...] = m_new
        v_out[...] = v_new

    blk = pl.BlockSpec((tr, wl), lambda i: (i, 0))
    return pl.pallas_call(
        body, grid=(rows // tr,),
        in_specs=[pl.BlockSpec((N_DEV, tr, wp), lambda i: (0, i + off, 0)), blk, blk, blk],
        out_specs=[blk] * 4, out_shape=[SDS((rows, wl), F32)] * 4, name=name,
        compiler_params=_params(("arbitrary",)))(parts, w, m, v)


def kernel(x, mem, positions, norm_g, ffn_w_gate, ffn_w_up, ffn_w_down, mem_w_kv, a_w_in, a_q_norm, a_kv_norm, a_w_uq, a_w_ukv, a_w_out, b_w_in, b_w_out, c_w_in, c_conv_w, c_conv_b, c_ln_g, c_ln_b, c_w_out, d_w_in, d_conv_w, d_w_out, loss_target, m_norm_g, m_ffn_w_gate, m_ffn_w_up, m_ffn_w_down, m_mem_w_kv, m_a_w_in, m_a_q_norm, m_a_kv_norm, m_a_w_uq, m_a_w_ukv, m_a_w_out, m_b_w_in, m_b_w_out, m_c_w_in, m_c_conv_w, m_c_conv_b, m_c_ln_g, m_c_ln_b, m_c_w_out, m_d_w_in, m_d_conv_w, m_d_w_out, v_norm_g, v_ffn_w_gate, v_ffn_w_up, v_ffn_w_down, v_mem_w_kv, v_a_w_in, v_a_q_norm, v_a_kv_norm, v_a_w_uq, v_a_w_ukv, v_a_w_out, v_b_w_in, v_b_w_out, v_c_w_in, v_c_conv_w, v_c_conv_b, v_c_ln_g, v_c_ln_b, v_c_w_out, v_d_w_in, v_d_conv_w, v_d_w_out):
    a = dict(zip(IN_NAMES, (x, mem, positions, norm_g, ffn_w_gate, ffn_w_up, ffn_w_down, mem_w_kv, a_w_in, a_q_norm, a_kv_norm, a_w_uq, a_w_ukv, a_w_out, b_w_in, b_w_out, c_w_in, c_conv_w, c_conv_b, c_ln_g, c_ln_b, c_w_out, d_w_in, d_conv_w, d_w_out, loss_target, m_norm_g, m_ffn_w_gate, m_ffn_w_up, m_ffn_w_down, m_mem_w_kv, m_a_w_in, m_a_q_norm, m_a_kv_norm, m_a_w_uq, m_a_w_ukv, m_a_w_out, m_b_w_in, m_b_w_out, m_c_w_in, m_c_conv_w, m_c_conv_b, m_c_ln_g, m_c_ln_b, m_c_w_out, m_d_w_in, m_d_conv_w, m_d_w_out, v_norm_g, v_ffn_w_gate, v_ffn_w_up, v_ffn_w_down, v_mem_w_kv, v_a_w_in, v_a_q_norm, v_a_kv_norm, v_a_w_uq, v_a_w_ukv, v_a_w_out, v_b_w_in, v_b_w_out, v_c_w_in, v_c_conv_w, v_c_conv_b, v_c_ln_g, v_c_ln_b, v_c_w_out, v_d_w_in, v_d_conv_w, v_d_w_out)))
    shapes = {n: tuple(a[n].shape) for n in WEIGHTS}
    got_edge = _exchange("gather_edge", wire_weights_edge(a) + [pack_local([a[n] for n in SMALL], F32, 8)], False)
    common = unpack_gathered(got_edge[-1], SMALL, shapes)
    common['a_q_norm'], common['a_kv_norm'] = a['a_q_norm'], a['a_kv_norm']
    edge = unwire_edge(got_edge[:-1])

    def bulk_grads(upper, early):
        ffns = [early['ffn1']] + [f for p in upper for f in p['ffn']]
        g = dict(a_w_out=early['a_w_out'])
        for p in upper:
            g.update({n: v for n, v in p.items() if n.endswith(('_w_in', '_w_out'))})
        return wire_grads_bulk(ffns, g, jnp.stack([early['mem_w_kv']] + [p['mem_w_kv'] for p in upper]))

    loss, grad_x, per_layer, got_bulk = local_step(
        a['x'][0], a['mem'][0], a['positions'][0], a['loss_target'][0], {**edge, **common},
        gather=(wire_weights_bulk(a), lambda got: {**unwire_bulk(got, edge), **common}), scatter=bulk_grads)
    flat_grads = stack_grads(per_layer, FLAT)
    got_late = _exchange("scatter_edge", wire_grads_edge(per_layer[0]['ffn'][0], per_layer[0])
                         + [pack_grads(flat_grads, FLAT, ADAM_ROWS)], True)
    res = {}

    def adam(n, p, row_off, rows, tr):
        view = [a[pre + n].reshape(rows, -1) for pre in ('', 'm_', 'v_')]
        res[n] = [t.reshape(shapes[n]) for t in sum_adamw("adamw_" + n, p, row_off, *view, tr)]

    both = lambda k: jnp.concatenate([got_late[k], got_bulk[k]], axis=1)
    n_ff = N_FFN * D_MODEL
    adam('ffn_w_gate', both(0), 0, n_ff, 256)
    adam('ffn_w_up', both(1), 0, n_ff, 256)
    n_dn = N_FFN * FF_SHARD
    rows_1024 = both(2)
    adam('ffn_w_down', rows_1024, 0, n_dn, 128)
    rows = D_MODEL // N_DEV
    for k, n in enumerate(ROW_SHARDED_1024):
        adam(n, rows_1024, n_dn + k * rows, rows, 128)
    adam('mem_w_kv', got_bulk[3], 0, DEPTH * rows, 128)
    adam('a_w_in', got_late[3], 0, rows, 128)
    for k, n in enumerate(EDGE_COLS):
        adam(n, got_late[4 + k], 0, shapes[n][1], 128)
    for k, n in enumerate(BULK_COLS):
        adam(n, got_bulk[4 + k], 0, shapes[n][1], 128)
    packed = [pack_local([a[pre + n] for n in FLAT], F32, ADAM_ROWS) for pre in ('', 'm_', 'v_')]
    flat = [unpack_local(r, FLAT, shapes) for r in sum_adamw("adamw_flat", got_late[-1], 0, *packed, ADAM_ROWS)]
    for n in FLAT:
        res[n] = [f[n] for f in flat]
    loss = lax.psum(loss, MESH_AXES)
    return (loss, grad_x[None], *[res[n][k] for k in range(4) for n in WEIGHTS])
```

```python
import functools
import math

import jax
import jax.numpy as jnp
from jax import lax
from jax.experimental import pallas as pl
from jax.experimental.pallas import tpu as pltpu

BF, F32 = jnp.bfloat16, jnp.float32
SDS = jax.ShapeDtypeStruct

D_MODEL, DEPTH, D_FF = 1024, 4, 2816
HEAD_DIM, MEM_HEADS, MEM_WIDTH, N_MEM = 64, 4, 256, 256
MLA_HEADS, MLA_Q_LORA, MLA_KV_LORA, MLA_NOPE, MLA_ROPE, MLA_V = 12, 384, 256, 64, 32, 64
MLA_QK = MLA_NOPE + MLA_ROPE
ROPE_THETA = 10000.0
DIL_GROUPS = ((128, 1), (512, 4), (2048, 16))
DIL_HEADS, DIL_HALF = 8, 64
DIL_QKV = len(DIL_GROUPS) * 3 * DIL_HEADS * HEAD_DIM
ALIBI_MAX = 8.0
CONV_CH, CONV_WIDTH, SC_CH, SC_WIDTH = 768, 31, 768, 3
EPS, NEG = 1e-6, -1e30
ADAM_LR, ADAM_B1, ADAM_B2, ADAM_EPS, ADAM_WD, ADAM_STEP = 0.001, 0.9, 0.999, 1e-08, 0.01, 10

N_DEV = 8
MESH_AXES = ("x", "y", "c")
FF_SHARD = D_FF // N_DEV
FF_SLOT = 384
D_FFP = N_DEV * FF_SLOT
LOG2E = 1.4426950408889634
V7X_VMEM_LIMIT = 48 * 1024 * 1024
HALO = 16
PACK_W = 1024

WEIGHTS = ['norm_g', 'ffn_w_gate', 'ffn_w_up', 'ffn_w_down', 'mem_w_kv', 'a_w_in', 'a_q_norm', 'a_kv_norm',
           'a_w_uq', 'a_w_ukv', 'a_w_out', 'b_w_in', 'b_w_out', 'c_w_in', 'c_conv_w', 'c_conv_b', 'c_ln_g',
           'c_ln_b', 'c_w_out', 'd_w_in', 'd_conv_w', 'd_w_out']
SHARD_AXIS = dict(norm_g=2, ffn_w_gate=3, ffn_w_up=3, ffn_w_down=2, mem_w_kv=1, a_w_in=1, a_q_norm=None,
                  a_kv_norm=None, a_w_uq=2, a_w_ukv=2, a_w_out=1, b_w_in=2, b_w_out=2, c_w_in=2, c_conv_w=2,
                  c_conv_b=1, c_ln_g=1, c_ln_b=1, c_w_out=1, d_w_in=2, d_conv_w=2, d_w_out=1)
SMALL = ['norm_g', 'c_conv_w', 'c_conv_b', 'c_ln_g', 'c_ln_b', 'd_conv_w']
BIG = [n for n in WEIGHTS if n not in SMALL and SHARD_AXIS[n] is not None]
IN_NAMES = (['x', 'mem', 'positions'] + WEIGHTS + ['loss_target'] + ['m_' + n for n in WEIGHTS]
            + ['v_' + n for n in WEIGHTS])


def _params(sem):
    return pltpu.CompilerParams(dimension_semantics=sem, vmem_limit_bytes=V7X_VMEM_LIMIT)


def _dot(a, b, ca=1, cb=0):
    return lax.dot_general(a, b, (((ca,), (cb,)), ((), ())), preferred_element_type=F32)


def mm_call(name, accs, M, N, K, *, tm, tn, tk, epilogue, outs, tiles=(), cols=(), fulls=(), reds=()):
    tm, tn, tk = min(tm, M), min(tn, N), min(tk, K)
    assert M % tm == 0 and N % tn == 0 and K % tk == 0, (name, M, N, K, tm, tn, tk)
    gj, gi, gk = N // tn, M // tm, K // tk
    flat = [p for acc in accs for p in acc]
    in_specs, args = [], []
    for (a, b, ta, tb) in flat:
        in_specs.append(pl.BlockSpec((tk, tm), lambda j, i, k: (k, i)) if ta else pl.BlockSpec((tm, tk), lambda j, i, k: (i, k)))
        in_specs.append(pl.BlockSpec((tn, tk), lambda j, i, k: (j, k)) if tb else pl.BlockSpec((tk, tn), lambda j, i, k: (k, j)))
        args += [a, b]
    for t in tiles:
        in_specs.append(pl.BlockSpec((tm, tn), lambda j, i, k: (i, j)))
        args.append(t)
    for c in cols:
        in_specs.append(pl.BlockSpec((c.shape[0], tn), lambda j, i, k: (0, j)))
        args.append(c)
    for f in fulls:
        in_specs.append(pl.BlockSpec(f.shape, lambda j, i, k: (0,) * f.ndim))
        args.append(f)
    out_shape = [SDS((M, N), dt) for dt in outs] + [SDS((r, N), F32) for r in reds]
    out_specs = ([pl.BlockSpec((tm, tn), lambda j, i, k: (i, j)) for _ in outs]
                 + [pl.BlockSpec((r, tn), lambda j, i, k: (0, j)) for r in reds])
    n_in = len(args)
    n_acc = len(accs)

    def body(*refs):
        in_refs = refs[:n_in]
        out_refs = refs[n_in:n_in + len(outs)]
        red_refs = refs[n_in + len(outs):n_in + len(outs) + len(reds)]
        acc_refs = refs[n_in + len(outs) + len(reds):]
        i, k = pl.program_id(1), pl.program_id(2)

        def product(idx):
            _, _, ta, tb = flat[idx]
            av = in_refs[2 * idx][...].astype(BF)
            bv = in_refs[2 * idx + 1][...].astype(BF)
            return _dot(av, bv, 0 if ta else 1, 1 if tb else 0)

        def finish(vals):
            p = 2 * len(flat)
            tv = [r[...] for r in in_refs[p:p + len(tiles)]]
            cv = [r[...] for r in in_refs[p + len(tiles):p + len(tiles) + len(cols)]]
            fv = [r[...] for r in in_refs[p + len(tiles) + len(cols):]]
            ov, rv = epilogue(vals, tv, cv, fv)
            for r, v in zip(out_refs, ov):
                r[...] = v.astype(r.dtype)
            for r, v in zip(red_refs, rv):
                @pl.when(i == 0)
                def _(r=r, v=v):
                    r[...] = v

                @pl.when(i > 0)
                def _(r=r, v=v):
                    r[...] += v

        starts = [sum(len(acc) for acc in accs[:n]) for n in range(n_acc)]
        if gk == 1:
            vals = []
            for s, acc in zip(starts, accs):
                tot = product(s)
                for p in range(1, len(acc)):
                    tot = tot + product(s + p)
                vals.append(tot)
            finish(vals)
        else:
            @pl.when(k == 0)
            def _():
                for r in acc_refs:
                    r[...] = jnp.zeros(r.shape, F32)

            for r, s, acc in zip(acc_refs, starts, accs):
                for p in range(len(acc)):
                    r[...] += product(s + p)

            @pl.when(k == gk - 1)
            def _():
                finish([r[...] for r in acc_refs])

    scratch = [] if gk == 1 else [pltpu.VMEM((tm, tn), F32) for _ in range(n_acc)]
    res = pl.pallas_call(body, grid=(gj, gi, gk), in_specs=in_specs, out_specs=out_specs, out_shape=out_shape,
                         scratch_shapes=scratch, name=name,
                         compiler_params=_params(("arbitrary", "arbitrary", "arbitrary")))(*args)
    return res


def _plain(dt):
    return dict(epilogue=lambda a, t, c, f: ([a[0]], []), outs=[dt])


def rows_call(name, fn, S, ts, tiled, fulls, outs, reds=()):
    ts = min(ts, S)
    assert S % ts == 0, (name, S, ts)
    in_specs = ([pl.BlockSpec((ts, t.shape[1]), lambda i: (i, 0)) for t in tiled]
                + [pl.BlockSpec(f.shape, lambda i, n=f.ndim: (0,) * n) for f in fulls])
    out_shape = [SDS((S, w), dt) for (w, dt) in outs] + [SDS(rw, F32) for rw in reds]
    out_specs = ([pl.BlockSpec((ts, w), lambda i: (i, 0)) for (w, _) in outs]
                 + [pl.BlockSpec(rw, lambda i: (0, 0)) for rw in reds])
    nt, nf, no = len(tiled), len(fulls), len(outs)

    def body(*refs):
        i = pl.program_id(0)
        tv = [r[...] for r in refs[:nt]]
        fv = [r[...] for r in refs[nt:nt + nf]]
        ov, rv = fn(i, tv, fv)
        for r, v in zip(refs[nt + nf:nt + nf + no], ov):
            r[...] = v.astype(r.dtype)
        for r, v in zip(refs[nt + nf + no:], rv):
            @pl.when(i == 0)
            def _(r=r, v=v):
                r[...] = v

            @pl.when(i > 0)
            def _(r=r, v=v):
                r[...] += v

    return pl.pallas_call(body, grid=(S // ts,), in_specs=in_specs, out_specs=out_specs, out_shape=out_shape,
                          name=name, compiler_params=_params(("arbitrary",)))(*tiled, *fulls)


def hrows_call(name, fn, H, S, ts, tiled, outs):
    ts = min(ts, S)
    in_specs = [pl.BlockSpec((1, ts, t.shape[2]), lambda h, i: (h, i, 0)) for t in tiled]
    out_shape = [SDS((H, S, w), dt) for (w, dt) in outs]
    out_specs = [pl.BlockSpec((1, ts, w), lambda h, i: (h, i, 0)) for (w, _) in outs]
    nt = len(tiled)

    def body(*refs):
        ov = fn([r[0] for r in refs[:nt]])
        for r, v in zip(refs[nt:], ov):
            r[0] = v.astype(r.dtype)

    return pl.pallas_call(body, grid=(H, S // ts), in_specs=in_specs, out_specs=out_specs, out_shape=out_shape,
                          name=name, compiler_params=_params(("arbitrary", "arbitrary")))(*tiled)


def _rms(x, g):
    r = lax.rsqrt(jnp.mean(x * x, axis=-1, keepdims=True) + EPS)
    return x * r * g


def _rms_bwd(x, g, dy):
    r = lax.rsqrt(jnp.mean(x * x, axis=-1, keepdims=True) + EPS)
    xh = x * r
    dg = jnp.sum(dy * xh, axis=0, keepdims=True)
    dxh = dy * g
    dx = r * (dxh - xh * jnp.mean(dxh * xh, axis=-1, keepdims=True))
    return dx, dg


def _silu(x):
    return x * jax.nn.sigmoid(x)


def _dsilu(x):
    s = jax.nn.sigmoid(x)
    return s * (1.0 + x * (1.0 - s))


def rms_rows(name, x, g):
    S, w = x.shape
    return rows_call(name, lambda i, t, f: ([_rms(t[0].astype(F32), f[0])], []), S, 512, [x], [g], [(w, BF)])[0]


def ffn_fwd(tag, x0, g_in, g_out, wg, wu, wd):
    S = x0.shape[0]
    h = rms_rows(tag + "_rms", x0, g_in)

    def ep_up(a, t, c, f):
        return [a[0], a[1], _silu(a[0]) * a[1]], []

    gate, up, act = mm_call(tag + "_up", [[(h, wg, False, False)], [(h, wu, False, False)]], S, D_FFP, D_MODEL,
                            tm=1024, tn=512, tk=1024, epilogue=ep_up, outs=[BF, BF, BF])

    def ep_down(a, t, c, f):
        return [a[0], t[0] + 0.5 * _rms(a[0], c[0])], []

    y, x1 = mm_call(tag + "_down", [[(act, wd, False, False)]], S, D_MODEL, D_FFP, tm=512, tn=1024, tk=1024,
                    epilogue=ep_down, outs=[BF, F32], tiles=[x0], cols=[g_out])
    return x1, dict(x0=x0, h=h, gate=gate, up=up, act=act, y=y)


def ffn_bwd(tag, dx1, sv, g_in, g_out, wg, wu, wd):
    S = dx1.shape[0]

    def post(i, t, f):
        dx, dg = _rms_bwd(t[1].astype(F32), f[0], 0.5 * t[0])
        return [dx], [dg]

    da, dg_out = rows_call(tag + "_bpost", post, S, 512, [dx1, sv['y']], [g_out], [(D_MODEL, BF)], [(1, D_MODEL)])

    def ep_act(a, t, c, f):
        g, u = t[0].astype(F32), t[1].astype(F32)
        sg = jax.nn.sigmoid(g)
        return [a[0] * u * (sg * (1.0 + g * (1.0 - sg))), a[0] * (g * sg)], []

    dgate, dup = mm_call(tag + "_bact", [[(da, wd, False, True)]], S, D_FFP, D_MODEL, tm=1024, tn=512, tk=1024,
                         epilogue=ep_act, outs=[BF, BF], tiles=[sv['gate'], sv['up']])
    dwd, = mm_call(tag + "_dwd", [[(sv['act'], da, True, False)]], D_FFP, D_MODEL, S, tm=1024, tn=1024, tk=1024,
                   **_plain(F32))
    dwg, dwu = mm_call(tag + "_dwgu", [[(sv['h'], dgate, True, False)], [(sv['h'], dup, True, False)]],
                       D_MODEL, D_FFP, S, tm=1024, tn=1024, tk=1024,
                       epilogue=lambda a, t, c, f: ([a[0], a[1]], []), outs=[F32, F32])

    def ep_in(a, t, c, f):
        dx, dg = _rms_bwd(t[0], c[0], a[0])
        return [t[1] + dx], [dg]

    (dx0, dg_in) = mm_call(tag + "_bin", [[(dgate, wg, False, True), (dup, wu, False, True)]], S, D_MODEL, D_FFP,
                           tm=512, tn=1024, tk=1024, epilogue=ep_in, outs=[F32], tiles=[sv['x0'], dx1], cols=[g_in],
                           reds=[1])
    return dx0, dict(g_in=dg_in, g_out=dg_out, wg=dwg, wu=dwu, wd=dwd)


def mem_kv_fwd(tag, mem, g6, w_kv):
    def body(mem_ref, g_ref, w_ref, kv_ref):
        kv_ref[...] = _dot(_rms(mem_ref[...], g_ref[...]).astype(BF), w_ref[...])

    return pl.pallas_call(body, out_shape=SDS((N_MEM, 2 * MEM_WIDTH), F32), name=tag + "_memkv",
                          compiler_params=pltpu.CompilerParams(vmem_limit_bytes=V7X_VMEM_LIMIT))(mem, g6, w_kv)


def mem_kv_bwd(tag, mem, g6, w_kv, dkv):
    def body(mem_ref, g_ref, w_ref, dkv_ref, dw_ref, dg_ref):
        x = mem_ref[...]
        r = lax.rsqrt(jnp.mean(x * x, axis=-1, keepdims=True) + EPS)
        xh = x * r
        d = dkv_ref[...].astype(BF)
        dw_ref[...] = _dot((xh * g_ref[...]).astype(BF), d, 0, 0)
        dmem_n = _dot(d, w_ref[...], 1, 1)
        dg_ref[...] = jnp.sum(dmem_n * xh, axis=0, keepdims=True)

    return pl.pallas_call(body, out_shape=(SDS((D_MODEL, 2 * MEM_WIDTH), F32), SDS((1, D_MODEL), F32)),
                          name=tag + "_memkvb",
                          compiler_params=pltpu.CompilerParams(vmem_limit_bytes=V7X_VMEM_LIMIT))(mem, g6, w_kv, dkv)


def _block_diag_kv(kv):
    k = kv[:, :MEM_WIDTH].reshape(N_MEM, MEM_HEADS, HEAD_DIM)
    v = kv[:, MEM_WIDTH:].reshape(N_MEM, MEM_HEADS, HEAD_DIM)
    eye = jnp.eye(MEM_HEADS, dtype=kv.dtype)
    kbd = jnp.einsum('nhd,hg->hdgn', k, eye).reshape(MEM_WIDTH, MEM_HEADS * N_MEM)
    vbd = jnp.einsum('nhd,hg->hngd', v, eye).reshape(MEM_HEADS * N_MEM, MEM_WIDTH)
    return kbd.astype(BF), vbd.astype(BF)


def _mem_probs(q, kbd):
    s = _dot(q, kbd) * (HEAD_DIM ** -0.5)
    ps = []
    for h in range(MEM_HEADS):
        sh = s[:, h * N_MEM:(h + 1) * N_MEM]
        e = jnp.exp(sh - jnp.max(sh, axis=-1, keepdims=True))
        ps.append(e / jnp.sum(e, axis=-1, keepdims=True))
    return ps


def mem_attn_fwd(tag, q_mem, kbd, vbd):
    S = q_mem.shape[0]

    def fn(i, t, f):
        p = jnp.concatenate(_mem_probs(t[0], f[0]), axis=1).astype(BF)
        return [_dot(p, f[1])], []

    return rows_call(tag + "_mem", fn, S, 512, [q_mem], [kbd, vbd], [(MEM_WIDTH, BF)])[0]


def mem_attn_bwd(tag, q_mem, dmo, kbd, vbd):
    S = q_mem.shape[0]

    def fn(i, t, f):
        q, do = t
        ps = _mem_probs(q, f[0])
        dp = _dot(do, f[1], 1, 1)
        dss = []
        for h in range(MEM_HEADS):
            dph = dp[:, h * N_MEM:(h + 1) * N_MEM]
            dss.append(ps[h] * (dph - jnp.sum(dph * ps[h], axis=-1, keepdims=True)) * (HEAD_DIM ** -0.5))
        ds = jnp.concatenate(dss, axis=1).astype(BF)
        p = jnp.concatenate(ps, axis=1).astype(BF)
        return [_dot(ds, f[0], 1, 1)], [_dot(q, ds, 0, 0), _dot(p, do, 0, 0)]

    dq, dkbd, dvbd = rows_call(tag + "_memb", fn, S, 512, [q_mem, dmo], [kbd, vbd], [(MEM_WIDTH, BF)],
                               [(MEM_WIDTH, MEM_HEADS * N_MEM), (MEM_HEADS * N_MEM, MEM_WIDTH)])
    dk = jnp.einsum('hdhn->nhd', dkbd.reshape(MEM_HEADS, HEAD_DIM, MEM_HEADS, N_MEM)).reshape(N_MEM, MEM_WIDTH)
    dv = jnp.einsum('hnhd->nhd', dvbd.reshape(MEM_HEADS, N_MEM, MEM_HEADS, HEAD_DIM)).reshape(N_MEM, MEM_WIDTH)
    return dq, jnp.concatenate([dk, dv], axis=1)


def _rope_tables(positions):
    half = MLA_ROPE // 2
    inv = ROPE_THETA ** (-jnp.arange(half, dtype=F32) / half)
    ang = positions.astype(F32)[:, None] * inv
    cos, sin = jnp.cos(ang), jnp.sin(ang)
    cos32, sin32 = jnp.concatenate([cos, cos], 1), jnp.concatenate([sin, sin], 1)
    idx = jnp.arange(MLA_ROPE)
    p32 = (jnp.where((idx[:, None] == idx[None, :] + half), -1.0, 0.0)
           + jnp.where((idx[:, None] + half == idx[None, :]), 1.0, 0.0)).astype(F32)
    S = positions.shape[0]
    ones = jnp.ones((S, MLA_NOPE), F32)
    cq = jnp.tile(jnp.concatenate([ones, cos32], 1), (1, MLA_HEADS))
    sq = jnp.tile(jnp.concatenate([0.0 * ones, sin32], 1), (1, MLA_HEADS))
    pq = jnp.zeros((MLA_QK, MLA_QK), F32).at[MLA_NOPE:, MLA_NOPE:].set(p32)
    pq = jnp.kron(jnp.eye(MLA_HEADS, dtype=F32), pq)
    return dict(cos32=cos32, sin32=sin32, p32=p32.astype(BF), p32t=p32.T.astype(BF), cq=cq, sq=sq,
                pq=pq.astype(BF), pqt=pq.T.astype(BF))


def _cargo_specs(cargo):
    if cargo is None:
        return 0, [], [], [], []
    arrays, scatter = cargo
    n = len(arrays)
    hbm = pl.BlockSpec(memory_space=pltpu.HBM)
    out_shape = [SDS(t.shape if scatter else (N_DEV,) + t.shape, t.dtype) for t in arrays]
    scratch = [pltpu.SemaphoreType.DMA((n, N_DEV - 1)), pltpu.SemaphoreType.DMA((n, N_DEV - 1)),
               pltpu.SemaphoreType.DMA((n,))]
    return n, [hbm] * n, [hbm] * n, out_shape, scratch


def flash_fwd(tag, qh, kh, vh1, dv, tq=1024, tk=2048, cargo=None):
    H, S, dq = qh.shape
    dv1 = vh1.shape[2]
    tq, tk = min(tq, S), min(tk, S)
    nk = S // tk
    nc, c_in, c_out, c_shape, c_scratch = _cargo_specs(cargo)

    def body(*refs):
        q_ref, k_ref, v_ref = refs[:3]
        o_ref, lse_ref = refs[3 + nc:5 + nc]
        m_s, acc_s = refs[5 + 2 * nc:7 + 2 * nc]
        h, i, j = pl.program_id(0), pl.program_id(1), pl.program_id(2)
        if nc:
            start, wait = _exchange_ops(refs[3:3 + nc], refs[5 + nc:5 + 2 * nc], *refs[7 + 2 * nc:], cargo[1])
            pl.when((h == 0) & (i == 0) & (j == 0))(start)

        @pl.when(j == 0)
        def _():
            m_s[...] = jnp.full(m_s.shape, NEG, F32)
            acc_s[...] = jnp.zeros(acc_s.shape, F32)

        s = _dot(q_ref[0], k_ref[0], 1, 1)
        m_old = m_s[...]
        m_new = jnp.maximum(m_old, jnp.max(s, axis=-1, keepdims=True))
        p = jnp.exp2((s - m_new).astype(BF))
        acc_s[...] = jnp.exp2(m_old - m_new) * acc_s[...] + _dot(p, v_ref[0])
        m_s[...] = m_new

        @pl.when(j == nk - 1)
        def _():
            acc = acc_s[...]
            l = acc[:, dv:dv + 1]
            o_ref[0] = (acc[:, :dv] / l).astype(o_ref.dtype)
            lse_ref[0] = m_s[...] + jnp.log2(l)

        if nc:
            pl.when((h == H - 1) & (i == S // tq - 1) & (j == nk - 1))(wait)

    res = pl.pallas_call(
        body, grid=(H, S // tq, nk),
        in_specs=[pl.BlockSpec((1, tq, dq), lambda h, i, j: (h, i, 0)),
                  pl.BlockSpec((1, tk, dq), lambda h, i, j: (h, j, 0)),
                  pl.BlockSpec((1, tk, dv1), lambda h, i, j: (h, j, 0))] + c_in,
        out_specs=[pl.BlockSpec((1, tq, dv), lambda h, i, j: (h, i, 0)),
                   pl.BlockSpec((1, tq, 1), lambda h, i, j: (h, i, 0))] + c_out,
        out_shape=[SDS((H, S, dv), BF), SDS((H, S, 1), F32)] + c_shape,
        scratch_shapes=[pltpu.VMEM((tq, 1), F32), pltpu.VMEM((tq, dv1), F32)] + c_scratch,
        name=tag + "_flash", compiler_params=_params(("arbitrary", "arbitrary", "arbitrary")))(
            qh, kh, vh1, *(cargo[0] if nc else []))
    return res[0], res[1], list(res[2:])


def flash_bwd(tag, qh, kh, vh, doh, lse_row, delta_row, tq=1024, tk=1024, cargo=None):
    H, S, dq = qh.shape
    dv = vh.shape[2]
    tq, tk = min(tq, S), min(tk, S)
    nc, c_in, c_out, c_shape, c_scratch = _cargo_specs(cargo)

    def body(*refs):
        k_ref, v_ref, q_ref, do_ref, lse_ref, dl_ref = refs[:6]
        dk_ref, dv_ref, dq_ref = refs[6 + nc:9 + nc]
        h, j, i = pl.program_id(0), pl.program_id(1), pl.program_id(2)
        if nc:
            start, wait = _exchange_ops(refs[6:6 + nc], refs[9 + nc:9 + 2 * nc], *refs[9 + 2 * nc:], cargo[1])
            pl.when((h == 0) & (j == 0) & (i == 0))(start)
        k, v, q, do = k_ref[0], v_ref[0], q_ref[0], do_ref[0]
        pt = jnp.exp2((_dot(k, q, 1, 1) - lse_ref[0]).astype(BF))
        dst = pt * (_dot(v, do, 1, 1) - dl_ref[0]).astype(BF)
        dvc = _dot(pt, do)
        dkc = _dot(dst, q) * (1.0 / LOG2E)
        dqc = _dot(dst, k, 0, 0)

        @pl.when(i == 0)
        def _():
            dk_ref[0] = dkc
            dv_ref[0] = dvc

        @pl.when(i > 0)
        def _():
            dk_ref[0] += dkc
            dv_ref[0] += dvc

        rows = pl.ds(pl.multiple_of(i * tq, tq), tq)

        @pl.when(j == 0)
        def _():
            dq_ref[0, rows, :] = dqc

        @pl.when(j > 0)
        def _():
            dq_ref[0, rows, :] += dqc

        if nc:
            pl.when((h == H - 1) & (j == S // tk - 1) & (i == S // tq - 1))(wait)

    res = pl.pallas_call(
        body, grid=(H, S // tk, S // tq),
        in_specs=[pl.BlockSpec((1, tk, dq), lambda h, j, i: (h, j, 0)),
                  pl.BlockSpec((1, tk, dv), lambda h, j, i: (h, j, 0)),
                  pl.BlockSpec((1, tq, dq), lambda h, j, i: (h, i, 0)),
                  pl.BlockSpec((1, tq, dv), lambda h, j, i: (h, i, 0)),
                  pl.BlockSpec((1, 1, tq), lambda h, j, i: (h, 0, i)),
                  pl.BlockSpec((1, 1, tq), lambda h, j, i: (h, 0, i))] + c_in,
        out_specs=[pl.BlockSpec((1, tk, dq), lambda h, j, i: (h, j, 0)),
                   pl.BlockSpec((1, tk, dv), lambda h, j, i: (h, j, 0)),
                   pl.BlockSpec((1, S, dq), lambda h, j, i: (h, 0, 0))] + c_out,
        out_shape=[SDS((H, S, dq), F32), SDS((H, S, dv), F32), SDS((H, S, dq), F32)] + c_shape,
        scratch_shapes=c_scratch, name=tag + "_flashb",
        compiler_params=_params(("arbitrary", "arbitrary", "arbitrary")))(
            kh, vh, qh, doh, lse_row, delta_row, *(cargo[0] if nc else []))
    return res[0], res[1], res[2], list(res[3:])


def _to_heads(t, H):
    return t.reshape(t.shape[0], H, -1).transpose(1, 0, 2)


def _from_heads(t):
    return t.transpose(1, 0, 2).reshape(t.shape[1], -1)


def mla_fwd(tag, z, rope, q_g, kv_g, w_uq, w_ukv, cargo=None):
    S = z.shape[0]
    o1, o2, o3 = MLA_Q_LORA, MLA_Q_LORA + MLA_KV_LORA, MLA_Q_LORA + MLA_KV_LORA + MLA_ROPE
    c_q, c_kv, k_r, q_mem = z[:, :o1], z[:, o1:o2], z[:, o2:o3], z[:, o3:]

    def pre(i, t, f):
        kr = t[2].astype(F32)
        krot = kr * t[3] + _dot(t[2], f[2]) * t[4]
        return [_rms(t[0].astype(F32), f[0]), _rms(t[1].astype(F32), f[1]), krot], []

    qn, kvn, krot = rows_call(tag + "_pre", pre, S, 512, [c_q, c_kv, k_r, rope['cos32'], rope['sin32']],
                              [q_g, kv_g, rope['p32']], [(MLA_Q_LORA, BF), (MLA_KV_LORA, BF), (MLA_ROPE, BF)])

    qscale = MLA_QK ** -0.5 * LOG2E

    def ep_q(a, t, c, f):
        return [(a[0] * t[0] + _dot(a[0].astype(BF), f[0]) * t[1]) * qscale], []

    nq = MLA_HEADS * MLA_QK
    q, = mm_call(tag + "_uq", [[(qn, w_uq, False, False)]], S, nq, MLA_Q_LORA, tm=512, tn=nq, tk=MLA_Q_LORA,
                 epilogue=ep_q, outs=[BF], tiles=[rope['cq'], rope['sq']], fulls=[rope['pq']])
    nkv = MLA_HEADS * (MLA_NOPE + MLA_V)
    kv, = mm_call(tag + "_ukv", [[(kvn, w_ukv, False, False)]], S, nkv, MLA_KV_LORA, tm=1024, tn=nkv, tk=MLA_KV_LORA,
                  **_plain(BF))
    qh = _to_heads(q, MLA_HEADS)
    kv3 = kv.reshape(S, MLA_HEADS, MLA_NOPE + MLA_V)
    kh = jnp.concatenate([kv3[:, :, :MLA_NOPE].transpose(1, 0, 2),
                          jnp.broadcast_to(krot[None], (MLA_HEADS, S, MLA_ROPE))], axis=2)
    vh = kv3[:, :, MLA_NOPE:].transpose(1, 0, 2)
    ones = jnp.ones((MLA_HEADS, S, 1), BF)
    vh1 = jnp.concatenate([vh, ones, jnp.zeros((MLA_HEADS, S, 128 - MLA_V - 1), BF)], axis=2)
    oh, lse, got = flash_fwd(tag, qh, kh, vh1, MLA_V, cargo=cargo)
    sv = dict(c_q=c_q, c_kv=c_kv, k_r=k_r, qn=qn, kvn=kvn, qh=qh, kh=kh, vh=vh, oh=oh, lse=lse)
    return _from_heads(oh), q_mem, sv, got


def mla_bwd(tag, do, sv, rope, q_g, kv_g, w_uq, w_ukv, cargo=None):
    S = do.shape[0]
    H = MLA_HEADS
    doh = _to_heads(do, H)
    delta, = hrows_call(tag + "_delta",
                        lambda t: [jnp.sum(t[0].astype(F32) * t[1].astype(F32), axis=-1, keepdims=True)],
                        H, S, 1024, [doh, sv['oh']], [(1, F32)])
    dkh, dvh, dqh, got = flash_bwd(tag, sv['qh'], sv['kh'], sv['vh'], doh, sv['lse'].reshape(H, 1, S),
                                   delta.reshape(H, 1, S), cargo=cargo)
    nq = H * MLA_QK
    dq_rot = _from_heads(dqh)
    scale = MLA_QK ** -0.5

    def unrope(i, t, f):
        d = t[0] * scale
        return [d * t[1] + _dot((d * t[2]).astype(BF), f[0])], []

    dq, = rows_call(tag + "_unrope", unrope, S, 512, [dq_rot, rope['cq'], rope['sq']], [rope['pqt']], [(nq, BF)])

    def ep_norm(a, t, c, f):
        dx, dg = _rms_bwd(t[0].astype(F32), c[0], a[0])
        return [dx], [dg]

    dc_q, dq_g = mm_call(tag + "_bqn", [[(dq, w_uq, False, True)]], S, MLA_Q_LORA, nq, tm=1024, tn=MLA_Q_LORA, tk=nq,
                         epilogue=ep_norm, outs=[BF], tiles=[sv['c_q']], cols=[q_g], reds=[1])
    dw_uq, = mm_call(tag + "_dwuq", [[(sv['qn'], dq, True, False)]], MLA_Q_LORA, nq, S, tm=MLA_Q_LORA, tn=nq, tk=1024,
                     **_plain(F32))
    nkv = H * (MLA_NOPE + MLA_V)
    dkv = jnp.concatenate([dkh[:, :, :MLA_NOPE], dvh], axis=2).transpose(1, 0, 2).reshape(S, nkv).astype(BF)
    dc_kv, dkv_g = mm_call(tag + "_bkvn", [[(dkv, w_ukv, False, True)]], S, MLA_KV_LORA, nkv, tm=1024, tn=MLA_KV_LORA,
                           tk=nkv, epilogue=ep_norm, outs=[BF], tiles=[sv['c_kv']], cols=[kv_g], reds=[1])
    dw_ukv, = mm_call(tag + "_dwukv", [[(sv['kvn'], dkv, True, False)]], MLA_KV_LORA, nkv, S, tm=MLA_KV_LORA, tn=nkv,
                      tk=1024, **_plain(F32))
    dkr_heads = dkh[:, :, MLA_NOPE:].transpose(1, 0, 2).reshape(S, H * MLA_ROPE)
    hsum = jnp.tile(jnp.eye(MLA_ROPE, dtype=F32), (H, 1))

    def krope(i, t, f):
        d = jnp.dot(t[0], f[0], preferred_element_type=F32, precision=lax.Precision.HIGHEST)
        return [d * t[1] + _dot((d * t[2]).astype(BF), f[1])], []

    dk_r, = rows_call(tag + "_bkr", krope, S, 512, [dkr_heads, rope['cos32'], rope['sin32']], [hsum, rope['p32t']],
                      [(MLA_ROPE, BF)])
    grads = dict(a_q_norm=dq_g, a_kv_norm=dkv_g, a_w_uq=dw_uq, a_w_ukv=dw_ukv)
    return [dc_q, dc_kv, dk_r], grads, got


def _dil_perm(t, dil):
    H, S, d = t.shape
    return t.reshape(H, S // dil, dil, d).transpose(0, 2, 1, 3).reshape(H, S, d)


def _dil_unperm(t, dil):
    H, S, d = t.shape
    return t.reshape(H, dil, S // dil, d).transpose(0, 2, 1, 3).reshape(H, S, d)


def _band(rpos, cpos, S, seg_shift):
    dist = jnp.abs(rpos - cpos)
    valid = ((dist <= DIL_HALF) & (cpos >= 0) & (cpos < S) & (rpos >= 0) & (rpos < S)
             & (jnp.right_shift(rpos, seg_shift) == jnp.right_shift(cpos, seg_shift)))
    return dist, valid


def _halo_specs(blk, halo, n_halo_blocks, w, nh):
    r = blk // halo
    return [pl.BlockSpec((nh, halo, w), lambda h, i: (h, jnp.maximum(i * r - 1, 0), 0)),
            pl.BlockSpec((nh, blk, w), lambda h, i: (h, i, 0)),
            pl.BlockSpec((nh, halo, w), lambda h, i: (h, jnp.minimum((i + 1) * r, n_halo_blocks - 1), 0))]


def dil_fwd(tag, q, k, v, slopes, dil, tq=256):
    H, S, dh = q.shape
    nh = H
    tq = min(tq, S)
    seg_shift = int(math.log2(S // dil))
    hb = DIL_HALF
    scale = dh ** -0.5

    def body(sl_ref, q_ref, kp, kc, kn, vp, vc, vn, o_ref, lse_ref):
        i = pl.program_id(1)
        shp = (tq, tq + 2 * hb)
        qpos = i * tq + lax.broadcasted_iota(jnp.int32, shp, 0)
        kpos = i * tq - hb + lax.broadcasted_iota(jnp.int32, shp, 1)
        dist, valid = _band(qpos, kpos, S, seg_shift)
        tokens = (dist * dil).astype(F32)
        for h in range(nh):
            kcat = jnp.concatenate([kp[h], kc[h], kn[h]], axis=0)
            vcat = jnp.concatenate([vp[h], vc[h], vn[h]], axis=0)
            s = _dot(q_ref[h], kcat, 1, 1) * scale - sl_ref[h][:, :1] * tokens
            s = jnp.where(valid, s, NEG)
            m = jnp.max(s, axis=-1, keepdims=True)
            lse = m + jnp.log(jnp.sum(jnp.exp(s - m), axis=-1, keepdims=True))
            o_ref[h] = _dot(jnp.exp(s - lse).astype(BF), vcat)
            lse_ref[h] = lse

    return pl.pallas_call(
        body, grid=(H // nh, S // tq),
        in_specs=([pl.BlockSpec((nh, 1, 128), lambda h, i: (h, 0, 0)), pl.BlockSpec((nh, tq, dh), lambda h, i: (h, i, 0))]
                  + _halo_specs(tq, hb, S // hb, dh, nh) * 2),
        out_specs=[pl.BlockSpec((nh, tq, dh), lambda h, i: (h, i, 0)), pl.BlockSpec((nh, tq, 1), lambda h, i: (h, i, 0))],
        out_shape=[SDS((H, S, dh), F32), SDS((H, S, 1), F32)],
        name=tag + "_dil", compiler_params=_params(("arbitrary", "arbitrary")))(slopes, q, k, k, k, v, v, v)


def dil_bwd_q(tag, q, k, v, do, lse, deff, slopes, dil, tq=256):
    H, S, dh = q.shape
    nh = H
    tq = min(tq, S)
    seg_shift = int(math.log2(S // dil))
    hb = DIL_HALF
    scale = dh ** -0.5

    def body(sl_ref, q_ref, do_ref, lse_ref, de_ref, kp, kc, kn, vp, vc, vn, dq_ref):
        i = pl.program_id(1)
        shp = (tq, tq + 2 * hb)
        qpos = i * tq + lax.broadcasted_iota(jnp.int32, shp, 0)
        kpos = i * tq - hb + lax.broadcasted_iota(jnp.int32, shp, 1)
        dist, valid = _band(qpos, kpos, S, seg_shift)
        tokens = (dist * dil).astype(F32)
        for h in range(nh):
            kcat = jnp.concatenate([kp[h], kc[h], kn[h]], axis=0)
            vcat = jnp.concatenate([vp[h], vc[h], vn[h]], axis=0)
            s = _dot(q_ref[h], kcat, 1, 1) * scale - sl_ref[h][:, :1] * tokens
            p = jnp.where(valid, jnp.exp(jnp.where(valid, s, NEG) - lse_ref[h]), 0.0)
            dp = _dot(do_ref[h], vcat, 1, 1)
            ds = (p * (dp - de_ref[h]) * scale).astype(BF)
            dq_ref[h] = _dot(ds, kcat).astype(dq_ref.dtype)

    row = lambda w: pl.BlockSpec((nh, tq, w), lambda h, i: (h, i, 0))
    return pl.pallas_call(
        body, grid=(H // nh, S // tq),
        in_specs=([pl.BlockSpec((nh, 1, 128), lambda h, i: (h, 0, 0)), row(dh), row(dh), row(1), row(1)]
                  + _halo_specs(tq, hb, S // hb, dh, nh) * 2),
        out_specs=row(dh), out_shape=SDS((H, S, dh), BF),
        name=tag + "_dilbq",
        compiler_params=_params(("arbitrary", "arbitrary")))(slopes, q, do, lse, deff, k, k, k, v, v, v)


def dil_bwd_kv(tag, q, k, v, do, lse_row, deff_row, slopes, dil, tk=256):
    H, S, dh = q.shape
    nh = H
    tk = min(tk, S)
    seg_shift = int(math.log2(S // dil))
    hq = 128
    scale = dh ** -0.5
    r = tk // hq
    nhb = S // hq
    stat_specs = [pl.BlockSpec((nh, 1, hq), lambda h, j: (h, 0, jnp.maximum(j * r - 1, 0))),
                  pl.BlockSpec((nh, 1, tk), lambda h, j: (h, 0, j)),
                  pl.BlockSpec((nh, 1, hq), lambda h, j: (h, 0, jnp.minimum((j + 1) * r, nhb - 1)))]

    def body(sl_ref, k_ref, v_ref, qp, qc, qn, dop, doc, don, lp, lc, ln, ep, ec, en, dk_ref, dv_ref):
        j = pl.program_id(1)
        shp = (tk, tk + 2 * hq)
        kpos = j * tk + lax.broadcasted_iota(jnp.int32, shp, 0)
        qpos = j * tk - hq + lax.broadcasted_iota(jnp.int32, shp, 1)
        dist, valid = _band(kpos, qpos, S, seg_shift)
        tokens = (dist * dil).astype(F32)
        for h in range(nh):
            qcat = jnp.concatenate([qp[h], qc[h], qn[h]], axis=0)
            docat = jnp.concatenate([dop[h], doc[h], don[h]], axis=0)
            lse = jnp.concatenate([lp[h], lc[h], ln[h]], axis=1)
            deff = jnp.concatenate([ep[h], ec[h], en[h]], axis=1)
            st = _dot(k_ref[h], qcat, 1, 1) * scale - sl_ref[h][:, :1] * tokens
            pt = jnp.where(valid, jnp.exp(jnp.where(valid, st, NEG) - jnp.where(valid, lse, 0.0)), 0.0)
            dpt = _dot(v_ref[h], docat, 1, 1)
            dst = (pt * (dpt - deff) * scale).astype(BF)
            dv_ref[h] = _dot(pt.astype(BF), docat).astype(dv_ref.dtype)
            dk_ref[h] = _dot(dst, qcat).astype(dk_ref.dtype)

    cur = lambda w: pl.BlockSpec((nh, tk, w), lambda h, j: (h, j, 0))
    return pl.pallas_call(
        body, grid=(H // nh, S // tk),
        in_specs=([pl.BlockSpec((nh, 1, 128), lambda h, j: (h, 0, 0)), cur(dh), cur(dh)]
                  + _halo_specs(tk, hq, nhb, dh, nh) * 2 + stat_specs * 2),
        out_specs=[cur(dh), cur(dh)], out_shape=[SDS((H, S, dh), BF), SDS((H, S, dh), BF)],
        name=tag + "_dilbkv", compiler_params=_params(("arbitrary", "arbitrary")))(
            slopes, k, v, q, q, q, do, do, do, lse_row, lse_row, lse_row, deff_row, deff_row, deff_row)


def _alibi_slopes():
    n = len(DIL_GROUPS) * DIL_HEADS
    s = 2.0 ** (-ALIBI_MAX * (jnp.arange(n, dtype=F32) + 1.0) / n)
    return jnp.broadcast_to(s.reshape(len(DIL_GROUPS), DIL_HEADS, 1, 1), (len(DIL_GROUPS), DIL_HEADS, 1, 128))


def _merge_weights(lses):
    m = jnp.maximum(jnp.maximum(lses[0], lses[1]), lses[2])
    es = [jnp.exp(l - m) for l in lses]
    tot = es[0] + es[1] + es[2]
    return [e / tot for e in es]


def dilated_fwd(tag, z):
    S = z.shape[0]
    H = DIL_HEADS
    ng = len(DIL_GROUPS)
    zd = z[:, :DIL_QKV].reshape(S, ng, 3, H, HEAD_DIM)
    q_mem = z[:, DIL_QKV:]
    slopes = _alibi_slopes()
    qkv, outs, lses = [], [], []
    for g, (_, dil) in enumerate(DIL_GROUPS):
        t = [_dil_perm(zd[:, g, c].transpose(1, 0, 2), dil) for c in range(3)]
        o, lse = dil_fwd(f"{tag}_g{g}", t[0], t[1], t[2], slopes[g], dil)
        qkv.append(t)
        outs.append(_dil_unperm(o, dil))
        lses.append(_dil_unperm(lse, dil))

    def merge(t):
        w = _merge_weights(t[3:6])
        return [w[0] * t[0] + w[1] * t[1] + w[2] * t[2]]

    oh, = hrows_call(tag + "_merge", merge, H, S, 1024, outs + lses, [(HEAD_DIM, BF)])
    return _from_heads(oh), q_mem, dict(qkv=qkv, outs=outs, lses=lses)


def dilated_bwd(tag, do, sv):
    S = do.shape[0]
    H = DIL_HEADS
    slopes = _alibi_slopes()
    doh = _to_heads(do, H)

    def merge_b(t):
        d = t[0].astype(F32)
        w = _merge_weights(t[4:7])
        dws = [jnp.sum(d * t[1 + g], axis=-1, keepdims=True) for g in range(3)]
        c = w[0] * dws[0] + w[1] * dws[1] + w[2] * dws[2]
        return [w[g] * d for g in range(3)] + [w[g] * c for g in range(3)]

    res = hrows_call(tag + "_mergeb", merge_b, H, S, 1024, [doh] + sv['outs'] + sv['lses'],
                     [(HEAD_DIM, BF)] * 3 + [(1, F32)] * 3)
    parts = []
    for g, (_, dil) in enumerate(DIL_GROUPS):
        q, k, v = sv['qkv'][g]
        do_g = _dil_perm(res[g], dil)
        deff = _dil_perm(res[3 + g], dil)
        lse = _dil_perm(sv['lses'][g], dil)
        dq = dil_bwd_q(f"{tag}_g{g}", q, k, v, do_g, lse, deff, slopes[g], dil)
        dk, dv = dil_bwd_kv(f"{tag}_g{g}", q, k, v, do_g, lse.reshape(H, 1, S), deff.reshape(H, 1, S), slopes[g], dil)
        parts.append(jnp.stack([_dil_unperm(t, dil).transpose(1, 0, 2) for t in (dq, dk, dv)], axis=1))
    dzd = jnp.stack(parts, axis=1)
    return [dzd.reshape(S, DIL_QKV).astype(BF)]


def _halo_specs2(ts, w, S):
    r = ts // HALO
    return [pl.BlockSpec((HALO, w), lambda i: (jnp.maximum(i * r - 1, 0), 0)),
            pl.BlockSpec((ts, w), lambda i: (i, 0)),
            pl.BlockSpec((HALO, w), lambda i: (jnp.minimum((i + 1) * r, S // HALO - 1), 0))]


def _cat_rows(i, ts, S, refs):
    v = jnp.concatenate([r[...].astype(F32) for r in refs], axis=0)
    pos = i * ts - HALO + lax.broadcasted_iota(jnp.int32, (ts + 2 * HALO, 1), 0)
    return v, (pos >= 0) & (pos < S)


def _conv_taps(buf_ref, w, ts, width, flip):
    acc = None
    for k in range(width):
        off = HALO + ((width // 2 - k) if flip else (k - width // 2))
        term = buf_ref[pl.ds(off, ts), :] * w[k:k + 1, :]
        acc = term if acc is None else acc + term
    return acc


def conformer_fwd(tag, z, conv_w, conv_b, ln_g, ln_b, ts=256):
    S = z.shape[0]
    ts = min(ts, S)
    C = CONV_CH
    a, gate, q_mem = z[:, :C], z[:, C:2 * C], z[:, 2 * C:]

    def body(ap, ac, an, gp, gc, gn, w_ref, b_ref, lg_ref, lb_ref, u_ref, u1_ref, buf):
        i = pl.program_id(0)
        av, ok = _cat_rows(i, ts, S, (ap, ac, an))
        gv, _ = _cat_rows(i, ts, S, (gp, gc, gn))
        buf[...] = jnp.where(ok, av * jax.nn.sigmoid(gv), 0.0)
        u1 = _conv_taps(buf, w_ref[...], ts, CONV_WIDTH, False) + b_ref[...]
        mu = jnp.mean(u1, axis=-1, keepdims=True)
        xc = u1 - mu
        y = xc * lax.rsqrt(jnp.mean(xc * xc, axis=-1, keepdims=True) + EPS) * lg_ref[...] + lb_ref[...]
        u_ref[...] = _silu(y).astype(u_ref.dtype)
        u1_ref[...] = u1

    full = lambda t: pl.BlockSpec(t.shape, lambda i: (0, 0))
    u, u1 = pl.pallas_call(
        body, grid=(S // ts,),
        in_specs=_halo_specs2(ts, C, S) + _halo_specs2(ts, C, S) + [full(conv_w), full(conv_b), full(ln_g), full(ln_b)],
        out_specs=[pl.BlockSpec((ts, C), lambda i: (i, 0))] * 2, out_shape=[SDS((S, C), BF), SDS((S, C), F32)],
        scratch_shapes=[pltpu.VMEM((ts + 2 * HALO, C), F32)], name=tag + "_conf",
        compiler_params=_params(("arbitrary",)))(a, a, a, gate, gate, gate, conv_w, conv_b, ln_g, ln_b)
    return u, q_mem, dict(a=a, gate=gate, u1=u1)


def conformer_bwd(tag, du, sv, conv_w, conv_b, ln_g, ln_b, ts=256):
    S = du.shape[0]
    ts = min(ts, S)
    C = CONV_CH

    def ln_b_fn(i, t, f):
        u1, d = t[1], t[0].astype(F32)
        mu = jnp.mean(u1, axis=-1, keepdims=True)
        xc = u1 - mu
        r = lax.rsqrt(jnp.mean(xc * xc, axis=-1, keepdims=True) + EPS)
        xh = xc * r
        dy = d * _dsilu(xh * f[0] + f[1])
        dxh = dy * f[0]
        du1 = r * (dxh - jnp.mean(dxh, axis=-1, keepdims=True) - xh * jnp.mean(dxh * xh, axis=-1, keepdims=True))
        return [du1], [jnp.sum(dy * xh, axis=0, keepdims=True), jnp.sum(dy, axis=0, keepdims=True),
                       jnp.sum(du1, axis=0, keepdims=True)]

    du1, dln_g, dln_b, dconv_b = rows_call(tag + "_confb1", ln_b_fn, S, 512, [du, sv['u1']], [ln_g, ln_b],
                                           [(C, F32)], [(1, C), (1, C), (1, C)])
    wpad = CONV_WIDTH + 1

    def body(dp, dc, dn, ap, ac, an, gp, gc, gn, w_ref, da_ref, dg_ref, dw_ref, ubuf, dbuf):
        i = pl.program_id(0)
        av, ok = _cat_rows(i, ts, S, (ap, ac, an))
        gv, _ = _cat_rows(i, ts, S, (gp, gc, gn))
        dv, _ = _cat_rows(i, ts, S, (dp, dc, dn))
        sg = jax.nn.sigmoid(gv)
        ubuf[...] = jnp.where(ok, av * sg, 0.0)
        dbuf[...] = jnp.where(ok, dv, 0.0)
        du0 = _conv_taps(dbuf, w_ref[...], ts, CONV_WIDTH, True)
        a_c, s_c = av[HALO:HALO + ts], sg[HALO:HALO + ts]
        da_ref[...] = (du0 * s_c).astype(da_ref.dtype)
        dg_ref[...] = (du0 * a_c * s_c * (1.0 - s_c)).astype(dg_ref.dtype)
        d_c = dbuf[pl.ds(HALO, ts), :]

        @pl.when(i == 0)
        def _():
            dw_ref[...] = jnp.zeros(dw_ref.shape, F32)

        for k in range(CONV_WIDTH):
            off = HALO + k - CONV_WIDTH // 2
            dw_ref[k:k + 1, :] += jnp.sum(d_c * ubuf[pl.ds(off, ts), :], axis=0, keepdims=True)

    full = lambda t: pl.BlockSpec(t.shape, lambda i: (0, 0))
    da, dgate, dw = pl.pallas_call(
        body, grid=(S // ts,),
        in_specs=_halo_specs2(ts, C, S) * 3 + [full(conv_w)],
        out_specs=[pl.BlockSpec((ts, C), lambda i: (i, 0))] * 2 + [pl.BlockSpec((wpad, C), lambda i: (0, 0))],
        out_shape=[SDS((S, C), BF), SDS((S, C), BF), SDS((wpad, C), F32)],
        scratch_shapes=[pltpu.VMEM((ts + 2 * HALO, C), F32)] * 2, name=tag + "_confb2",
        compiler_params=_params(("arbitrary",)))(du1, du1, du1, sv['a'], sv['a'], sv['a'], sv['gate'], sv['gate'],
                                                 sv['gate'], conv_w)
    grads = dict(c_conv_w=dw[:CONV_WIDTH], c_conv_b=dconv_b, c_ln_g=dln_g, c_ln_b=dln_b)
    return [da, dgate], grads


def shortconv_fwd(tag, z, conv_w, ts=256):
    S = z.shape[0]
    ts = min(ts, S)
    C = SC_CH
    bg, cg, hx, q_mem = z[:, :C], z[:, C:2 * C], z[:, 2 * C:3 * C], z[:, 3 * C:]

    def body(b_ref, cp, cc, cn, hp, hc, hn, w_ref, o_ref, buf):
        i = pl.program_id(0)
        cv, ok = _cat_rows(i, ts, S, (cp, cc, cn))
        hv, _ = _cat_rows(i, ts, S, (hp, hc, hn))
        buf[...] = jnp.where(ok, cv * hv, 0.0)
        o_ref[...] = (b_ref[...].astype(F32) * _conv_taps(buf, w_ref[...], ts, SC_WIDTH, False)).astype(o_ref.dtype)

    o = pl.pallas_call(
        body, grid=(S // ts,),
        in_specs=[pl.BlockSpec((ts, C), lambda i: (i, 0))] + _halo_specs2(ts, C, S) * 2
        + [pl.BlockSpec(conv_w.shape, lambda i: (0, 0))],
        out_specs=pl.BlockSpec((ts, C), lambda i: (i, 0)), out_shape=SDS((S, C), BF),
        scratch_shapes=[pltpu.VMEM((ts + 2 * HALO, C), F32)], name=tag + "_sconv",
        compiler_params=_params(("arbitrary",)))(bg, cg, cg, cg, hx, hx, hx, conv_w)
    return o, q_mem, dict(bg=bg, cg=cg, hx=hx)


def shortconv_bwd(tag, do, sv, conv_w, ts=256):
    S = do.shape[0]
    ts = min(ts, S)
    C = SC_CH
    wpad = 8

    def body(dp, dc, dn, bp, bc, bn, cp, cc, cn, hp, hc, hn, w_ref, db_ref, dcg_ref, dhx_ref, dw_ref, mbuf, dbuf):
        i = pl.program_id(0)
        dov, ok = _cat_rows(i, ts, S, (dp, dc, dn))
        bv, _ = _cat_rows(i, ts, S, (bp, bc, bn))
        cv, _ = _cat_rows(i, ts, S, (cp, cc, cn))
        hv, _ = _cat_rows(i, ts, S, (hp, hc, hn))
        mbuf[...] = jnp.where(ok, cv * hv, 0.0)
        dbuf[...] = jnp.where(ok, dov * bv, 0.0)
        conv = _conv_taps(mbuf, w_ref[...], ts, SC_WIDTH, False)
        db_ref[...] = (dov[HALO:HALO + ts] * conv).astype(db_ref.dtype)
        dm = _conv_taps(dbuf, w_ref[...], ts, SC_WIDTH, True)
        dcg_ref[...] = (dm * hv[HALO:HALO + ts]).astype(dcg_ref.dtype)
        dhx_ref[...] = (dm * cv[HALO:HALO + ts]).astype(dhx_ref.dtype)
        d_c = dbuf[pl.ds(HALO, ts), :]

        @pl.when(i == 0)
        def _():
            dw_ref[...] = jnp.zeros(dw_ref.shape, F32)

        for k in range(SC_WIDTH):
            off = HALO + k - SC_WIDTH // 2
            dw_ref[k:k + 1, :] += jnp.sum(d_c * mbuf[pl.ds(off, ts), :], axis=0, keepdims=True)

    dbg, dcg, dhx, dw = pl.pallas_call(
        body, grid=(S // ts,),
        in_specs=_halo_specs2(ts, C, S) * 4 + [pl.BlockSpec(conv_w.shape, lambda i: (0, 0))],
        out_specs=[pl.BlockSpec((ts, C), lambda i: (i, 0))] * 3 + [pl.BlockSpec((wpad, C), lambda i: (0, 0))],
        out_shape=[SDS((S, C), BF)] * 3 + [SDS((wpad, C), F32)],
        scratch_shapes=[pltpu.VMEM((ts + 2 * HALO, C), F32)] * 2, name=tag + "_sconvb",
        compiler_params=_params(("arbitrary",)))(do, do, do, sv['bg'], sv['bg'], sv['bg'], sv['cg'], sv['cg'],
                                                 sv['cg'], sv['hx'], sv['hx'], sv['hx'], conv_w)
    return [dbg, dcg, dhx], dict(d_conv_w=dw[:SC_WIDTH])


MIXERS = ('a', 'b', 'c', 'd')


def _mixer_weights(W, m):
    return W[m + '_w_in'][0], W[m + '_w_out'][0]


def layer_fwd(i, x, mem, rope, W, gather=None):
    tag = f"l{i}"
    g = W['norm_g'][i]
    gr = lambda k: g[k:k + 1]
    m = MIXERS[i % len(MIXERS)]
    sv = {}
    x1, sv['ffn0'] = ffn_fwd(tag + "_f0", x, gr(0), gr(1), W['ffn_w_gate'][i, 0], W['ffn_w_up'][i, 0],
                             W['ffn_w_down'][i, 0])
    w_in = W[m + '_w_in'][0]
    S = x.shape[0]
    h2 = rms_rows(tag + "_rms2", x1, gr(2))
    n_in = w_in.shape[1]
    z, = mm_call(tag + "_in", [[(h2, w_in, False, False)]], S, n_in, D_MODEL, tm=1024,
                 tn=(256 if n_in % 256 == 0 else n_in), tk=1024, **_plain(BF))
    if m == 'a':
        o, q_mem, sv['mix'], got = mla_fwd(tag, z, rope, W['a_q_norm'], W['a_kv_norm'], W['a_w_uq'][0],
                                           W['a_w_ukv'][0], cargo=(gather[0], False) if gather else None)
        if gather:
            W = gather[1](got)
    elif m == 'b':
        o, q_mem, sv['mix'] = dilated_fwd(tag, z)
    elif m == 'c':
        o, q_mem, sv['mix'] = conformer_fwd(tag, z, W['c_conv_w'][0], W['c_conv_b'], W['c_ln_g'], W['c_ln_b'])
    else:
        o, q_mem, sv['mix'] = shortconv_fwd(tag, z, W['d_conv_w'][0])
    w_out = W[m + '_w_out'][0]
    kv = mem_kv_fwd(tag, mem, gr(6), W['mem_w_kv'][i])
    kbd, vbd = _block_diag_kv(kv)
    mo = mem_attn_fwd(tag, q_mem, kbd, vbd)
    cat = jnp.concatenate([o, mo], axis=1)
    n_out = cat.shape[1]

    def ep_out(a, t, c, f):
        return [a[0], t[0] + _rms(a[0], c[0])], []

    y2, x2 = mm_call(tag + "_out", [[(cat, w_out, False, False)]], S, D_MODEL, n_out, tm=1024, tn=1024, tk=n_out,
                     epilogue=ep_out, outs=[BF, F32], tiles=[x1], cols=[gr(3)])
    sv.update(x1=x1, h2=h2, q_mem=q_mem, kbd=kbd, vbd=vbd, cat=cat, y2=y2)
    x3, sv['ffn1'] = ffn_fwd(tag + "_f1", x2, gr(4), gr(5), W['ffn_w_gate'][i, 1], W['ffn_w_up'][i, 1],
                             W['ffn_w_down'][i, 1])
    return x3, sv, W


def layer_bwd(i, dx3, sv, mem, rope, W, scatter=None):
    tag = f"l{i}"
    g = W['norm_g'][i]
    gr = lambda k: g[k:k + 1]
    m = MIXERS[i % len(MIXERS)]
    S = dx3.shape[0]
    dx2, f1 = ffn_bwd(tag + "_f1", dx3, sv['ffn1'], gr(4), gr(5), W['ffn_w_gate'][i, 1], W['ffn_w_up'][i, 1],
                      W['ffn_w_down'][i, 1])
    w_in, w_out = W[m + '_w_in'][0], W[m + '_w_out'][0]
    n_in, n_out = w_in.shape[1], w_out.shape[0]
    got = None

    def post(ii, t, f):
        dx, dg = _rms_bwd(t[1].astype(F32), f[0], t[0])
        return [dx], [dg]

    dy2, dg3 = rows_call(tag + "_bpost", post, S, 512, [dx2, sv['y2']], [gr(3)], [(D_MODEL, BF)], [(1, D_MODEL)])
    dcat, = mm_call(tag + "_bout", [[(dy2, w_out, False, True)]], S, n_out, D_MODEL, tm=1024,
                    tn=(256 if n_out % 256 == 0 else n_out), tk=1024, **_plain(BF))
    dw_out, = mm_call(tag + "_dwout", [[(sv['cat'], dy2, True, False)]], n_out, D_MODEL, S, tm=n_out, tn=1024, tk=1024,
                      **_plain(F32))
    n_o = n_out - MEM_WIDTH
    do, dmo = dcat[:, :n_o], dcat[:, n_o:]
    dq_mem, dkv = mem_attn_bwd(tag, sv['q_mem'], dmo, sv['kbd'], sv['vbd'])
    dw_kv, dg6 = mem_kv_bwd(tag, mem, gr(6), W['mem_w_kv'][i], dkv)
    if m == 'a':
        cargo = (scatter(dict(ffn1=f1, a_w_out=dw_out[None], mem_w_kv=dw_kv)), True) if scatter else None
        dzs, mg, got = mla_bwd(tag, do, sv['mix'], rope, W['a_q_norm'], W['a_kv_norm'], W['a_w_uq'][0],
                               W['a_w_ukv'][0], cargo=cargo)
    elif m == 'b':
        dzs, mg = dilated_bwd(tag, do, sv['mix']), {}
    elif m == 'c':
        dzs, mg = conformer_bwd(tag, do, sv['mix'], W['c_conv_w'][0], W['c_conv_b'], W['c_ln_g'], W['c_ln_b'])
    else:
        dzs, mg = shortconv_bwd(tag, do, sv['mix'], W['d_conv_w'][0])
    dz = jnp.concatenate(dzs + [dq_mem], axis=1)
    dw_in, = mm_call(tag + "_dwin", [[(sv['h2'], dz, True, False)]], D_MODEL, n_in, S, tm=1024,
                     tn=(512 if n_in % 512 == 0 else (256 if n_in % 256 == 0 else n_in)), tk=1024, **_plain(F32))

    def ep_in(a, t, c, f):
        dx, dg = _rms_bwd(t[0], c[0], a[0])
        return [t[1] + dx], [dg]

    dx1, dg2 = mm_call(tag + "_bin", [[(dz, w_in, False, True)]], S, D_MODEL, n_in, tm=512, tn=1024,
                       tk=(n_in // 2 if n_in % 256 == 0 else n_in), epilogue=ep_in, outs=[F32],
                       tiles=[sv['x1'], dx2], cols=[gr(2)], reds=[1])
    dx0, f0 = ffn_bwd(tag + "_f0", dx1, sv['ffn0'], gr(0), gr(1), W['ffn_w_gate'][i, 0], W['ffn_w_up'][i, 0],
                      W['ffn_w_down'][i, 0])
    dg = jnp.concatenate([f0['g_in'], f0['g_out'], dg2, dg3, f1['g_in'], f1['g_out'], dg6], axis=0)
    grads = dict(norm_g=dg, ffn=[f0, f1], mem_w_kv=dw_kv)
    grads[m + '_w_in'] = dw_in[None]
    grads[m + '_w_out'] = dw_out[None]
    for k, v in mg.items():
        grads[k] = v if k in ('a_q_norm', 'a_kv_norm', 'c_conv_b', 'c_ln_g', 'c_ln_b') else v[None]
    return dx0, grads, got


def local_step(x, mem, positions, target, W, gather=None, scatter=None):
    S = x.shape[0]
    rope = _rope_tables(positions)
    saved = []
    for i in range(DEPTH):
        x, sv, W = layer_fwd(i, x, mem, rope, W, gather if i == 0 else None)
        saved.append(sv)

    def loss_fn(i, t, f):
        e = t[0] - t[1]
        part = jnp.sum(0.5 * jnp.mean(e * e, axis=-1, keepdims=True), axis=0, keepdims=True)
        return [e * (1.0 / D_MODEL)], [jnp.broadcast_to(part, (1, 128))]

    dx, loss = rows_call("loss", loss_fn, S, 512, [x, target], [], [(D_MODEL, F32)], [(1, 128)])
    upper, got = [], None
    for i in reversed(range(DEPTH)):
        hook = (lambda early: scatter(upper, early)) if (scatter and i == 0) else None
        dx, gr, got = layer_bwd(i, dx, saved[i], mem, rope, W, hook)
        upper.insert(0, gr)
    return loss[0, 0], dx, upper, got


def stack_grads(per_layer, names=None):
    grads = {}
    want = lambda n: names is None or n in names
    if want('norm_g'):
        grads['norm_g'] = jnp.stack([p['norm_g'] for p in per_layer])
    if want('mem_w_kv'):
        grads['mem_w_kv'] = jnp.stack([p['mem_w_kv'] for p in per_layer])
    for n, k in (('ffn_w_gate', 'wg'), ('ffn_w_up', 'wu'), ('ffn_w_down', 'wd')):
        if want(n):
            grads[n] = jnp.stack([jnp.stack([f[k] for f in p['ffn']]) for p in per_layer])
    for p in per_layer:
        for n, v in p.items():
            if n not in ('norm_g', 'mem_w_kv', 'ffn') and want(n):
                grads[n] = v
    return grads


def _local_shape(full_shape, axis):
    s = list(full_shape)
    if axis is not None:
        s[axis] //= N_DEV
    return tuple(s)


def _pad_to(n, m):
    return -(-n // m) * m


def pack_local(arrs, dtype, row_mult):
    flat = jnp.concatenate([a.reshape(-1).astype(dtype) for a in arrs])
    n = _pad_to(flat.shape[0], PACK_W * row_mult)
    return jnp.pad(flat, (0, n - flat.shape[0])).reshape(n // PACK_W, PACK_W)


def unpack_local(buf, names, shapes):
    flat = buf.reshape(-1)
    out, off = {}, 0
    for n in names:
        sz = math.prod(shapes[n])
        out[n] = flat[off:off + sz].reshape(shapes[n])
        off += sz
    return out


def unpack_gathered(buf, names, shapes):
    flat = buf.reshape(N_DEV, -1)
    out, off = {}, 0
    for n in names:
        ax = SHARD_AXIS[n]
        sz = math.prod(shapes[n])
        t = flat[:, off:off + sz].reshape((N_DEV,) + tuple(shapes[n]))
        t = jnp.moveaxis(t, 0, ax)
        full = list(shapes[n])
        full[ax] *= N_DEV
        out[n] = t.reshape(full)
        off += sz
    return out


def pack_grads(grads, names, row_mult):
    parts = []
    for n in names:
        g = grads[n].astype(F32)
        ax = SHARD_AXIS[n]
        if ax is None:
            parts.append(jnp.broadcast_to(g.reshape(1, -1), (N_DEV, g.size)))
        else:
            shp = list(g.shape)
            t = g.reshape(shp[:ax] + [N_DEV, shp[ax] // N_DEV] + shp[ax + 1:])
            parts.append(jnp.moveaxis(t, ax, 0).reshape(N_DEV, -1))
    flat = jnp.concatenate(parts, axis=1)
    n = _pad_to(flat.shape[1], PACK_W * row_mult)
    return jnp.pad(flat, ((0, 0), (0, n - flat.shape[1]))).reshape(N_DEV, n // PACK_W, PACK_W)


COL_SHARDED = ['a_w_uq', 'a_w_ukv', 'b_w_in', 'b_w_out', 'c_w_in', 'd_w_in']
ROW_SHARDED_1024 = ['a_w_out', 'c_w_out', 'd_w_out']
FLAT = SMALL + ['a_q_norm', 'a_kv_norm']


EDGE_COLS = ['a_w_uq', 'a_w_ukv']
BULK_COLS = ['b_w_in', 'b_w_out', 'c_w_in', 'd_w_in']
N_FFN = DEPTH * 2


def _pad_cols(t):
    return jnp.pad(t, ((0, 0), (0, FF_SLOT - FF_SHARD)))


def wire_weights_edge(a):
    ws = [_pad_cols(a['ffn_w_gate'][0, 0]), _pad_cols(a['ffn_w_up'][0, 0]), a['ffn_w_down'][0, 0], a['a_w_in'][0]]
    return [t.astype(BF) for t in ws + [a[n][0] for n in EDGE_COLS]]


def wire_weights_bulk(a):
    rest = lambda t, r: t.reshape(N_FFN * r, t.shape[-1])[r:]
    ws = [_pad_cols(rest(a['ffn_w_gate'], D_MODEL)), _pad_cols(rest(a['ffn_w_up'], D_MODEL)),
          jnp.concatenate([rest(a['ffn_w_down'], FF_SHARD)] + [a[n][0] for n in ROW_SHARDED_1024], axis=0),
          a['mem_w_kv'].reshape(-1, 2 * MEM_WIDTH)]
    return [t.astype(BF) for t in ws + [a[n][0] for n in BULK_COLS]]


def _unslot_cols(t):
    return t.transpose(1, 0, 2).reshape(t.shape[1], N_DEV * t.shape[2])


def _unslot_down(t, n):
    t = t.reshape(N_DEV, n, FF_SHARD, D_MODEL)
    t = jnp.pad(t, ((0, 0), (0, 0), (0, FF_SLOT - FF_SHARD), (0, 0)))
    return t.transpose(1, 0, 2, 3).reshape(n, D_FFP, D_MODEL)


def unwire_edge(g):
    W = dict(ffn_w_gate=_unslot_cols(g[0]).reshape(1, 1, D_MODEL, D_FFP),
             ffn_w_up=_unslot_cols(g[1]).reshape(1, 1, D_MODEL, D_FFP),
             ffn_w_down=_unslot_down(g[2], 1).reshape(1, 1, D_FFP, D_MODEL), a_w_in=g[3].reshape(1, D_MODEL, -1))
    for k, n in enumerate(EDGE_COLS):
        W[n] = _unslot_cols(g[4 + k])[None]
    return W


def unwire_bulk(g, edge):
    W = dict(edge)
    for k, n in enumerate(('ffn_w_gate', 'ffn_w_up')):
        t = _unslot_cols(g[k]).reshape(N_FFN - 1, D_MODEL, D_FFP)
        W[n] = jnp.concatenate([edge[n][0], t], axis=0).reshape(DEPTH, 2, D_MODEL, D_FFP)
    n_dn = (N_FFN - 1) * FF_SHARD
    dn = _unslot_down(g[2][:, :n_dn], N_FFN - 1)
    W['ffn_w_down'] = jnp.concatenate([edge['ffn_w_down'][0], dn], axis=0).reshape(DEPTH, 2, D_FFP, D_MODEL)
    rows = D_MODEL // N_DEV
    for k, n in enumerate(ROW_SHARDED_1024):
        W[n] = g[2][:, n_dn + k * rows:n_dn + (k + 1) * rows].reshape(1, D_MODEL, D_MODEL)
    W['mem_w_kv'] = g[3].reshape(N_DEV, DEPTH, rows, 2 * MEM_WIDTH).transpose(1, 0, 2, 3).reshape(DEPTH, D_MODEL, -1)
    for k, n in enumerate(BULK_COLS):
        W[n] = _unslot_cols(g[4 + k])[None]
    return W


def _slot_cols(t):
    return t.reshape(t.shape[0], N_DEV, -1).transpose(1, 0, 2)


def _slot_down(t):
    return t.reshape(N_DEV, FF_SLOT, D_MODEL)[:, :FF_SHARD]


def wire_grads_edge(ffn0, g):
    ws = [_slot_cols(ffn0['wg']), _slot_cols(ffn0['wu']), _slot_down(ffn0['wd']),
          g['a_w_in'][0].reshape(N_DEV, D_MODEL // N_DEV, -1)] + [_slot_cols(g[n][0]) for n in EDGE_COLS]
    return [t.astype(BF) for t in ws]


def wire_grads_bulk(ffns, g, mem_w_kv):
    rows = D_MODEL // N_DEV
    kv = mem_w_kv.reshape(DEPTH, N_DEV, rows, 2 * MEM_WIDTH).transpose(1, 0, 2, 3).reshape(N_DEV, DEPTH * rows, -1)
    ws = [jnp.concatenate([_slot_cols(f['wg']) for f in ffns], axis=1),
          jnp.concatenate([_slot_cols(f['wu']) for f in ffns], axis=1),
          jnp.concatenate([_slot_down(f['wd']) for f in ffns]
                          + [g[n][0].reshape(N_DEV, rows, D_MODEL) for n in ROW_SHARDED_1024], axis=1), kv]
    return [t.astype(BF) for t in ws + [_slot_cols(g[n][0]) for n in BULK_COLS]]


def _peers():
    x, y, c = lax.axis_index("x"), lax.axis_index("y"), lax.axis_index("c")
    me = 4 * x + 2 * y + c
    peers = []
    for k in range(1, N_DEV):
        px = 1 - x if k & 4 else x
        py = 1 - y if k & 2 else y
        pc = 1 - c if k & 1 else c
        peers.append(((px, py, pc), 4 * px + 2 * py + pc))
    return me, peers


def _exchange(name, arrays, scatter):
    n, c_in, c_out, c_shape, c_scratch = _cargo_specs((arrays, scatter))

    def body(*refs):
        start, wait = _exchange_ops(refs[:n], refs[n:2 * n], *refs[2 * n:], scatter)
        start()
        wait()

    return pl.pallas_call(body, out_shape=c_shape, in_specs=c_in, out_specs=c_out, scratch_shapes=c_scratch,
                          name=name)(*arrays)


def _exchange_ops(srcs, outs, send_sems, recv_sems, local_sems, scatter):
    me, peers = _peers()

    def local(a):
        return pltpu.make_async_copy(srcs[a].at[me] if scatter else srcs[a], outs[a].at[me], local_sems.at[a])

    def remote(a, k, landing):
        dev, idx = peers[k]
        return pltpu.make_async_remote_copy(src_ref=srcs[a].at[idx] if scatter else srcs[a],
                                            dst_ref=outs[a].at[idx if landing else me],
                                            send_sem=send_sems.at[a, k], recv_sem=recv_sems.at[a, k],
                                            device_id=dev, device_id_type=pl.DeviceIdType.MESH)

    pairs = [(a, k) for a in range(len(srcs)) for k in range(N_DEV - 1)]

    def start():
        for a in range(len(srcs)):
            local(a).start()
        for a, k in pairs:
            remote(a, k, False).start()

    def wait():
        for a, k in pairs:
            remote(a, k, True).wait_recv()
        for a, k in pairs:
            remote(a, k, False).wait_send()
        for a in range(len(srcs)):
            local(a).wait()

    return start, wait


ADAM_ROWS = 128


def sum_adamw(name, parts, row_off, w, m, v, tr):
    rows, wl = w.shape
    wp = parts.shape[2]
    tr = min(tr, rows)
    assert rows % tr == 0 and row_off % tr == 0, (name, rows, row_off, tr)
    off = row_off // tr

    def body(p_ref, w_ref, m_ref, v_ref, g_out, d_out, m_out, v_out):
        g = p_ref[0].astype(F32)
        for q in range(1, N_DEV):
            g = g + p_ref[q].astype(F32)
        g = g[:, :wl]
        m_new = ADAM_B1 * m_ref[...] + (1.0 - ADAM_B1) * g
        v_new = ADAM_B2 * v_ref[...] + (1.0 - ADAM_B2) * jnp.square(g)
        m_hat = m_new / (1.0 - ADAM_B1 ** ADAM_STEP)
        v_hat = v_new / (1.0 - ADAM_B2 ** ADAM_STEP)
        g_out[...] = g
        d_out[...] = -ADAM_LR * (m_hat / (jnp.sqrt(v_hat) + ADAM_EPS) + ADAM_WD * w_ref[...])
        m_out[...] = m_new
        v_out[...] = v_new

    blk = pl.BlockSpec((tr, wl), lambda i: (i, 0))
    return pl.pallas_call(
        body, grid=(rows // tr,),
        in_specs=[pl.BlockSpec((N_DEV, tr, wp), lambda i: (0, i + off, 0)), blk, blk, blk],
        out_specs=[blk] * 4, out_shape=[SDS((rows, wl), F32)] * 4, name=name,
        compiler_params=_params(("arbitrary",)))(parts, w, m, v)


def kernel(x, mem, positions, norm_g, ffn_w_gate, ffn_w_up, ffn_w_down, mem_w_kv, a_w_in, a_q_norm, a_kv_norm, a_w_uq, a_w_ukv, a_w_out, b_w_in, b_w_out, c_w_in, c_conv_w, c_conv_b, c_ln_g, c_ln_b, c_w_out, d_w_in, d_conv_w, d_w_out, loss_target, m_norm_g, m_ffn_w_gate, m_ffn_w_up, m_ffn_w_down, m_mem_w_kv, m_a_w_in, m_a_q_norm, m_a_kv_norm, m_a_w_uq, m_a_w_ukv, m_a_w_out, m_b_w_in, m_b_w_out, m_c_w_in, m_c_conv_w, m_c_conv_b, m_c_ln_g, m_c_ln_b, m_c_w_out, m_d_w_in, m_d_conv_w, m_d_w_out, v_norm_g, v_ffn_w_gate, v_ffn_w_up, v_ffn_w_down, v_mem_w_kv, v_a_w_in, v_a_q_norm, v_a_kv_norm, v_a_w_uq, v_a_w_ukv, v_a_w_out, v_b_w_in, v_b_w_out, v_c_w_in, v_c_conv_w, v_c_conv_b, v_c_ln_g, v_c_ln_b, v_c_w_out, v_d_w_in, v_d_conv_w, v_d_w_out):
    a = dict(zip(IN_NAMES, (x, mem, positions, norm_g, ffn_w_gate, ffn_w_up, ffn_w_down, mem_w_kv, a_w_in, a_q_norm, a_kv_norm, a_w_uq, a_w_ukv, a_w_out, b_w_in, b_w_out, c_w_in, c_conv_w, c_conv_b, c_ln_g, c_ln_b, c_w_out, d_w_in, d_conv_w, d_w_out, loss_target, m_norm_g, m_ffn_w_gate, m_ffn_w_up, m_ffn_w_down, m_mem_w_kv, m_a_w_in, m_a_q_norm, m_a_kv_norm, m_a_w_uq, m_a_w_ukv, m_a_w_out, m_b_w_in, m_b_w_out, m_c_w_in, m_c_conv_w, m_c_conv_b, m_c_ln_g, m_c_ln_b, m_c_w_out, m_d_w_in, m_d_conv_w, m_d_w_out, v_norm_g, v_ffn_w_gate, v_ffn_w_up, v_ffn_w_down, v_mem_w_kv, v_a_w_in, v_a_q_norm, v_a_kv_norm, v_a_w_uq, v_a_w_ukv, v_a_w_out, v_b_w_in, v_b_w_out, v_c_w_in, v_c_conv_w, v_c_conv_b, v_c_ln_g, v_c_ln_b, v_c_w_out, v_d_w_in, v_d_conv_w, v_d_w_out)))
    shapes = {n: tuple(a[n].shape) for n in WEIGHTS}
    got_edge = _exchange("gather_edge", wire_weights_edge(a) + [pack_local([a[n] for n in SMALL], F32, 8)], False)
    common = unpack_gathered(got_edge[-1], SMALL, shapes)
    common['a_q_norm'], common['a_kv_norm'] = a['a_q_norm'], a['a_kv_norm']
    edge = unwire_edge(got_edge[:-1])

    def bulk_grads(upper, early):
        ffns = [early['ffn1']] + [f for p in upper for f in p['ffn']]
        g = dict(a_w_out=early['a_w_out'])
        for p in upper:
            g.update({n: v for n, v in p.items() if n.endswith(('_w_in', '_w_out'))})
        return wire_grads_bulk(ffns, g, jnp.stack([early['mem_w_kv']] + [p['mem_w_kv'] for p in upper]))

    loss, grad_x, per_layer, got_bulk = local_step(
        a['x'][0], a['mem'][0], a['positions'][0], a['loss_target'][0], {**edge, **common},
        gather=(wire_weights_bulk(a), lambda got: {**unwire_bulk(got, edge), **common}), scatter=bulk_grads)
    flat_grads = stack_grads(per_layer, FLAT)
    got_late = _exchange("scatter_edge", wire_grads_edge(per_layer[0]['ffn'][0], per_layer[0])
                         + [pack_grads(flat_grads, FLAT, ADAM_ROWS)], True)
    res = {}

    def adam(n, p, row_off, rows, tr):
        view = [a[pre + n].reshape(rows, -1) for pre in ('', 'm_', 'v_')]
        res[n] = [t.reshape(shapes[n]) for t in sum_adamw("adamw_" + n, p, row_off, *view, tr)]

    both = lambda k: jnp.concatenate([got_late[k], got_bulk[k]], axis=1)
    n_ff = N_FFN * D_MODEL
    adam('ffn_w_gate', both(0), 0, n_ff, 256)
    adam('ffn_w_up', both(1), 0, n_ff, 256)
    n_dn = N_FFN * FF_SHARD
    rows_1024 = both(2)
    adam('ffn_w_down', rows_1024, 0, n_dn, 128)
    rows = D_MODEL // N_DEV
    for k, n in enumerate(ROW_SHARDED_1024):
        adam(n, rows_1024, n_dn + k * rows, rows, 128)
    adam('mem_w_kv', got_bulk[3], 0, DEPTH * rows, 128)
    adam('a_w_in', got_late[3], 0, rows, 128)
    for k, n in enumerate(EDGE_COLS):
        adam(n, got_late[4 + k], 0, shapes[n][1], 128)
    for k, n in enumerate(BULK_COLS):
        adam(n, got_bulk[4 + k], 0, shapes[n][1], 128)
    packed = [pack_local([a[pre + n] for n in FLAT], F32, ADAM_ROWS) for pre in ('', 'm_', 'v_')]
    flat = [unpack_local(r, FLAT, shapes) for r in sum_adamw("adamw_flat", got_late[-1], 0, *packed, ADAM_ROWS)]
    for n in FLAT:
        res[n] = [f[n] for f in flat]
    loss = lax.psum(loss, MESH_AXES)
    return (loss, grad_x[None], *[res[n][k] for k in range(4) for n in WEIGHTS])
```

```python
import functools
import math

import jax
import jax.numpy as jnp
from jax import lax
from jax.experimental import pallas as pl
from jax.experimental.pallas import tpu as pltpu

BF, F32 = jnp.bfloat16, jnp.float32
SDS = jax.ShapeDtypeStruct

D_MODEL, DEPTH, D_FF = 1024, 4, 2816
HEAD_DIM, MEM_HEADS, MEM_WIDTH, N_MEM = 64, 4, 256, 256
MLA_HEADS, MLA_Q_LORA, MLA_KV_LORA, MLA_NOPE, MLA_ROPE, MLA_V = 12, 384, 256, 64, 32, 64
MLA_QK = MLA_NOPE + MLA_ROPE
ROPE_THETA = 10000.0
DIL_GROUPS = ((128, 1), (512, 4), (2048, 16))
DIL_HEADS, DIL_HALF = 8, 64
DIL_QKV = len(DIL_GROUPS) * 3 * DIL_HEADS * HEAD_DIM
ALIBI_MAX = 8.0
CONV_CH, CONV_WIDTH, SC_CH, SC_WIDTH = 768, 31, 768, 3
EPS, NEG = 1e-6, -1e30
ADAM_LR, ADAM_B1, ADAM_B2, ADAM_EPS, ADAM_WD, ADAM_STEP = 0.001, 0.9, 0.999, 1e-08, 0.01, 10

N_DEV = 8
MESH_AXES = ("x", "y", "c")
FF_SHARD = D_FF // N_DEV
FF_SLOT = 384
D_FFP = N_DEV * FF_SLOT
LOG2E = 1.4426950408889634
V7X_VMEM_LIMIT = 48 * 1024 * 1024
HALO = 16
PACK_W = 1024

WEIGHTS = ['norm_g', 'ffn_w_gate', 'ffn_w_up', 'ffn_w_down', 'mem_w_kv', 'a_w_in', 'a_q_norm', 'a_kv_norm',
           'a_w_uq', 'a_w_ukv', 'a_w_out', 'b_w_in', 'b_w_out', 'c_w_in', 'c_conv_w', 'c_conv_b', 'c_ln_g',
           'c_ln_b', 'c_w_out', 'd_w_in', 'd_conv_w', 'd_w_out']
SHARD_AXIS = dict(norm_g=2, ffn_w_gate=3, ffn_w_up=3, ffn_w_down=2, mem_w_kv=1, a_w_in=1, a_q_norm=None,
                  a_kv_norm=None, a_w_uq=2, a_w_ukv=2, a_w_out=1, b_w_in=2, b_w_out=2, c_w_in=2, c_conv_w=2,
                  c_conv_b=1, c_ln_g=1, c_ln_b=1, c_w_out=1, d_w_in=2, d_conv_w=2, d_w_out=1)
SMALL = ['norm_g', 'c_conv_w', 'c_conv_b', 'c_ln_g', 'c_ln_b', 'd_conv_w']
BIG = [n for n in WEIGHTS if n not in SMALL and SHARD_AXIS[n] is not None]
IN_NAMES = (['x', 'mem', 'positions'] + WEIGHTS + ['loss_target'] + ['m_' + n for n in WEIGHTS]
            + ['v_' + n for n in WEIGHTS])


def _params(sem):
    return pltpu.CompilerParams(dimension_semantics=sem, vmem_limit_bytes=V7X_VMEM_LIMIT)


def _dot(a, b, ca=1, cb=0):
    return lax.dot_general(a, b, (((ca,), (cb,)), ((), ())), preferred_element_type=F32)


def mm_call(name, accs, M, N, K, *, tm, tn, tk, epilogue, outs, tiles=(), cols=(), fulls=(), reds=()):
    tm, tn, tk = min(tm, M), min(tn, N), min(tk, K)
    assert M % tm == 0 and N % tn == 0 and K % tk == 0, (name, M, N, K, tm, tn, tk)
    gj, gi, gk = N // tn, M // tm, K // tk
    flat = [p for acc in accs for p in acc]
    in_specs, args = [], []
    for (a, b, ta, tb) in flat:
        in_specs.append(pl.BlockSpec((tk, tm), lambda j, i, k: (k, i)) if ta else pl.BlockSpec((tm, tk), lambda j, i, k: (i, k)))
        in_specs.append(pl.BlockSpec((tn, tk), lambda j, i, k: (j, k)) if tb else pl.BlockSpec((tk, tn), lambda j, i, k: (k, j)))
        args += [a, b]
    for t in tiles:
        in_specs.append(pl.BlockSpec((tm, tn), lambda j, i, k: (i, j)))
        args.append(t)
    for c in cols:
        in_specs.append(pl.BlockSpec((c.shape[0], tn), lambda j, i, k: (0, j)))
        args.append(c)
    for f in fulls:
        in_specs.append(pl.BlockSpec(f.shape, lambda j, i, k: (0,) * f.ndim))
        args.append(f)
    out_shape = [SDS((M, N), dt) for dt in outs] + [SDS((r, N), F32) for r in reds]
    out_specs = ([pl.BlockSpec((tm, tn), lambda j, i, k: (i, j)) for _ in outs]
                 + [pl.BlockSpec((r, tn), lambda j, i, k: (0, j)) for r in reds])
    n_in = len(args)
    n_acc = len(accs)

    def body(*refs):
        in_refs = refs[:n_in]
        out_refs = refs[n_in:n_in + len(outs)]
        red_refs = refs[n_in + len(outs):n_in + len(outs) + len(reds)]
        acc_refs = refs[n_in + len(outs) + len(reds):]
        i, k = pl.program_id(1), pl.program_id(2)

        def product(idx):
            _, _, ta, tb = flat[idx]
            av = in_refs[2 * idx][...].astype(BF)
            bv = in_refs[2 * idx + 1][...].astype(BF)
            return _dot(av, bv, 0 if ta else 1, 1 if tb else 0)

        def finish(vals):
            p = 2 * len(flat)
            tv = [r[...] for r in in_refs[p:p + len(tiles)]]
            cv = [r[...] for r in in_refs[p + len(tiles):p + len(tiles) + len(cols)]]
            fv = [r[...] for r in in_refs[p + len(tiles) + len(cols):]]
            ov, rv = epilogue(vals, tv, cv, fv)
            for r, v in zip(out_refs, ov):
                r[...] = v.astype(r.dtype)
            for r, v in zip(red_refs, rv):
                @pl.when(i == 0)
                def _(r=r, v=v):
                    r[...] = v

                @pl.when(i > 0)
                def _(r=r, v=v):
                    r[...] += v

        starts = [sum(len(acc) for acc in accs[:n]) for n in range(n_acc)]
        if gk == 1:
            vals = []
            for s, acc in zip(starts, accs):
                tot = product(s)
                for p in range(1, len(acc)):
                    tot = tot + product(s + p)
                vals.append(tot)
            finish(vals)
        else:
            @pl.when(k == 0)
            def _():
                for r in acc_refs:
                    r[...] = jnp.zeros(r.shape, F32)

            for r, s, acc in zip(acc_refs, starts, accs):
                for p in range(len(acc)):
                    r[...] += product(s + p)

            @pl.when(k == gk - 1)
            def _():
                finish([r[...] for r in acc_refs])

    scratch = [] if gk == 1 else [pltpu.VMEM((tm, tn), F32) for _ in range(n_acc)]
    res = pl.pallas_call(body, grid=(gj, gi, gk), in_specs=in_specs, out_specs=out_specs, out_shape=out_shape,
                         scratch_shapes=scratch, name=name,
                         compiler_params=_params(("arbitrary", "arbitrary", "arbitrary")))(*args)
    return res


def _plain(dt):
    return dict(epilogue=lambda a, t, c, f: ([a[0]], []), outs=[dt])


def rows_call(name, fn, S, ts, tiled, fulls, outs, reds=()):
    ts = min(ts, S)
    assert S % ts == 0, (name, S, ts)
    in_specs = ([pl.BlockSpec((ts, t.shape[1]), lambda i: (i, 0)) for t in tiled]
                + [pl.BlockSpec(f.shape, lambda i, n=f.ndim: (0,) * n) for f in fulls])
    out_shape = [SDS((S, w), dt) for (w, dt) in outs] + [SDS(rw, F32) for rw in reds]
    out_specs = ([pl.BlockSpec((ts, w), lambda i: (i, 0)) for (w, _) in outs]
                 + [pl.BlockSpec(rw, lambda i: (0, 0)) for rw in reds])
    nt, nf, no = len(tiled), len(fulls), len(outs)

    def body(*refs):
        i = pl.program_id(0)
        tv = [r[...] for r in refs[:nt]]
        fv = [r[...] for r in refs[nt:nt + nf]]
        ov, rv = fn(i, tv, fv)
        for r, v in zip(refs[nt + nf:nt + nf + no], ov):
            r[...] = v.astype(r.dtype)
        for r, v in zip(refs[nt + nf + no:], rv):
            @pl.when(i == 0)
            def _(r=r, v=v):
                r[...] = v

            @pl.when(i > 0)
            def _(r=r, v=v):
                r[...] += v

    return pl.pallas_call(body, grid=(S // ts,), in_specs=in_specs, out_specs=out_specs, out_shape=out_shape,
                          name=name, compiler_params=_params(("arbitrary",)))(*tiled, *fulls)


def hrows_call(name, fn, H, S, ts, tiled, outs):
    ts = min(ts, S)
    in_specs = [pl.BlockSpec((1, ts, t.shape[2]), lambda h, i: (h, i, 0)) for t in tiled]
    out_shape = [SDS((H, S, w), dt) for (w, dt) in outs]
    out_specs = [pl.BlockSpec((1, ts, w), lambda h, i: (h, i, 0)) for (w, _) in outs]
    nt = len(tiled)

    def body(*refs):
        ov = fn([r[0] for r in refs[:nt]])
        for r, v in zip(refs[nt:], ov):
            r[0] = v.astype(r.dtype)

    return pl.pallas_call(body, grid=(H, S // ts), in_specs=in_specs, out_specs=out_specs, out_shape=out_shape,
                          name=name, compiler_params=_params(("arbitrary", "arbitrary")))(*tiled)


def _rms(x, g):
    r = lax.rsqrt(jnp.mean(x * x, axis=-1, keepdims=True) + EPS)
    return x * r * g


def _rms_bwd(x, g, dy):
    r = lax.rsqrt(jnp.mean(x * x, axis=-1, keepdims=True) + EPS)
    xh = x * r
    dg = jnp.sum(dy * xh, axis=0, keepdims=True)
    dxh = dy * g
    dx = r * (dxh - xh * jnp.mean(dxh * xh, axis=-1, keepdims=True))
    return dx, dg


def _silu(x):
    return x * jax.nn.sigmoid(x)


def _dsilu(x):
    s = jax.nn.sigmoid(x)
    return s * (1.0 + x * (1.0 - s))


def rms_rows(name, x, g):
    S, w = x.shape
    return rows_call(name, lambda i, t, f: ([_rms(t[0].astype(F32), f[0])], []), S, 512, [x], [g], [(w, BF)])[0]


def ffn_fwd(tag, x0, g_in, g_out, wg, wu, wd):
    S = x0.shape[0]
    h = rms_rows(tag + "_rms", x0, g_in)

    def ep_up(a, t, c, f):
        return [a[0], a[1], _silu(a[0]) * a[1]], []

    gate, up, act = mm_call(tag + "_up", [[(h, wg, False, False)], [(h, wu, False, False)]], S, D_FFP, D_MODEL,
                            tm=1024, tn=512, tk=1024, epilogue=ep_up, outs=[BF, BF, BF])

    def ep_down(a, t, c, f):
        return [a[0], t[0] + 0.5 * _rms(a[0], c[0])], []

    y, x1 = mm_call(tag + "_down", [[(act, wd, False, False)]], S, D_MODEL, D_FFP, tm=512, tn=1024, tk=1024,
                    epilogue=ep_down, outs=[BF, F32], tiles=[x0], cols=[g_out])
    return x1, dict(x0=x0, h=h, gate=gate, up=up, act=act, y=y)


def ffn_bwd(tag, dx1, sv, g_in, g_out, wg, wu, wd):
    S = dx1.shape[0]

    def post(i, t, f):
        dx, dg = _rms_bwd(t[1].astype(F32), f[0], 0.5 * t[0])
        return [dx], [dg]

    da, dg_out = rows_call(tag + "_bpost", post, S, 512, [dx1, sv['y']], [g_out], [(D_MODEL, BF)], [(1, D_MODEL)])

    def ep_act(a, t, c, f):
        g, u = t[0].astype(F32), t[1].astype(F32)
        sg = jax.nn.sigmoid(g)
        return [a[0] * u * (sg * (1.0 + g * (1.0 - sg))), a[0] * (g * sg)], []

    dgate, dup = mm_call(tag + "_bact", [[(da, wd, False, True)]], S, D_FFP, D_MODEL, tm=1024, tn=512, tk=1024,
                         epilogue=ep_act, outs=[BF, BF], tiles=[sv['gate'], sv['up']])
    dwd, = mm_call(tag + "_dwd", [[(sv['act'], da, True, False)]], D_FFP, D_MODEL, S, tm=1024, tn=1024, tk=1024,
                   **_plain(F32))
    dwg, dwu = mm_call(tag + "_dwgu", [[(sv['h'], dgate, True, False)], [(sv['h'], dup, True, False)]],
                       D_MODEL, D_FFP, S, tm=1024, tn=1024, tk=1024,
                       epilogue=lambda a, t, c, f: ([a[0], a[1]], []), outs=[F32, F32])

    def ep_in(a, t, c, f):
        dx, dg = _rms_bwd(t[0], c[0], a[0])
        return [t[1] + dx], [dg]

    (dx0, dg_in) = mm_call(tag + "_bin", [[(dgate, wg, False, True), (dup, wu, False, True)]], S, D_MODEL, D_FFP,
                           tm=512, tn=1024, tk=1024, epilogue=ep_in, outs=[F32], tiles=[sv['x0'], dx1], cols=[g_in],
                           reds=[1])
    return dx0, dict(g_in=dg_in, g_out=dg_out, wg=dwg, wu=dwu, wd=dwd)


def mem_kv_fwd(tag, mem, g6, w_kv):
    def body(mem_ref, g_ref, w_ref, kv_ref):
        kv_ref[...] = _dot(_rms(mem_ref[...], g_ref[...]).astype(BF), w_ref[...])

    return pl.pallas_call(body, out_shape=SDS((N_MEM, 2 * MEM_WIDTH), F32), name=tag + "_memkv",
                          compiler_params=pltpu.CompilerParams(vmem_limit_bytes=V7X_VMEM_LIMIT))(mem, g6, w_kv)


def mem_kv_bwd(tag, mem, g6, w_kv, dkv):
    def body(mem_ref, g_ref, w_ref, dkv_ref, dw_ref, dg_ref):
        x = mem_ref[...]
        r = lax.rsqrt(jnp.mean(x * x, axis=-1, keepdims=True) + EPS)
        xh = x * r
        d = dkv_ref[...].astype(BF)
        dw_ref[...] = _dot((xh * g_ref[...]).astype(BF), d, 0, 0)
        dmem_n = _dot(d, w_ref[...], 1, 1)
        dg_ref[...] = jnp.sum(dmem_n * xh, axis=0, keepdims=True)

    return pl.pallas_call(body, out_shape=(SDS((D_MODEL, 2 * MEM_WIDTH), F32), SDS((1, D_MODEL), F32)),
                          name=tag + "_memkvb",
                          compiler_params=pltpu.CompilerParams(vmem_limit_bytes=V7X_VMEM_LIMIT))(mem, g6, w_kv, dkv)


def _block_diag_kv(kv):
    k = kv[:, :MEM_WIDTH].reshape(N_MEM, MEM_HEADS, HEAD_DIM)
    v = kv[:, MEM_WIDTH:].reshape(N_MEM, MEM_HEADS, HEAD_DIM)
    eye = jnp.eye(MEM_HEADS, dtype=kv.dtype)
    kbd = jnp.einsum('nhd,hg->hdgn', k, eye).reshape(MEM_WIDTH, MEM_HEADS * N_MEM)
    vbd = jnp.einsum('nhd,hg->hngd', v, eye).reshape(MEM_HEADS * N_MEM, MEM_WIDTH)
    return kbd.astype(BF), vbd.astype(BF)


def _mem_probs(q, kbd):
    s = _dot(q, kbd) * (HEAD_DIM ** -0.5)
    ps = []
    for h in range(MEM_HEADS):
        sh = s[:, h * N_MEM:(h + 1) * N_MEM]
        e = jnp.exp(sh - jnp.max(sh, axis=-1, keepdims=True))
        ps.append(e / jnp.sum(e, axis=-1, keepdims=True))
    return ps


def mem_attn_fwd(tag, q_mem, kbd, vbd):
    S = q_mem.shape[0]

    def fn(i, t, f):
        p = jnp.concatenate(_mem_probs(t[0], f[0]), axis=1).astype(BF)
        return [_dot(p, f[1])], []

    return rows_call(tag + "_mem", fn, S, 512, [q_mem], [kbd, vbd], [(MEM_WIDTH, BF)])[0]


def mem_attn_bwd(tag, q_mem, dmo, kbd, vbd):
    S = q_mem.shape[0]

    def fn(i, t, f):
        q, do = t
        ps = _mem_probs(q, f[0])
        dp = _dot(do, f[1], 1, 1)
        dss = []
        for h in range(MEM_HEADS):
            dph = dp[:, h * N_MEM:(h + 1) * N_MEM]
            dss.append(ps[h] * (dph - jnp.sum(dph * ps[h], axis=-1, keepdims=True)) * (HEAD_DIM ** -0.5))
        ds = jnp.concatenate(dss, axis=1).astype(BF)
        p = jnp.concatenate(ps, axis=1).astype(BF)
        return [_dot(ds, f[0], 1, 1)], [_dot(q, ds, 0, 0), _dot(p, do, 0, 0)]

    dq, dkbd, dvbd = rows_call(tag + "_memb", fn, S, 512, [q_mem, dmo], [kbd, vbd], [(MEM_WIDTH, BF)],
                               [(MEM_WIDTH, MEM_HEADS * N_MEM), (MEM_HEADS * N_MEM, MEM_WIDTH)])
    dk = jnp.einsum('hdhn->nhd', dkbd.reshape(MEM_HEADS, HEAD_DIM, MEM_HEADS, N_MEM)).reshape(N_MEM, MEM_WIDTH)
    dv = jnp.einsum('hnhd->nhd', dvbd.reshape(MEM_HEADS, N_MEM, MEM_HEADS, HEAD_DIM)).reshape(N_MEM, MEM_WIDTH)
    return dq, jnp.concatenate([dk, dv], axis=1)


def _rope_tables(positions):
    half = MLA_ROPE // 2
    inv = ROPE_THETA ** (-jnp.arange(half, dtype=F32) / half)
    ang = positions.astype(F32)[:, None] * inv
    cos, sin = jnp.cos(ang), jnp.sin(ang)
    cos32, sin32 = jnp.concatenate([cos, cos], 1), jnp.concatenate([sin, sin], 1)
    idx = jnp.arange(MLA_ROPE)
    p32 = (jnp.where((idx[:, None] == idx[None, :] + half), -1.0, 0.0)
           + jnp.where((idx[:, None] + half == idx[None, :]), 1.0, 0.0)).astype(F32)
    S = positions.shape[0]
    ones = jnp.ones((S, MLA_NOPE), F32)
    cq = jnp.tile(jnp.concatenate([ones, cos32], 1), (1, MLA_HEADS))
    sq = jnp.tile(jnp.concatenate([0.0 * ones, sin32], 1), (1, MLA_HEADS))
    pq = jnp.zeros((MLA_QK, MLA_QK), F32).at[MLA_NOPE:, MLA_NOPE:].set(p32)
    pq = jnp.kron(jnp.eye(MLA_HEADS, dtype=F32), pq)
    return dict(cos32=cos32, sin32=sin32, p32=p32.astype(BF), p32t=p32.T.astype(BF), cq=cq, sq=sq,
                pq=pq.astype(BF), pqt=pq.T.astype(BF))


def _cargo_specs(cargo):
    if cargo is None:
        return 0, [], [], [], []
    arrays, scatter = cargo
    n = len(arrays)
    hbm = pl.BlockSpec(memory_space=pltpu.HBM)
    out_shape = [SDS(t.shape if scatter else (N_DEV,) + t.shape, t.dtype) for t in arrays]
    scratch = [pltpu.SemaphoreType.DMA((n, N_DEV - 1)), pltpu.SemaphoreType.DMA((n, N_DEV - 1)),
               pltpu.SemaphoreType.DMA((n,))]
    return n, [hbm] * n, [hbm] * n, out_shape, scratch


def flash_fwd(tag, qh, kh, vh1, dv, tq=1024, tk=2048, cargo=None):
    H, S, dq = qh.shape
    dv1 = vh1.shape[2]
    tq, tk = min(tq, S), min(tk, S)
    nk = S // tk
    nc, c_in, c_out, c_shape, c_scratch = _cargo_specs(cargo)

    def body(*refs):
        q_ref, k_ref, v_ref = refs[:3]
        o_ref, lse_ref = refs[3 + nc:5 + nc]
        m_s, acc_s = refs[5 + 2 * nc:7 + 2 * nc]
        h, i, j = pl.program_id(0), pl.program_id(1), pl.program_id(2)
        if nc:
            start, wait = _exchange_ops(refs[3:3 + nc], refs[5 + nc:5 + 2 * nc], *refs[7 + 2 * nc:], cargo[1])
            pl.when((h == 0) & (i == 0) & (j == 0))(start)

        @pl.when(j == 0)
        def _():
            m_s[...] = jnp.full(m_s.shape, NEG, F32)
            acc_s[...] = jnp.zeros(acc_s.shape, F32)

        s = _dot(q_ref[0], k_ref[0], 1, 1)
        m_old = m_s[...]
        m_new = jnp.maximum(m_old, jnp.max(s, axis=-1, keepdims=True))
        p = jnp.exp2((s - m_new).astype(BF))
        acc_s[...] = jnp.exp2(m_old - m_new) * acc_s[...] + _dot(p, v_ref[0])
        m_s[...] = m_new

        @pl.when(j == nk - 1)
        def _():
            acc = acc_s[...]
            l = acc[:, dv:dv + 1]
            o_ref[0] = (acc[:, :dv] / l).astype(o_ref.dtype)
            lse_ref[0] = m_s[...] + jnp.log2(l)

        if nc:
            pl.when((h == H - 1) & (i == S // tq - 1) & (j == nk - 1))(wait)

    res = pl.pallas_call(
        body, grid=(H, S // tq, nk),
        in_specs=[pl.BlockSpec((1, tq, dq), lambda h, i, j: (h, i, 0)),
                  pl.BlockSpec((1, tk, dq), lambda h, i, j: (h, j, 0)),
                  pl.BlockSpec((1, tk, dv1), lambda h, i, j: (h, j, 0))] + c_in,
        out_specs=[pl.BlockSpec((1, tq, dv), lambda h, i, j: (h, i, 0)),
                   pl.BlockSpec((1, tq, 1), lambda h, i, j: (h, i, 0))] + c_out,
        out_shape=[SDS((H, S, dv), BF), SDS((H, S, 1), F32)] + c_shape,
        scratch_shapes=[pltpu.VMEM((tq, 1), F32), pltpu.VMEM((tq, dv1), F32)] + c_scratch,
        name=tag + "_flash", compiler_params=_params(("arbitrary", "arbitrary", "arbitrary")))(
            qh, kh, vh1, *(cargo[0] if nc else []))
    return res[0], res[1], list(res[2:])


def flash_bwd(tag, qh, kh, vh, doh, lse_row, delta_row, tq=1024, tk=1024, cargo=None):
    H, S, dq = qh.shape
    dv = vh.shape[2]
    tq, tk = min(tq, S), min(tk, S)
    nc, c_in, c_out, c_shape, c_scratch = _cargo_specs(cargo)

    def body(*refs):
        k_ref, v_ref, q_ref, do_ref, lse_ref, dl_ref = refs[:6]
        dk_ref, dv_ref, dq_ref = refs[6 + nc:9 + nc]
        h, j, i = pl.program_id(0), pl.program_id(1), pl.program_id(2)
        if nc:
            start, wait = _exchange_ops(refs[6:6 + nc], refs[9 + nc:9 + 2 * nc], *refs[9 + 2 * nc:], cargo[1])
            pl.when((h == 0) & (j == 0) & (i == 0))(start)
        k, v, q, do = k_ref[0], v_ref[0], q_ref[0], do_ref[0]
        pt = jnp.exp2((_dot(k, q, 1, 1) - lse_ref[0]).astype(BF))
        dst = pt * (_dot(v, do, 1, 1) - dl_ref[0]).astype(BF)
        dvc = _dot(pt, do)
        dkc = _dot(dst, q) * (1.0 / LOG2E)
        dqc = _dot(dst, k, 0, 0)

        @pl.when(i == 0)
        def _():
            dk_ref[0] = dkc
            dv_ref[0] = dvc

        @pl.when(i > 0)
        def _():
            dk_ref[0] += dkc
            dv_ref[0] += dvc

        rows = pl.ds(pl.multiple_of(i * tq, tq), tq)

        @pl.when(j == 0)
        def _():
            dq_ref[0, rows, :] = dqc

        @pl.when(j > 0)
        def _():
            dq_ref[0, rows, :] += dqc

        if nc:
            pl.when((h == H - 1) & (j == S // tk - 1) & (i == S // tq - 1))(wait)

    res = pl.pallas_call(
        body, grid=(H, S // tk, S // tq),
        in_specs=[pl.BlockSpec((1, tk, dq), lambda h, j, i: (h, j, 0)),
                  pl.BlockSpec((1, tk, dv), lambda h, j, i: (h, j, 0)),
                  pl.BlockSpec((1, tq, dq), lambda h, j, i: (h, i, 0)),
                  pl.BlockSpec((1, tq, dv), lambda h, j, i: (h, i, 0)),
                  pl.BlockSpec((1, 1, tq), lambda h, j, i: (h, 0, i)),
                  pl.BlockSpec((1, 1, tq), lambda h, j, i: (h, 0, i))] + c_in,
        out_specs=[pl.BlockSpec((1, tk, dq), lambda h, j, i: (h, j, 0)),
                   pl.BlockSpec((1, tk, dv), lambda h, j, i: (h, j, 0)),
                   pl.BlockSpec((1, S, dq), lambda h, j, i: (h, 0, 0))] + c_out,
        out_shape=[SDS((H, S, dq), F32), SDS((H, S, dv), F32), SDS((H, S, dq), F32)] + c_shape,
        scratch_shapes=c_scratch, name=tag + "_flashb",
        compiler_params=_params(("arbitrary", "arbitrary", "arbitrary")))(
            kh, vh, qh, doh, lse_row, delta_row, *(cargo[0] if nc else []))
    return res[0], res[1], res[2], list(res[3:])


def _to_heads(t, H):
    return t.reshape(t.shape[0], H, -1).transpose(1, 0, 2)


def _from_heads(t):
    return t.transpose(1, 0, 2).reshape(t.shape[1], -1)


def mla_fwd(tag, z, rope, q_g, kv_g, w_uq, w_ukv, cargo=None):
    S = z.shape[0]
    o1, o2, o3 = MLA_Q_LORA, MLA_Q_LORA + MLA_KV_LORA, MLA_Q_LORA + MLA_KV_LORA + MLA_ROPE
    c_q, c_kv, k_r, q_mem = z[:, :o1], z[:, o1:o2], z[:, o2:o3], z[:, o3:]

    def pre(i, t, f):
        kr = t[2].astype(F32)
        krot = kr * t[3] + _dot(t[2], f[2]) * t[4]
        return [_rms(t[0].astype(F32), f[0]), _rms(t[1].astype(F32), f[1]), krot], []

    qn, kvn, krot = rows_call(tag + "_pre", pre, S, 512, [c_q, c_kv, k_r, rope['cos32'], rope['sin32']],
                              [q_g, kv_g, rope['p32']], [(MLA_Q_LORA, BF), (MLA_KV_LORA, BF), (MLA_ROPE, BF)])

    qscale = MLA_QK ** -0.5 * LOG2E

    def ep_q(a, t, c, f):
        return [(a[0] * t[0] + _dot(a[0].astype(BF), f[0]) * t[1]) * qscale], []

    nq = MLA_HEADS * MLA_QK
    q, = mm_call(tag + "_uq", [[(qn, w_uq, False, False)]], S, nq, MLA_Q_LORA, tm=512, tn=nq, tk=MLA_Q_LORA,
                 epilogue=ep_q, outs=[BF], tiles=[rope['cq'], rope['sq']], fulls=[rope['pq']])
    nkv = MLA_HEADS * (MLA_NOPE + MLA_V)
    kv, = mm_call(tag + "_ukv", [[(kvn, w_ukv, False, False)]], S, nkv, MLA_KV_LORA, tm=1024, tn=nkv, tk=MLA_KV_LORA,
                  **_plain(BF))
    qh = _to_heads(q, MLA_HEADS)
    kv3 = kv.reshape(S, MLA_HEADS, MLA_NOPE + MLA_V)
    kh = jnp.concatenate([kv3[:, :, :MLA_NOPE].transpose(1, 0, 2),
                          jnp.broadcast_to(krot[None], (MLA_HEADS, S, MLA_ROPE))], axis=2)
    vh = kv3[:, :, MLA_NOPE:].transpose(1, 0, 2)
    ones = jnp.ones((MLA_HEADS, S, 1), BF)
    vh1 = jnp.concatenate([vh, ones, jnp.zeros((MLA_HEADS, S, 128 - MLA_V - 1), BF)], axis=2)
    oh, lse, got = flash_fwd(tag, qh, kh, vh1, MLA_V, cargo=cargo)
    sv = dict(c_q=c_q, c_kv=c_kv, k_r=k_r, qn=qn, kvn=kvn, qh=qh, kh=kh, vh=vh, oh=oh, lse=lse)
    return _from_heads(oh), q_mem, sv, got


def mla_bwd(tag, do, sv, rope, q_g, kv_g, w_uq, w_ukv, cargo=None):
    S = do.shape[0]
    H = MLA_HEADS
    doh = _to_heads(do, H)
    delta, = hrows_call(tag + "_delta",
                        lambda t: [jnp.sum(t[0].astype(F32) * t[1].astype(F32), axis=-1, keepdims=True)],
                        H, S, 1024, [doh, sv['oh']], [(1, F32)])
    dkh, dvh, dqh, got = flash_bwd(tag, sv['qh'], sv['kh'], sv['vh'], doh, sv['lse'].reshape(H, 1, S),
                                   delta.reshape(H, 1, S), cargo=cargo)
    nq = H * MLA_QK
    dq_rot = _from_heads(dqh)
    scale = MLA_QK ** -0.5

    def unrope(i, t, f):
        d = t[0] * scale
        return [d * t[1] + _dot((d * t[2]).astype(BF), f[0])], []

    dq, = rows_call(tag + "_unrope", unrope, S, 512, [dq_rot, rope['cq'], rope['sq']], [rope['pqt']], [(nq, BF)])

    def ep_norm(a, t, c, f):
        dx, dg = _rms_bwd(t[0].astype(F32), c[0], a[0])
        return [dx], [dg]

    dc_q, dq_g = mm_call(tag + "_bqn", [[(dq, w_uq, False, True)]], S, MLA_Q_LORA, nq, tm=1024, tn=MLA_Q_LORA, tk=nq,
                         epilogue=ep_norm, outs=[BF], tiles=[sv['c_q']], cols=[q_g], reds=[1])
    dw_uq, = mm_call(tag + "_dwuq", [[(sv['qn'], dq, True, False)]], MLA_Q_LORA, nq, S, tm=MLA_Q_LORA, tn=nq, tk=1024,
                     **_plain(F32))
    nkv = H * (MLA_NOPE + MLA_V)
    dkv = jnp.concatenate([dkh[:, :, :MLA_NOPE], dvh], axis=2).transpose(1, 0, 2).reshape(S, nkv).astype(BF)
    dc_kv, dkv_g = mm_call(tag + "_bkvn", [[(dkv, w_ukv, False, True)]], S, MLA_KV_LORA, nkv, tm=1024, tn=MLA_KV_LORA,
                           tk=nkv, epilogue=ep_norm, outs=[BF], tiles=[sv['c_kv']], cols=[kv_g], reds=[1])
    dw_ukv, = mm_call(tag + "_dwukv", [[(sv['kvn'], dkv, True, False)]], MLA_KV_LORA, nkv, S, tm=MLA_KV_LORA, tn=nkv,
                      tk=1024, **_plain(F32))
    dkr_heads = dkh[:, :, MLA_NOPE:].transpose(1, 0, 2).reshape(S, H * MLA_ROPE)
    hsum = jnp.tile(jnp.eye(MLA_ROPE, dtype=F32), (H, 1))

    def krope(i, t, f):
        d = jnp.dot(t[0], f[0], preferred_element_type=F32, precision=lax.Precision.HIGHEST)
        return [d * t[1] + _dot((d * t[2]).astype(BF), f[1])], []

    dk_r, = rows_call(tag + "_bkr", krope, S, 512, [dkr_heads, rope['cos32'], rope['sin32']], [hsum, rope['p32t']],
                      [(MLA_ROPE, BF)])
    grads = dict(a_q_norm=dq_g, a_kv_norm=dkv_g, a_w_uq=dw_uq, a_w_ukv=dw_ukv)
    return [dc_q, dc_kv, dk_r], grads, got


def _dil_perm(t, dil):
    H, S, d = t.shape
    return t.reshape(H, S // dil, dil, d).transpose(0, 2, 1, 3).reshape(H, S, d)


def _dil_unperm(t, dil):
    H, S, d = t.shape
    return t.reshape(H, dil, S // dil, d).transpose(0, 2, 1, 3).reshape(H, S, d)


def _band(rpos, cpos, S, seg_shift):
    dist = jnp.abs(rpos - cpos)
    valid = ((dist <= DIL_HALF) & (cpos >= 0) & (cpos < S) & (rpos >= 0) & (rpos < S)
             & (jnp.right_shift(rpos, seg_shift) == jnp.right_shift(cpos, seg_shift)))
    return dist, valid


def _halo_specs(blk, halo, n_halo_blocks, w, nh):
    r = blk // halo
    return [pl.BlockSpec((nh, halo, w), lambda h, i: (h, jnp.maximum(i * r - 1, 0), 0)),
            pl.BlockSpec((nh, blk, w), lambda h, i: (h, i, 0)),
            pl.BlockSpec((nh, halo, w), lambda h, i: (h, jnp.minimum((i + 1) * r, n_halo_blocks - 1), 0))]


def dil_fwd(tag, q, k, v, slopes, dil, tq=256):
    H, S, dh = q.shape
    nh = H
    tq = min(tq, S)
    seg_shift = int(math.log2(S // dil))
    hb = DIL_HALF
    scale = dh ** -0.5

    def body(sl_ref, q_ref, kp, kc, kn, vp, vc, vn, o_ref, lse_ref):
        i = pl.program_id(1)
        shp = (tq, tq + 2 * hb)
        qpos = i * tq + lax.broadcasted_iota(jnp.int32, shp, 0)
        kpos = i * tq - hb + lax.broadcasted_iota(jnp.int32, shp, 1)
        dist, valid = _band(qpos, kpos, S, seg_shift)
        tokens = (dist * dil).astype(F32)
        for h in range(nh):
            kcat = jnp.concatenate([kp[h], kc[h], kn[h]], axis=0)
            vcat = jnp.concatenate([vp[h], vc[h], vn[h]], axis=0)
            s = _dot(q_ref[h], kcat, 1, 1) * scale - sl_ref[h][:, :1] * tokens
            s = jnp.where(valid, s, NEG)
            m = jnp.max(s, axis=-1, keepdims=True)
            lse = m + jnp.log(jnp.sum(jnp.exp(s - m), axis=-1, keepdims=True))
            o_ref[h] = _dot(jnp.exp(s - lse).astype(BF), vcat)
            lse_ref[h] = lse

    return pl.pallas_call(
        body, grid=(H // nh, S // tq),
        in_specs=([pl.BlockSpec((nh, 1, 128), lambda h, i: (h, 0, 0)), pl.BlockSpec((nh, tq, dh), lambda h, i: (h, i, 0))]
                  + _halo_specs(tq, hb, S // hb, dh, nh) * 2),
        out_specs=[pl.BlockSpec((nh, tq, dh), lambda h, i: (h, i, 0)), pl.BlockSpec((nh, tq, 1), lambda h, i: (h, i, 0))],
        out_shape=[SDS((H, S, dh), F32), SDS((H, S, 1), F32)],
        name=tag + "_dil", compiler_params=_params(("arbitrary", "arbitrary")))(slopes, q, k, k, k, v, v, v)


def dil_bwd_q(tag, q, k, v, do, lse, deff, slopes, dil, tq=256):
    H, S, dh = q.shape
    nh = H
    tq = min(tq, S)
    seg_shift = int(math.log2(S // dil))
    hb = DIL_HALF
    scale = dh ** -0.5

    def body(sl_ref, q_ref, do_ref, lse_ref, de_ref, kp, kc, kn, vp, vc, vn, dq_ref):
        i = pl.program_id(1)
        shp = (tq, tq + 2 * hb)
        qpos = i * tq + lax.broadcasted_iota(jnp.int32, shp, 0)
        kpos = i * tq - hb + lax.broadcasted_iota(jnp.int32, shp, 1)
        dist, valid = _band(qpos, kpos, S, seg_shift)
        tokens = (dist * dil).astype(F32)
        for h in range(nh):
            kcat = jnp.concatenate([kp[h], kc[h], kn[h]], axis=0)
            vcat = jnp.concatenate([vp[h], vc[h], vn[h]], axis=0)
            s = _dot(q_ref[h], kcat, 1, 1) * scale - sl_ref[h][:, :1] * tokens
            p = jnp.where(valid, jnp.exp(jnp.where(valid, s, NEG) - lse_ref[h]), 0.0)
            dp = _dot(do_ref[h], vcat, 1, 1)
            ds = (p * (dp - de_ref[h]) * scale).astype(BF)
            dq_ref[h] = _dot(ds, kcat).astype(dq_ref.dtype)

    row = lambda w: pl.BlockSpec((nh, tq, w), lambda h, i: (h, i, 0))
    return pl.pallas_call(
        body, grid=(H // nh, S // tq),
        in_specs=([pl.BlockSpec((nh, 1, 128), lambda h, i: (h, 0, 0)), row(dh), row(dh), row(1), row(1)]
                  + _halo_specs(tq, hb, S // hb, dh, nh) * 2),
        out_specs=row(dh), out_shape=SDS((H, S, dh), BF),
        name=tag + "_dilbq",
        compiler_params=_params(("arbitrary", "arbitrary")))(slopes, q, do, lse, deff, k, k, k, v, v, v)


def dil_bwd_kv(tag, q, k, v, do, lse_row, deff_row, slopes, dil, tk=256):
    H, S, dh = q.shape
    nh = H
    tk = min(tk, S)
    seg_shift = int(math.log2(S // dil))
    hq = 128
    scale = dh ** -0.5
    r = tk // hq
    nhb = S // hq
    stat_specs = [pl.BlockSpec((nh, 1, hq), lambda h, j: (h, 0, jnp.maximum(j * r - 1, 0))),
                  pl.BlockSpec((nh, 1, tk), lambda h, j: (h, 0, j)),
                  pl.BlockSpec((nh, 1, hq), lambda h, j: (h, 0, jnp.minimum((j + 1) * r, nhb - 1)))]

    def body(sl_ref, k_ref, v_ref, qp, qc, qn, dop, doc, don, lp, lc, ln, ep, ec, en, dk_ref, dv_ref):
        j = pl.program_id(1)
        shp = (tk, tk + 2 * hq)
        kpos = j * tk + lax.broadcasted_iota(jnp.int32, shp, 0)
        qpos = j * tk - hq + lax.broadcasted_iota(jnp.int32, shp, 1)
        dist, valid = _band(kpos, qpos, S, seg_shift)
        tokens = (dist * dil).astype(F32)
        for h in range(nh):
            qcat = jnp.concatenate([qp[h], qc[h], qn[h]], axis=0)
            docat = jnp.concatenate([dop[h], doc[h], don[h]], axis=0)
            lse = jnp.concatenate([lp[h], lc[h], ln[h]], axis=1)
            deff = jnp.concatenate([ep[h], ec[h], en[h]], axis=1)
            st = _dot(k_ref[h], qcat, 1, 1) * scale - sl_ref[h][:, :1] * tokens
            pt = jnp.where(valid, jnp.exp(jnp.where(valid, st, NEG) - jnp.where(valid, lse, 0.0)), 0.0)
            dpt = _dot(v_ref[h], docat, 1, 1)
            dst = (pt * (dpt - deff) * scale).astype(BF)
            dv_ref[h] = _dot(pt.astype(BF), docat).astype(dv_ref.dtype)
            dk_ref[h] = _dot(dst, qcat).astype(dk_ref.dtype)

    cur = lambda w: pl.BlockSpec((nh, tk, w), lambda h, j: (h, j, 0))
    return pl.pallas_call(
        body, grid=(H // nh, S // tk),
        in_specs=([pl.BlockSpec((nh, 1, 128), lambda h, j: (h, 0, 0)), cur(dh), cur(dh)]
                  + _halo_specs(tk, hq, nhb, dh, nh) * 2 + stat_specs * 2),
        out_specs=[cur(dh), cur(dh)], out_shape=[SDS((H, S, dh), BF), SDS((H, S, dh), BF)],
        name=tag + "_dilbkv", compiler_params=_params(("arbitrary", "arbitrary")))(
            slopes, k, v, q, q, q, do, do, do, lse_row, lse_row, lse_row, deff_row, deff_row, deff_row)


def _alibi_slopes():
    n = len(DIL_GROUPS) * DIL_HEADS
    s = 2.0 ** (-ALIBI_MAX * (jnp.arange(n, dtype=F32) + 1.0) / n)
    return jnp.broadcast_to(s.reshape(len(DIL_GROUPS), DIL_HEADS, 1, 1), (len(DIL_GROUPS), DIL_HEADS, 1, 128))


def _merge_weights(lses):
    m = jnp.maximum(jnp.maximum(lses[0], lses[1]), lses[2])
    es = [jnp.exp(l - m) for l in lses]
    tot = es[0] + es[1] + es[2]
    return [e / tot for e in es]


def dilated_fwd(tag, z):
    S = z.shape[0]
    H = DIL_HEADS
    ng = len(DIL_GROUPS)
    zd = z[:, :DIL_QKV].reshape(S, ng, 3, H, HEAD_DIM)
    q_mem = z[:, DIL_QKV:]
    slopes = _alibi_slopes()
    qkv, outs, lses = [], [], []
    for g, (_, dil) in enumerate(DIL_GROUPS):
        t = [_dil_perm(zd[:, g, c].transpose(1, 0, 2), dil) for c in range(3)]
        o, lse = dil_fwd(f"{tag}_g{g}", t[0], t[1], t[2], slopes[g], dil)
        qkv.append(t)
        outs.append(_dil_unperm(o, dil))
        lses.append(_dil_unperm(lse, dil))

    def merge(t):
        w = _merge_weights(t[3:6])
        return [w[0] * t[0] + w[1] * t[1] + w[2] * t[2]]

    oh, = hrows_call(tag + "_merge", merge, H, S, 1024, outs + lses, [(HEAD_DIM, BF)])
    return _from_heads(oh), q_mem, dict(qkv=qkv, outs=outs, lses=lses)


def dilated_bwd(tag, do, sv):
    S = do.shape[0]
    H = DIL_HEADS
    slopes = _alibi_slopes()
    doh = _to_heads(do, H)

    def merge_b(t):
        d = t[0].astype(F32)
        w = _merge_weights(t[4:7])
        dws = [jnp.sum(d * t[1 + g], axis=-1, keepdims=True) for g in range(3)]
        c = w[0] * dws[0] + w[1] * dws[1] + w[2] * dws[2]
        return [w[g] * d for g in range(3)] + [w[g] * c for g in range(3)]

    res = hrows_call(tag + "_mergeb", merge_b, H, S, 1024, [doh] + sv['outs'] + sv['lses'],
                     [(HEAD_DIM, BF)] * 3 + [(1, F32)] * 3)
    parts = []
    for g, (_, dil) in enumerate(DIL_GROUPS):
        q, k, v = sv['qkv'][g]
        do_g = _dil_perm(res[g], dil)
        deff = _dil_perm(res[3 + g], dil)
        lse = _dil_perm(sv['lses'][g], dil)
        dq = dil_bwd_q(f"{tag}_g{g}", q, k, v, do_g, lse, deff, slopes[g], dil)
        dk, dv = dil_bwd_kv(f"{tag}_g{g}", q, k, v, do_g, lse.reshape(H, 1, S), deff.reshape(H, 1, S), slopes[g], dil)
        parts.append(jnp.stack([_dil_unperm(t, dil).transpose(1, 0, 2) for t in (dq, dk, dv)], axis=1))
    dzd = jnp.stack(parts, axis=1)
    return [dzd.reshape(S, DIL_QKV).astype(BF)]


def _halo_specs2(ts, w, S):
    r = ts // HALO
    return [pl.BlockSpec((HALO, w), lambda i: (jnp.maximum(i * r - 1, 0), 0)),
            pl.BlockSpec((ts, w), lambda i: (i, 0)),
            pl.BlockSpec((HALO, w), lambda i: (jnp.minimum((i + 1) * r, S // HALO - 1), 0))]


def _cat_rows(i, ts, S, refs):
    v = jnp.concatenate([r[...].astype(F32) for r in refs], axis=0)
    pos = i * ts - HALO + lax.broadcasted_iota(jnp.int32, (ts + 2 * HALO, 1), 0)
    return v, (pos >= 0) & (pos < S)


def _conv_taps(buf_ref, w, ts, width, flip):
    acc = None
    for k in range(width):
        off = HALO + ((width // 2 - k) if flip else (k - width // 2))
        term = buf_ref[pl.ds(off, ts), :] * w[k:k + 1, :]
        acc = term if acc is None else acc + term
    return acc


def conformer_fwd(tag, z, conv_w, conv_b, ln_g, ln_b, ts=256):
    S = z.shape[0]
    ts = min(ts, S)
    C = CONV_CH
    a, gate, q_mem = z[:, :C], z[:, C:2 * C], z[:, 2 * C:]

    def body(ap, ac, an, gp, gc, gn, w_ref, b_ref, lg_ref, lb_ref, u_ref, u1_ref, buf):
        i = pl.program_id(0)
        av, ok = _cat_rows(i, ts, S, (ap, ac, an))
        gv, _ = _cat_rows(i, ts, S, (gp, gc, gn))
        buf[...] = jnp.where(ok, av * jax.nn.sigmoid(gv), 0.0)
        u1 = _conv_taps(buf, w_ref[...], ts, CONV_WIDTH, False) + b_ref[...]
        mu = jnp.mean(u1, axis=-1, keepdims=True)
        xc = u1 - mu
        y = xc * lax.rsqrt(jnp.mean(xc * xc, axis=-1, keepdims=True) + EPS) * lg_ref[...] + lb_ref[...]
        u_ref[...] = _silu(y).astype(u_ref.dtype)
        u1_ref[...] = u1

    full = lambda t: pl.BlockSpec(t.shape, lambda i: (0, 0))
    u, u1 = pl.pallas_call(
        body, grid=(S // ts,),
        in_specs=_halo_specs2(ts, C, S) + _halo_specs2(ts, C, S) + [full(conv_w), full(conv_b), full(ln_g), full(ln_b)],
        out_specs=[pl.BlockSpec((ts, C), lambda i: (i, 0))] * 2, out_shape=[SDS((S, C), BF), SDS((S, C), F32)],
        scratch_shapes=[pltpu.VMEM((ts + 2 * HALO, C), F32)], name=tag + "_conf",
        compiler_params=_params(("arbitrary",)))(a, a, a, gate, gate, gate, conv_w, conv_b, ln_g, ln_b)
    return u, q_mem, dict(a=a, gate=gate, u1=u1)


def conformer_bwd(tag, du, sv, conv_w, conv_b, ln_g, ln_b, ts=256):
    S = du.shape[0]
    ts = min(ts, S)
    C = CONV_CH

    def ln_b_fn(i, t, f):
        u1, d = t[1], t[0].astype(F32)
        mu = jnp.mean(u1, axis=-1, keepdims=True)
        xc = u1 - mu
        r = lax.rsqrt(jnp.mean(xc * xc, axis=-1, keepdims=True) + EPS)
        xh = xc * r
        dy = d * _dsilu(xh * f[0] + f[1])
        dxh = dy * f[0]
        du1 = r * (dxh - jnp.mean(dxh, axis=-1, keepdims=True) - xh * jnp.mean(dxh * xh, axis=-1, keepdims=True))
        return [du1], [jnp.sum(dy * xh, axis=0, keepdims=True), jnp.sum(dy, axis=0, keepdims=True),
                       jnp.sum(du1, axis=0, keepdims=True)]

    du1, dln_g, dln_b, dconv_b = rows_call(tag + "_confb1", ln_b_fn, S, 512, [du, sv['u1']], [ln_g, ln_b],
                                           [(C, F32)], [(1, C), (1, C), (1, C)])
    wpad = CONV_WIDTH + 1

    def body(dp, dc, dn, ap, ac, an, gp, gc, gn, w_ref, da_ref, dg_ref, dw_ref, ubuf, dbuf):
        i = pl.program_id(0)
        av, ok = _cat_rows(i, ts, S, (ap, ac, an))
        gv, _ = _cat_rows(i, ts, S, (gp, gc, gn))
        dv, _ = _cat_rows(i, ts, S, (dp, dc, dn))
        sg = jax.nn.sigmoid(gv)
        ubuf[...] = jnp.where(ok, av * sg, 0.0)
        dbuf[...] = jnp.where(ok, dv, 0.0)
        du0 = _conv_taps(dbuf, w_ref[...], ts, CONV_WIDTH, True)
        a_c, s_c = av[HALO:HALO + ts], sg[HALO:HALO + ts]
        da_ref[...] = (du0 * s_c).astype(da_ref.dtype)
        dg_ref[...] = (du0 * a_c * s_c * (1.0 - s_c)).astype(dg_ref.dtype)
        d_c = dbuf[pl.ds(HALO, ts), :]

        @pl.when(i == 0)
        def _():
            dw_ref[...] = jnp.zeros(dw_ref.shape, F32)

        for k in range(CONV_WIDTH):
            off = HALO + k - CONV_WIDTH // 2
            dw_ref[k:k + 1, :] += jnp.sum(d_c * ubuf[pl.ds(off, ts), :], axis=0, keepdims=True)

    full = lambda t: pl.BlockSpec(t.shape, lambda i: (0, 0))
    da, dgate, dw = pl.pallas_call(
        body, grid=(S // ts,),
        in_specs=_halo_specs2(ts, C, S) * 3 + [full(conv_w)],
        out_specs=[pl.BlockSpec((ts, C), lambda i: (i, 0))] * 2 + [pl.BlockSpec((wpad, C), lambda i: (0, 0))],
        out_shape=[SDS((S, C), BF), SDS((S, C), BF), SDS((wpad, C), F32)],
        scratch_shapes=[pltpu.VMEM((ts + 2 * HALO, C), F32)] * 2, name=tag + "_confb2",
        compiler_params=_params(("arbitrary",)))(du1, du1, du1, sv['a'], sv['a'], sv['a'], sv['gate'], sv['gate'],
                                                 sv['gate'], conv_w)
    grads = dict(c_conv_w=dw[:CONV_WIDTH], c_conv_b=dconv_b, c_ln_g=dln_g, c_ln_b=dln_b)
    return [da, dgate], grads


def shortconv_fwd(tag, z, conv_w, ts=256):
    S = z.shape[0]
    ts = min(ts, S)
    C = SC_CH
    bg, cg, hx, q_mem = z[:, :C], z[:, C:2 * C], z[:, 2 * C:3 * C], z[:, 3 * C:]

    def body(b_ref, cp, cc, cn, hp, hc, hn, w_ref, o_ref, buf):
        i = pl.program_id(0)
        cv, ok = _cat_rows(i, ts, S, (cp, cc, cn))
        hv, _ = _cat_rows(i, ts, S, (hp, hc, hn))
        buf[...] = jnp.where(ok, cv * hv, 0.0)
        o_ref[...] = (b_ref[...].astype(F32) * _conv_taps(buf, w_ref[...], ts, SC_WIDTH, False)).astype(o_ref.dtype)

    o = pl.pallas_call(
        body, grid=(S // ts,),
        in_specs=[pl.BlockSpec((ts, C), lambda i: (i, 0))] + _halo_specs2(ts, C, S) * 2
        + [pl.BlockSpec(conv_w.shape, lambda i: (0, 0))],
        out_specs=pl.BlockSpec((ts, C), lambda i: (i, 0)), out_shape=SDS((S, C), BF),
        scratch_shapes=[pltpu.VMEM((ts + 2 * HALO, C), F32)], name=tag + "_sconv",
        compiler_params=_params(("arbitrary",)))(bg, cg, cg, cg, hx, hx, hx, conv_w)
    return o, q_mem, dict(bg=bg, cg=cg, hx=hx)


def shortconv_bwd(tag, do, sv, conv_w, ts=256):
    S = do.shape[0]
    ts = min(ts, S)
    C = SC_CH
    wpad = 8

    def body(dp, dc, dn, bp, bc, bn, cp, cc, cn, hp, hc, hn, w_ref, db_ref, dcg_ref, dhx_ref, dw_ref, mbuf, dbuf):
        i = pl.program_id(0)
        dov, ok = _cat_rows(i, ts, S, (dp, dc, dn))
        bv, _ = _cat_rows(i, ts, S, (bp, bc, bn))
        cv, _ = _cat_rows(i, ts, S, (cp, cc, cn))
        hv, _ = _cat_rows(i, ts, S, (hp, hc, hn))
        mbuf[...] = jnp.where(ok, cv * hv, 0.0)
        dbuf[...] = jnp.where(ok, dov * bv, 0.0)
        conv = _conv_taps(mbuf, w_ref[...], ts, SC_WIDTH, False)
        db_ref[...] = (dov[HALO:HALO + ts] * conv).astype(db_ref.dtype)
        dm = _conv_taps(dbuf, w_ref[...], ts, SC_WIDTH, True)
        dcg_ref[...] = (dm * hv[HALO:HALO + ts]).astype(dcg_ref.dtype)
        dhx_ref[...] = (dm * cv[HALO:HALO + ts]).astype(dhx_ref.dtype)
        d_c = dbuf[pl.ds(HALO, ts), :]

        @pl.when(i == 0)
        def _():
            dw_ref[...] = jnp.zeros(dw_ref.shape, F32)

        for k in range(SC_WIDTH):
            off = HALO + k - SC_WIDTH // 2
            dw_ref[k:k + 1, :] += jnp.sum(d_c * mbuf[pl.ds(off, ts), :], axis=0, keepdims=True)

    dbg, dcg, dhx, dw = pl.pallas_call(
        body, grid=(S // ts,),
        in_specs=_halo_specs2(ts, C, S) * 4 + [pl.BlockSpec(conv_w.shape, lambda i: (0, 0))],
        out_specs=[pl.BlockSpec((ts, C), lambda i: (i, 0))] * 3 + [pl.BlockSpec((wpad, C), lambda i: (0, 0))],
        out_shape=[SDS((S, C), BF)] * 3 + [SDS((wpad, C), F32)],
        scratch_shapes=[pltpu.VMEM((ts + 2 * HALO, C), F32)] * 2, name=tag + "_sconvb",
        compiler_params=_params(("arbitrary",)))(do, do, do, sv['bg'], sv['bg'], sv['bg'], sv['cg'], sv['cg'],
                                                 sv['cg'], sv['hx'], sv['hx'], sv['hx'], conv_w)
    return [dbg, dcg, dhx], dict(d_conv_w=dw[:SC_WIDTH])


MIXERS = ('a', 'b', 'c', 'd')


def _mixer_weights(W, m):
    return W[m + '_w_in'][0], W[m + '_w_out'][0]


def layer_fwd(i, x, mem, rope, W, gather=None):
    tag = f"l{i}"
    g = W['norm_g'][i]
    gr = lambda k: g[k:k + 1]
    m = MIXERS[i % len(MIXERS)]
    sv = {}
    x1, sv['ffn0'] = ffn_fwd(tag + "_f0", x, gr(0), gr(1), W['ffn_w_gate'][i, 0], W['ffn_w_up'][i, 0],
                             W['ffn_w_down'][i, 0])
    w_in = W[m + '_w_in'][0]
    S = x.shape[0]
    h2 = rms_rows(tag + "_rms2", x1, gr(2))
    n_in = w_in.shape[1]
    z, = mm_call(tag + "_in", [[(h2, w_in, False, False)]], S, n_in, D_MODEL, tm=1024,
                 tn=(256 if n_in % 256 == 0 else n_in), tk=1024, **_plain(BF))
    if m == 'a':
        o, q_mem, sv['mix'], got = mla_fwd(tag, z, rope, W['a_q_norm'], W['a_kv_norm'], W['a_w_uq'][0],
                                           W['a_w_ukv'][0], cargo=(gather[0], False) if gather else None)
        if gather:
            W = gather[1](got)
    elif m == 'b':
        o, q_mem, sv['mix'] = dilated_fwd(tag, z)
    elif m == 'c':
        o, q_mem, sv['mix'] = conformer_fwd(tag, z, W['c_conv_w'][0], W['c_conv_b'], W['c_ln_g'], W['c_ln_b'])
    else:
        o, q_mem, sv['mix'] = shortconv_fwd(tag, z, W['d_conv_w'][0])
    w_out = W[m + '_w_out'][0]
    kv = mem_kv_fwd(tag, mem, gr(6), W['mem_w_kv'][i])
    kbd, vbd = _block_diag_kv(kv)
    mo = mem_attn_fwd(tag, q_mem, kbd, vbd)
    cat = jnp.concatenate([o, mo], axis=1)
    n_out = cat.shape[1]

    def ep_out(a, t, c, f):
        return [a[0], t[0] + _rms(a[0], c[0])], []

    y2, x2 = mm_call(tag + "_out", [[(cat, w_out, False, False)]], S, D_MODEL, n_out, tm=1024, tn=1024, tk=n_out,
                     epilogue=ep_out, outs=[BF, F32], tiles=[x1], cols=[gr(3)])
    sv.update(x1=x1, h2=h2, q_mem=q_mem, kbd=kbd, vbd=vbd, cat=cat, y2=y2)
    x3, sv['ffn1'] = ffn_fwd(tag + "_f1", x2, gr(4), gr(5), W['ffn_w_gate'][i, 1], W['ffn_w_up'][i, 1],
                             W['ffn_w_down'][i, 1])
    return x3, sv, W


def layer_bwd(i, dx3, sv, mem, rope, W, scatter=None):
    tag = f"l{i}"
    g = W['norm_g'][i]
    gr = lambda k: g[k:k + 1]
    m = MIXERS[i % len(MIXERS)]
    S = dx3.shape[0]
    dx2, f1 = ffn_bwd(tag + "_f1", dx3, sv['ffn1'], gr(4), gr(5), W['ffn_w_gate'][i, 1], W['ffn_w_up'][i, 1],
                      W['ffn_w_down'][i, 1])
    w_in, w_out = W[m + '_w_in'][0], W[m + '_w_out'][0]
    n_in, n_out = w_in.shape[1], w_out.shape[0]
    got = None

    def post(ii, t, f):
        dx, dg = _rms_bwd(t[1].astype(F32), f[0], t[0])
        return [dx], [dg]

    dy2, dg3 = rows_call(tag + "_bpost", post, S, 512, [dx2, sv['y2']], [gr(3)], [(D_MODEL, BF)], [(1, D_MODEL)])
    dcat, = mm_call(tag + "_bout", [[(dy2, w_out, False, True)]], S, n_out, D_MODEL, tm=1024,
                    tn=(256 if n_out % 256 == 0 else n_out), tk=1024, **_plain(BF))
    dw_out, = mm_call(tag + "_dwout", [[(sv['cat'], dy2, True, False)]], n_out, D_MODEL, S, tm=n_out, tn=1024, tk=1024,
                      **_plain(F32))
    n_o = n_out - MEM_WIDTH
    do, dmo = dcat[:, :n_o], dcat[:, n_o:]
    dq_mem, dkv = mem_attn_bwd(tag, sv['q_mem'], dmo, sv['kbd'], sv['vbd'])
    dw_kv, dg6 = mem_kv_bwd(tag, mem, gr(6), W['mem_w_kv'][i], dkv)
    if m == 'a':
        cargo = (scatter(dict(ffn1=f1, a_w_out=dw_out[None], mem_w_kv=dw_kv)), True) if scatter else None
        dzs, mg, got = mla_bwd(tag, do, sv['mix'], rope, W['a_q_norm'], W['a_kv_norm'], W['a_w_uq'][0],
                               W['a_w_ukv'][0], cargo=cargo)
    elif m == 'b':
        dzs, mg = dilated_bwd(tag, do, sv['mix']), {}
    elif m == 'c':
        dzs, mg = conformer_bwd(tag, do, sv['mix'], W['c_conv_w'][0], W['c_conv_b'], W['c_ln_g'], W['c_ln_b'])
    else:
        dzs, mg = shortconv_bwd(tag, do, sv['mix'], W['d_conv_w'][0])
    dz = jnp.concatenate(dzs + [dq_mem], axis=1)
    dw_in, = mm_call(tag + "_dwin", [[(sv['h2'], dz, True, False)]], D_MODEL, n_in, S, tm=1024,
                     tn=(512 if n_in % 512 == 0 else (256 if n_in % 256 == 0 else n_in)), tk=1024, **_plain(F32))

    def ep_in(a, t, c, f):
        dx, dg = _rms_bwd(t[0], c[0], a[0])
        return [t[1] + dx], [dg]

    dx1, dg2 = mm_call(tag + "_bin", [[(dz, w_in, False, True)]], S, D_MODEL, n_in, tm=512, tn=1024,
                       tk=(n_in // 2 if n_in % 256 == 0 else n_in), epilogue=ep_in, outs=[F32],
                       tiles=[sv['x1'], dx2], cols=[gr(2)], reds=[1])
    dx0, f0 = ffn_bwd(tag + "_f0", dx1, sv['ffn0'], gr(0), gr(1), W['ffn_w_gate'][i, 0], W['ffn_w_up'][i, 0],
                      W['ffn_w_down'][i, 0])
    dg = jnp.concatenate([f0['g_in'], f0['g_out'], dg2, dg3, f1['g_in'], f1['g_out'], dg6], axis=0)
    grads = dict(norm_g=dg, ffn=[f0, f1], mem_w_kv=dw_kv)
    grads[m + '_w_in'] = dw_in[None]
    grads[m + '_w_out'] = dw_out[None]
    for k, v in mg.items():
        grads[k] = v if k in ('a_q_norm', 'a_kv_norm', 'c_conv_b', 'c_ln_g', 'c_ln_b') else v[None]
    return dx0, grads, got


def local_step(x, mem, positions, target, W, gather=None, scatter=None):
    S = x.shape[0]
    rope = _rope_tables(positions)
    saved = []
    for i in range(DEPTH):
        x, sv, W = layer_fwd(i, x, mem, rope, W, gather if i == 0 else None)
        saved.append(sv)

    def loss_fn(i, t, f):
        e = t[0] - t[1]
        part = jnp.sum(0.5 * jnp.mean(e * e, axis=-1, keepdims=True), axis=0, keepdims=True)
        return [e * (1.0 / D_MODEL)], [jnp.broadcast_to(part, (1, 128))]

    dx, loss = rows_call("loss", loss_fn, S, 512, [x, target], [], [(D_MODEL, F32)], [(1, 128)])
    upper, got = [], None
    for i in reversed(range(DEPTH)):
        hook = (lambda early: scatter(upper, early)) if (scatter and i == 0) else None
        dx, gr, got = layer_bwd(i, dx, saved[i], mem, rope, W, hook)
        upper.insert(0, gr)
    return loss[0, 0], dx, upper, got


def stack_grads(per_layer, names=None):
    grads = {}
    want = lambda n: names is None or n in names
    if want('norm_g'):
        grads['norm_g'] = jnp.stack([p['norm_g'] for p in per_layer])
    if want('mem_w_kv'):
        grads['mem_w_kv'] = jnp.stack([p['mem_w_kv'] for p in per_layer])
    for n, k in (('ffn_w_gate', 'wg'), ('ffn_w_up', 'wu'), ('ffn_w_down', 'wd')):
        if want(n):
            grads[n] = jnp.stack([jnp.stack([f[k] for f in p['ffn']]) for p in per_layer])
    for p in per_layer:
        for n, v in p.items():
            if n not in ('norm_g', 'mem_w_kv', 'ffn') and want(n):
                grads[n] = v
    return grads


def _local_shape(full_shape, axis):
    s = list(full_shape)
    if axis is not None:
        s[axis] //= N_DEV
    return tuple(s)


def _pad_to(n, m):
    return -(-n // m) * m


def pack_local(arrs, dtype, row_mult):
    flat = jnp.concatenate([a.reshape(-1).astype(dtype) for a in arrs])
    n = _pad_to(flat.shape[0], PACK_W * row_mult)
    return jnp.pad(flat, (0, n - flat.shape[0])).reshape(n // PACK_W, PACK_W)


def unpack_local(buf, names, shapes):
    flat = buf.reshape(-1)
    out, off = {}, 0
    for n in names:
        sz = math.prod(shapes[n])
        out[n] = flat[off:off + sz].reshape(shapes[n])
        off += sz
    return out


def unpack_gathered(buf, names, shapes):
    flat = buf.reshape(N_DEV, -1)
    out, off = {}, 0
    for n in names:
        ax = SHARD_AXIS[n]
        sz = math.prod(shapes[n])
        t = flat[:, off:off + sz].reshape((N_DEV,) + tuple(shapes[n]))
        t = jnp.moveaxis(t, 0, ax)
        full = list(shapes[n])
        full[ax] *= N_DEV
        out[n] = t.reshape(full)
        off += sz
    return out


def pack_grads(grads, names, row_mult):
    parts = []
    for n in names:
        g = grads[n].astype(F32)
        ax = SHARD_AXIS[n]
        if ax is None:
            parts.append(jnp.broadcast_to(g.reshape(1, -1), (N_DEV, g.size)))
        else:
            shp = list(g.shape)
            t = g.reshape(shp[:ax] + [N_DEV, shp[ax] // N_DEV] + shp[ax + 1:])
            parts.append(jnp.moveaxis(t, ax, 0).reshape(N_DEV, -1))
    flat = jnp.concatenate(parts, axis=1)
    n = _pad_to(flat.shape[1], PACK_W * row_mult)
    return jnp.pad(flat, ((0, 0), (0, n - flat.shape[1]))).reshape(N_DEV, n // PACK_W, PACK_W)


COL_SHARDED = ['a_w_uq', 'a_w_ukv', 'b_w_in', 'b_w_out', 'c_w_in', 'd_w_in']
ROW_SHARDED_1024 = ['a_w_out', 'c_w_out', 'd_w_out']
FLAT = SMALL + ['a_q_norm', 'a_kv_norm']


EDGE_COLS = ['a_w_uq', 'a_w_ukv']
BULK_COLS = ['b_w_in', 'b_w_out', 'c_w_in', 'd_w_in']
N_FFN = DEPTH * 2


def _pad_cols(t):
    return jnp.pad(t, ((0, 0), (0, FF_SLOT - FF_SHARD)))


def wire_weights_edge(a):
    ws = [_pad_cols(a['ffn_w_gate'][0, 0]), _pad_cols(a['ffn_w_up'][0, 0]), a['ffn_w_down'][0, 0], a['a_w_in'][0]]
    return [t.astype(BF) for t in ws + [a[n][0] for n in EDGE_COLS]]


def wire_weights_bulk(a):
    rest = lambda t, r: t.reshape(N_FFN * r, t.shape[-1])[r:]
    ws = [_pad_cols(rest(a['ffn_w_gate'], D_MODEL)), _pad_cols(rest(a['ffn_w_up'], D_MODEL)),
          jnp.concatenate([rest(a['ffn_w_down'], FF_SHARD)] + [a[n][0] for n in ROW_SHARDED_1024], axis=0),
          a['mem_w_kv'].reshape(-1, 2 * MEM_WIDTH)]
    return [t.astype(BF) for t in ws + [a[n][0] for n in BULK_COLS]]


def _unslot_cols(t):
    return t.transpose(1, 0, 2).reshape(t.shape[1], N_DEV * t.shape[2])


def _unslot_down(t, n):
    t = t.reshape(N_DEV, n, FF_SHARD, D_MODEL)
    t = jnp.pad(t, ((0, 0), (0, 0), (0, FF_SLOT - FF_SHARD), (0, 0)))
    return t.transpose(1, 0, 2, 3).reshape(n, D_FFP, D_MODEL)


def unwire_edge(g):
    W = dict(ffn_w_gate=_unslot_cols(g[0]).reshape(1, 1, D_MODEL, D_FFP),
             ffn_w_up=_unslot_cols(g[1]).reshape(1, 1, D_MODEL, D_FFP),
             ffn_w_down=_unslot_down(g[2], 1).reshape(1, 1, D_FFP, D_MODEL), a_w_in=g[3].reshape(1, D_MODEL, -1))
    for k, n in enumerate(EDGE_COLS):
        W[n] = _unslot_cols(g[4 + k])[None]
    return W


def unwire_bulk(g, edge):
    W = dict(edge)
    for k, n in enumerate(('ffn_w_gate', 'ffn_w_up')):
        t = _unslot_cols(g[k]).reshape(N_FFN - 1, D_MODEL, D_FFP)
        W[n] = jnp.concatenate([edge[n][0], t], axis=0).reshape(DEPTH, 2, D_MODEL, D_FFP)
    n_dn = (N_FFN - 1) * FF_SHARD
    dn = _unslot_down(g[2][:, :n_dn], N_FFN - 1)
    W['ffn_w_down'] = jnp.concatenate([edge['ffn_w_down'][0], dn], axis=0).reshape(DEPTH, 2, D_FFP, D_MODEL)
    rows = D_MODEL // N_DEV
    for k, n in enumerate(ROW_SHARDED_1024):
        W[n] = g[2][:, n_dn + k * rows:n_dn + (k + 1) * rows].reshape(1, D_MODEL, D_MODEL)
    W['mem_w_kv'] = g[3].reshape(N_DEV, DEPTH, rows, 2 * MEM_WIDTH).transpose(1, 0, 2, 3).reshape(DEPTH, D_MODEL, -1)
    for k, n in enumerate(BULK_COLS):
        W[n] = _unslot_cols(g[4 + k])[None]
    return W


def _slot_cols(t):
    return t.reshape(t.shape[0], N_DEV, -1).transpose(1, 0, 2)


def _slot_down(t):
    return t.reshape(N_DEV, FF_SLOT, D_MODEL)[:, :FF_SHARD]


def wire_grads_edge(ffn0, g):
    ws = [_slot_cols(ffn0['wg']), _slot_cols(ffn0['wu']), _slot_down(ffn0['wd']),
          g['a_w_in'][0].reshape(N_DEV, D_MODEL // N_DEV, -1)] + [_slot_cols(g[n][0]) for n in EDGE_COLS]
    return [t.astype(BF) for t in ws]


def wire_grads_bulk(ffns, g, mem_w_kv):
    rows = D_MODEL // N_DEV
    kv = mem_w_kv.reshape(DEPTH, N_DEV, rows, 2 * MEM_WIDTH).transpose(1, 0, 2, 3).reshape(N_DEV, DEPTH * rows, -1)
    ws = [jnp.concatenate([_slot_cols(f['wg']) for f in ffns], axis=1),
          jnp.concatenate([_slot_cols(f['wu']) for f in ffns], axis=1),
          jnp.concatenate([_slot_down(f['wd']) for f in ffns]
                          + [g[n][0].reshape(N_DEV, rows, D_MODEL) for n in ROW_SHARDED_1024], axis=1), kv]
    return [t.astype(BF) for t in ws + [_slot_cols(g[n][0]) for n in BULK_COLS]]


def _peers():
    x, y, c = lax.axis_index("x"), lax.axis_index("y"), lax.axis_index("c")
    me = 4 * x + 2 * y + c
    peers = []
    for k in range(1, N_DEV):
        px = 1 - x if k & 4 else x
        py = 1 - y if k & 2 else y
        pc = 1 - c if k & 1 else c
        peers.append(((px, py, pc), 4 * px + 2 * py + pc))
    return me, peers


def _gather_two_level(name, arrays):
    n = len(arrays)

    def body(*refs):
        srcs, outs = refs[:n], refs[n:2 * n]
        send_sems, recv_sems, local_sems = refs[2 * n:]
        x, y, c = lax.axis_index("x"), lax.axis_index("y"), lax.axis_index("c")
        slot = lambda px, py, pc: 4 * px + 2 * py + pc
        sibling = (x, y, 1 - c)
        chips = [(1 - x, y), (x, 1 - y), (1 - x, 1 - y)]

        def copy(a, k, block, to, own):
            return pltpu.make_async_remote_copy(src_ref=srcs[a] if own else outs[a].at[slot(*block)],
                                                dst_ref=outs[a].at[slot(*block)], send_sem=send_sems.at[a, k],
                                                recv_sem=recv_sems.at[a, k], device_id=to,
                                                device_id_type=pl.DeviceIdType.MESH)

        started = []
        for a in range(n):
            mine = pltpu.make_async_copy(srcs[a], outs[a].at[slot(x, y, c)], local_sems.at[a])
            mine.start()
            started.append(mine)
        sends = []
        for a in range(n):
            sends.append(copy(a, 0, (x, y, c), sibling, True))
            sends += [copy(a, 1 + j, (x, y, c), (*chip, c), True) for j, chip in enumerate(chips)]
        for cp in sends:
            cp.start()
        for j, chip in enumerate(chips):
            for a in range(n):
                copy(a, 1 + j, (*chip, c), sibling, False).wait_recv()
                fwd = copy(a, 4 + j, (*chip, c), sibling, False)
                fwd.start()
                sends.append(fwd)
        for a in range(n):
            copy(a, 0, (x, y, 1 - c), sibling, False).wait_recv()
            for j, chip in enumerate(chips):
                copy(a, 4 + j, (*chip, 1 - c), sibling, False).wait_recv()
        for cp in sends:
            cp.wait_send()
        for mine in started:
            mine.wait()

    hbm = pl.BlockSpec(memory_space=pltpu.HBM)
    return pl.pallas_call(
        body, out_shape=[SDS((N_DEV,) + t.shape, t.dtype) for t in arrays], in_specs=[hbm] * n, out_specs=[hbm] * n,
        scratch_shapes=[pltpu.SemaphoreType.DMA((n, N_DEV - 1)), pltpu.SemaphoreType.DMA((n, N_DEV - 1)),
                        pltpu.SemaphoreType.DMA((n,))], name=name)(*arrays)


def _exchange(name, arrays, scatter):
    n, c_in, c_out, c_shape, c_scratch = _cargo_specs((arrays, scatter))

    def body(*refs):
        start, wait = _exchange_ops(refs[:n], refs[n:2 * n], *refs[2 * n:], scatter)
        start()
        wait()

    return pl.pallas_call(body, out_shape=c_shape, in_specs=c_in, out_specs=c_out, scratch_shapes=c_scratch,
                          name=name)(*arrays)


def _exchange_ops(srcs, outs, send_sems, recv_sems, local_sems, scatter):
    me, peers = _peers()

    def local(a):
        return pltpu.make_async_copy(srcs[a].at[me] if scatter else srcs[a], outs[a].at[me], local_sems.at[a])

    def remote(a, k, landing):
        dev, idx = peers[k]
        return pltpu.make_async_remote_copy(src_ref=srcs[a].at[idx] if scatter else srcs[a],
                                            dst_ref=outs[a].at[idx if landing else me],
                                            send_sem=send_sems.at[a, k], recv_sem=recv_sems.at[a, k],
                                            device_id=dev, device_id_type=pl.DeviceIdType.MESH)

    pairs = [(a, k) for a in range(len(srcs)) for k in range(N_DEV - 1)]

    def start():
        for a in range(len(srcs)):
            local(a).start()
        for a, k in pairs:
            remote(a, k, False).start()

    def wait():
        for a, k in pairs:
            remote(a, k, True).wait_recv()
        for a, k in pairs:
            remote(a, k, False).wait_send()
        for a in range(len(srcs)):
            local(a).wait()

    return start, wait


ADAM_ROWS = 128


def sum_adamw(name, parts, row_off, w, m, v, tr):
    rows, wl = w.shape
    wp = parts.shape[2]
    tr = min(tr, rows)
    assert rows % tr == 0 and row_off % tr == 0, (name, rows, row_off, tr)
    off = row_off // tr

    def body(p_ref, w_ref, m_ref, v_ref, g_out, d_out, m_out, v_out):
        g = p_ref[0].astype(F32)
        for q in range(1, N_DEV):
            g = g + p_ref[q].astype(F32)
        g = g[:, :wl]
        m_new = ADAM_B1 * m_ref[...] + (1.0 - ADAM_B1) * g
        v_new = ADAM_B2 * v_ref[...] + (1.0 - ADAM_B2) * jnp.square(g)
        m_hat = m_new / (1.0 - ADAM_B1 ** ADAM_STEP)
        v_hat = v_new / (1.0 - ADAM_B2 ** ADAM_STEP)
        g_out[...] = g
        d_out[...] = -ADAM_LR * (m_hat / (jnp.sqrt(v_hat) + ADAM_EPS) + ADAM_WD * w_ref[...])
        m_out[...] = m_new
        v_out[...] = v_new

    blk = pl.BlockSpec((tr, wl), lambda i: (i, 0))
    return pl.pallas_call(
        body, grid=(rows // tr,),
        in_specs=[pl.BlockSpec((N_DEV, tr, wp), lambda i: (0, i + off, 0)), blk, blk, blk],
        out_specs=[blk] * 4, out_shape=[SDS((rows, wl), F32)] * 4, name=name,
        compiler_params=_params(("arbitrary",)))(parts, w, m, v)


def kernel(x, mem, positions, norm_g, ffn_w_gate, ffn_w_up, ffn_w_down, mem_w_kv, a_w_in, a_q_norm, a_kv_norm, a_w_uq, a_w_ukv, a_w_out, b_w_in, b_w_out, c_w_in, c_conv_w, c_conv_b, c_ln_g, c_ln_b, c_w_out, d_w_in, d_conv_w, d_w_out, loss_target, m_norm_g, m_ffn_w_gate, m_ffn_w_up, m_ffn_w_down, m_mem_w_kv, m_a_w_in, m_a_q_norm, m_a_kv_norm, m_a_w_uq, m_a_w_ukv, m_a_w_out, m_b_w_in, m_b_w_out, m_c_w_in, m_c_conv_w, m_c_conv_b, m_c_ln_g, m_c_ln_b, m_c_w_out, m_d_w_in, m_d_conv_w, m_d_w_out, v_norm_g, v_ffn_w_gate, v_ffn_w_up, v_ffn_w_down, v_mem_w_kv, v_a_w_in, v_a_q_norm, v_a_kv_norm, v_a_w_uq, v_a_w_ukv, v_a_w_out, v_b_w_in, v_b_w_out, v_c_w_in, v_c_conv_w, v_c_conv_b, v_c_ln_g, v_c_ln_b, v_c_w_out, v_d_w_in, v_d_conv_w, v_d_w_out):
    a = dict(zip(IN_NAMES, (x, mem, positions, norm_g, ffn_w_gate, ffn_w_up, ffn_w_down, mem_w_kv, a_w_in, a_q_norm, a_kv_norm, a_w_uq, a_w_ukv, a_w_out, b_w_in, b_w_out, c_w_in, c_conv_w, c_conv_b, c_ln_g, c_ln_b, c_w_out, d_w_in, d_conv_w, d_w_out, loss_target, m_norm_g, m_ffn_w_gate, m_ffn_w_up, m_ffn_w_down, m_mem_w_kv, m_a_w_in, m_a_q_norm, m_a_kv_norm, m_a_w_uq, m_a_w_ukv, m_a_w_out, m_b_w_in, m_b_w_out, m_c_w_in, m_c_conv_w, m_c_conv_b, m_c_ln_g, m_c_ln_b, m_c_w_out, m_d_w_in, m_d_conv_w, m_d_w_out, v_norm_g, v_ffn_w_gate, v_ffn_w_up, v_ffn_w_down, v_mem_w_kv, v_a_w_in, v_a_q_norm, v_a_kv_norm, v_a_w_uq, v_a_w_ukv, v_a_w_out, v_b_w_in, v_b_w_out, v_c_w_in, v_c_conv_w, v_c_conv_b, v_c_ln_g, v_c_ln_b, v_c_w_out, v_d_w_in, v_d_conv_w, v_d_w_out)))
    shapes = {n: tuple(a[n].shape) for n in WEIGHTS}
    got_edge = _gather_two_level("gather_edge", wire_weights_edge(a) + [pack_local([a[n] for n in SMALL], F32, 8)])
    common = unpack_gathered(got_edge[-1], SMALL, shapes)
    common['a_q_norm'], common['a_kv_norm'] = a['a_q_norm'], a['a_kv_norm']
    edge = unwire_edge(got_edge[:-1])

    def bulk_grads(upper, early):
        ffns = [early['ffn1']] + [f for p in upper for f in p['ffn']]
        g = dict(a_w_out=early['a_w_out'])
        for p in upper:
            g.update({n: v for n, v in p.items() if n.endswith(('_w_in', '_w_out'))})
        return wire_grads_bulk(ffns, g, jnp.stack([early['mem_w_kv']] + [p['mem_w_kv'] for p in upper]))

    W = {**unwire_bulk(_gather_two_level("gather_bulk", wire_weights_bulk(a)), edge), **common}
    loss, grad_x, per_layer, _ = local_step(a['x'][0], a['mem'][0], a['positions'][0], a['loss_target'][0], W)
    early = dict(ffn1=per_layer[0]['ffn'][1], a_w_out=per_layer[0]['a_w_out'], mem_w_kv=per_layer[0]['mem_w_kv'])
    got_bulk = _exchange("scatter_bulk", bulk_grads(per_layer[1:], early), True)
    flat_grads = stack_grads(per_layer, FLAT)
    got_late = _exchange("scatter_edge", wire_grads_edge(per_layer[0]['ffn'][0], per_layer[0])
                         + [pack_grads(flat_grads, FLAT, ADAM_ROWS)], True)
    res = {}

    def adam(n, p, row_off, rows, tr):
        view = [a[pre + n].reshape(rows, -1) for pre in ('', 'm_', 'v_')]
        res[n] = [t.reshape(shapes[n]) for t in sum_adamw("adamw_" + n, p, row_off, *view, tr)]

    both = lambda k: jnp.concatenate([got_late[k], got_bulk[k]], axis=1)
    n_ff = N_FFN * D_MODEL
    adam('ffn_w_gate', both(0), 0, n_ff, 256)
    adam('ffn_w_up', both(1), 0, n_ff, 256)
    n_dn = N_FFN * FF_SHARD
    rows_1024 = both(2)
    adam('ffn_w_down', rows_1024, 0, n_dn, 128)
    rows = D_MODEL // N_DEV
    for k, n in enumerate(ROW_SHARDED_1024):
        adam(n, rows_1024, n_dn + k * rows, rows, 128)
    adam('mem_w_kv', got_bulk[3], 0, DEPTH * rows, 128)
    adam('a_w_in', got_late[3], 0, rows, 128)
    for k, n in enumerate(EDGE_COLS):
        adam(n, got_late[4 + k], 0, shapes[n][1], 128)
    for k, n in enumerate(BULK_COLS):
        adam(n, got_bulk[4 + k], 0, shapes[n][1], 128)
    packed = [pack_local([a[pre + n] for n in FLAT], F32, ADAM_ROWS) for pre in ('', 'm_', 'v_')]
    flat = [unpack_local(r, FLAT, shapes) for r in sum_adamw("adamw_flat", got_late[-1], 0, *packed, ADAM_ROWS)]
    for n in FLAT:
        res[n] = [f[n] for f in flat]
    loss = lax.psum(loss, MESH_AXES)
    return (loss, grad_x[None], *[res[n][k] for k in range(4) for n in WEIGHTS])
```
